```python
import math
import jax, jax.numpy as jnp
from jax import lax
import numpy as np

D_MODEL = 2048
BATCH = 16
SEQ = 256
DEPTH = 2
DEC_BATCH = 8
DEC_SEQ = 4096
PAST_LEN = 256

GRID_W = 64
N_MIXERS = 2
N_ATTN_LAYERS = (DEPTH + 1) // 2
N_HYENA_LAYERS = DEPTH // 2
HEAD_DIM = 128
N_HEADS = D_MODEL // HEAD_DIM
N_KV_HEADS = 4
GROUP = N_HEADS // N_KV_HEADS
QKV_DIM = (N_HEADS + 2 * N_KV_HEADS) * HEAD_DIM
WINDOW = 128
BLOCK = 128
ROPE_THETA = 10000.0
ROPE_PAIRS = HEAD_DIM // 4
HY_ORDER = 2
HY_BANDS = 16
HY_EMB = 1 + 2 * HY_BANDS
HY_FH = 64
HY_MIN_DECAY = math.log(1e-2) / 1.5
HY_MAX_DECAY = math.log(1e-2) / 0.3
D_FF = 5632
EPS = 1e-6
NEG_INF = -1e30

kernel_name = "hybrid_dit_window_gqa_hyena_step"


def _rmsnorm(x, g):
    xf = x.astype(jnp.float32)
    y = xf * lax.rsqrt(jnp.mean(xf * xf, axis=-1, keepdims=True) + EPS)
    return (y * g).astype(x.dtype)


def _modulate(x, g, shift, scale):
    return _rmsnorm(x, g) * (1.0 + scale) + shift


def _dwconv3(x, w):
    xp = jnp.pad(x, ((0, 0), (1, 1), (0, 0)))
    return xp[:, :-2] * w[0] + xp[:, 1:-1] * w[1] + xp[:, 2:] * w[2]


def _qkv(h, w):
    B, L, _ = h.shape
    qkv = h @ w
    q = qkv[..., :N_HEADS * HEAD_DIM].reshape(B, L, N_HEADS, HEAD_DIM)
    k = qkv[..., N_HEADS * HEAD_DIM:(N_HEADS + N_KV_HEADS) * HEAD_DIM].reshape(B, L, N_KV_HEADS, HEAD_DIM)
    v = qkv[..., (N_HEADS + N_KV_HEADS) * HEAD_DIM:].reshape(B, L, N_KV_HEADS, HEAD_DIM)
    return q, k, v


def _axial_rope_angles(L):
    t = jnp.arange(L)
    row = (t // GRID_W).astype(jnp.float32)
    col = (t % GRID_W).astype(jnp.float32)
    inv = ROPE_THETA ** (-jnp.arange(ROPE_PAIRS, dtype=jnp.float32) / ROPE_PAIRS)
    return row[:, None] * inv, col[:, None] * inv


def _rotate(x, ang):
    x1, x2 = jnp.split(x, 2, axis=-1)
    cos = jnp.cos(ang)[None, :, None, :]
    sin = jnp.sin(ang)[None, :, None, :]
    return jnp.concatenate([x1 * cos - x2 * sin, x2 * cos + x1 * sin], axis=-1).astype(x.dtype)


def _axial_rope(x, ang_row, ang_col):
    half = HEAD_DIM // 2
    return jnp.concatenate([_rotate(x[..., :half], ang_row), _rotate(x[..., half:], ang_col)], axis=-1)


def _context_attention(q, k, v, sink):
    B, C = q.shape[:2]
    qg = q.reshape(B, C, N_KV_HEADS, GROUP, HEAD_DIM)
    s = jnp.einsum("bqhgd,bchd->bhgqc", qg, k).astype(jnp.float32) * (HEAD_DIM ** -0.5)
    sk = sink.astype(jnp.float32).reshape(N_KV_HEADS, GROUP)[None, :, :, None, None]
    m = jnp.maximum(jnp.max(s, axis=-1, keepdims=True), sk)
    p = jnp.exp(s - m)
    denom = jnp.sum(p, axis=-1, keepdims=True) + jnp.exp(sk - m)
    o = jnp.einsum("bhgqc,bchd->bqhgd", (p / denom).astype(v.dtype), v)
    return o.reshape(B, C, N_HEADS * HEAD_DIM)


def _latent_attention(q, k, v, ck, cv, sink):
    B, L = q.shape[:2]
    nb = L // BLOCK
    qb = q.reshape(B, nb, BLOCK, N_KV_HEADS, GROUP, HEAD_DIM)
    pad = ((0, 0), (BLOCK, BLOCK), (0, 0), (0, 0))
    kp = jnp.pad(k, pad).reshape(B, nb + 2, BLOCK, N_KV_HEADS, HEAD_DIM)
    vp = jnp.pad(v, pad).reshape(B, nb + 2, BLOCK, N_KV_HEADS, HEAD_DIM)
    kw = jnp.concatenate([kp[:, :-2], kp[:, 1:-1], kp[:, 2:]], axis=2)
    vw = jnp.concatenate([vp[:, :-2], vp[:, 1:-1], vp[:, 2:]], axis=2)
    scale = HEAD_DIM ** -0.5
    sw = jnp.einsum("bnqhgd,bnjhd->bnhgqj", qb, kw).astype(jnp.float32) * scale
    sc = jnp.einsum("bnqhgd,bchd->bnhgqc", qb, ck).astype(jnp.float32) * scale
    blk = jnp.arange(nb)[:, None, None] * BLOCK
    qpos = blk + jnp.arange(BLOCK)[None, :, None]
    kpos = blk - BLOCK + jnp.arange(3 * BLOCK)[None, None, :]
    valid = (jnp.abs(qpos - kpos) <= WINDOW) & (kpos >= 0) & (kpos < L)
    sw = jnp.where(valid[None, :, None, None], sw, NEG_INF)
    sk = sink.astype(jnp.float32).reshape(N_KV_HEADS, GROUP)[None, None, :, :, None, None]
    m = jnp.maximum(jnp.maximum(jnp.max(sw, axis=-1, keepdims=True), jnp.max(sc, axis=-1, keepdims=True)), sk)
    pw = jnp.exp(sw - m)
    pc = jnp.exp(sc - m)
    denom = jnp.sum(pw, axis=-1, keepdims=True) + jnp.sum(pc, axis=-1, keepdims=True) + jnp.exp(sk - m)
    o = (jnp.einsum("bnhgqj,bnjhd->bnqhgd", (pw / denom).astype(v.dtype), vw)
         + jnp.einsum("bnhgqc,bchd->bnqhgd", (pc / denom).astype(cv.dtype), cv))
    return o.reshape(B, L, N_HEADS * HEAD_DIM)


def _hyena_filters(L, w_f1, b_f1, w_f2, b_f2, w_f3, freq):
    t = jnp.linspace(0.0, 1.0, L, dtype=jnp.float32)[:, None]
    w = 2.0 * math.pi * jnp.arange(L, dtype=jnp.float32)[:, None] / L
    f = jnp.linspace(1e-4, HY_BANDS - 1, HY_BANDS, dtype=jnp.float32)[None, :]
    feat = jnp.concatenate([t, jnp.cos(f * w), -jnp.sin(f * w)], axis=-1)
    h = jnp.sin(freq * (feat @ w_f1 + b_f1))
    h = jnp.sin(freq * (h @ w_f2 + b_f2))
    h = (h @ w_f3).astype(jnp.float32).reshape(L, 2, HY_ORDER, D_MODEL)
    deltas = jnp.linspace(HY_MIN_DECAY, HY_MAX_DECAY, D_MODEL, dtype=jnp.float32)
    decay = jnp.exp(-t[:, :, None] * jnp.abs(deltas))
    h_fwd = h[:, 0] * decay
    h_bwd = h[:, 1] * decay
    zero = jnp.zeros((1, HY_ORDER, D_MODEL), jnp.float32)
    return jnp.concatenate([h_fwd, zero, h_bwd[1:][::-1]], axis=0)


def _fft_conv(z, filt_f, bias):
    L = z.shape[1]
    zf = z.astype(jnp.float32)
    y = jnp.fft.irfft(jnp.fft.rfft(zf, n=2 * L, axis=1) * filt_f[None], n=2 * L, axis=1)[:, :L]
    return (y + zf * bias).astype(z.dtype)


def _hyena(u, w_in, conv_w, w_f1, b_f1, w_f2, b_f2, w_f3, freq, bias, w_out):
    L = u.shape[1]
    x1, x2, v = jnp.split(_dwconv3(u @ w_in, conv_w), 3, axis=-1)
    filt_f = jnp.fft.rfft(_hyena_filters(L, w_f1, b_f1, w_f2, b_f2, w_f3, freq), axis=0)
    z = x1 * _fft_conv(v, filt_f[:, 0], bias[0])
    z = x2 * _fft_conv(z, filt_f[:, 1], bias[1])
    return z @ w_out


def _conv_ffn(h, w_up, conv_w, w_down):
    a, b = jnp.split(h @ w_up, 2, axis=-1)
    return (jax.nn.silu(_dwconv3(a, conv_w)) * b) @ w_down


def setup_inputs(seed: int = 0) -> dict:
    key = jax.random.key(seed)
    ks = jax.random.split(key, 32)
    nrm = lambda k, shape, s: jax.random.normal(k, shape, jnp.float32) * s
    centre = jnp.array([0.0, 1.0, 0.0], jnp.float32)[:, None]
    D = D_MODEL
    return {
        "x_prompt": nrm(ks[0], (BATCH, SEQ, D), 1.0),
        "x_sample": nrm(ks[1], (DEC_BATCH, DEC_SEQ, D), 1.0),
        "cache_k": nrm(ks[2], (DEC_BATCH, N_ATTN_LAYERS, PAST_LEN, N_KV_HEADS, HEAD_DIM), 1.0),
        "cache_v": nrm(ks[3], (DEC_BATCH, N_ATTN_LAYERS, PAST_LEN, N_KV_HEADS, HEAD_DIM), 1.0),
        "c": nrm(ks[4], (DEC_BATCH, D), 1.0),
        "c_ctx": nrm(ks[5], (D,), 1.0),
        "w_mod": nrm(ks[6], (DEPTH, D, 6 * D), 0.5 * D ** -0.5),
        "b_mod": nrm(ks[7], (DEPTH, 6 * D), 0.02),
        "norm_mix": 1.0 + nrm(ks[8], (DEPTH, D), 0.02),
        "norm_ffn": 1.0 + nrm(ks[9], (DEPTH, D), 0.02),
        "norm_final": 1.0 + nrm(ks[10], (D,), 0.02),
        "w_qkv": nrm(ks[11], (N_ATTN_LAYERS, D, QKV_DIM), D ** -0.5),
        "w_o": nrm(ks[12], (N_ATTN_LAYERS, N_HEADS * HEAD_DIM, D), (N_HEADS * HEAD_DIM) ** -0.5),
        "attn_sink": nrm(ks[13], (N_ATTN_LAYERS, N_HEADS), 0.5),
        "hy_w_in": nrm(ks[14], (N_HYENA_LAYERS, D, 3 * D), D ** -0.5),
        "hy_conv": centre + nrm(ks[15], (N_HYENA_LAYERS, 3, 3 * D), 0.3),
        "hy_w_f1": nrm(ks[16], (N_HYENA_LAYERS, HY_EMB, HY_FH), HY_EMB ** -0.5),
        "hy_b_f1": nrm(ks[17], (N_HYENA_LAYERS, HY_FH), 0.1),
        "hy_w_f2": nrm(ks[18], (N_HYENA_LAYERS, HY_FH, HY_FH), HY_FH ** -0.5),
        "hy_b_f2": nrm(ks[19], (N_HYENA_LAYERS, HY_FH), 0.1),
        "hy_w_f3": nrm(ks[20], (N_HYENA_LAYERS, HY_FH, 2 * HY_ORDER * D), 0.05 * HY_FH ** -0.5),
        "hy_freq": 1.0 + nrm(ks[21], (N_HYENA_LAYERS, HY_FH), 0.05),
        "hy_bias": nrm(ks[22], (N_HYENA_LAYERS, HY_ORDER, D), 0.1),
        "hy_w_out": nrm(ks[23], (N_HYENA_LAYERS, D, D), D ** -0.5),
        "ffn_w_up": nrm(ks[24], (DEPTH, D, 2 * D_FF), D ** -0.5),
        "ffn_conv": centre + nrm(ks[25], (DEPTH, 3, D_FF), 0.3),
        "ffn_w_down": nrm(ks[26], (DEPTH, D_FF, D), D_FF ** -0.5),
    }


def reference(x_prompt, x_sample, cache_k, cache_v, c, c_ctx, w_mod, b_mod, norm_mix, norm_ffn, norm_final,
              w_qkv, w_o, attn_sink, hy_w_in, hy_conv, hy_w_f1, hy_b_f1, hy_w_f2, hy_b_f2, hy_w_f3, hy_freq,
              hy_bias, hy_w_out, ffn_w_up, ffn_conv, ffn_w_down):
    xp, xs = x_prompt, x_sample
    ang_row, ang_col = _axial_rope_angles(xs.shape[1])
    new_k, new_v = [], []
    for i in range(DEPTH):
        mod_p = jax.nn.silu(c_ctx) @ w_mod[i] + b_mod[i]
        mod_s = (jax.nn.silu(c) @ w_mod[i] + b_mod[i])[:, None, :]
        sh1_p, sc1_p, g1_p, sh2_p, sc2_p, g2_p = jnp.split(mod_p, 6, axis=-1)
        sh1_s, sc1_s, g1_s, sh2_s, sc2_s, g2_s = jnp.split(mod_s, 6, axis=-1)
        hp = _modulate(xp, norm_mix[i], sh1_p, sc1_p)
        hs = _modulate(xs, norm_mix[i], sh1_s, sc1_s)
        if i % N_MIXERS == 0:
            a = i // N_MIXERS
            qp, kp, vp = _qkv(hp, w_qkv[a])
            new_k.append(kp)
            new_v.append(vp)
            mp = _context_attention(qp, kp, vp, attn_sink[a]) @ w_o[a]
            qs, ks_, vs = _qkv(hs, w_qkv[a])
            qs = _axial_rope(qs, ang_row, ang_col)
            ks_ = _axial_rope(ks_, ang_row, ang_col)
            ms = _latent_attention(qs, ks_, vs, cache_k[:, a], cache_v[:, a], attn_sink[a]) @ w_o[a]
        else:
            h = i // N_MIXERS
            mp = _hyena(hp, hy_w_in[h], hy_conv[h], hy_w_f1[h], hy_b_f1[h], hy_w_f2[h], hy_b_f2[h],
                        hy_w_f3[h], hy_freq[h], hy_bias[h], hy_w_out[h])
            ms = _hyena(hs, hy_w_in[h], hy_conv[h], hy_w_f1[h], hy_b_f1[h], hy_w_f2[h], hy_b_f2[h],
                        hy_w_f3[h], hy_freq[h], hy_bias[h], hy_w_out[h])
        xp = xp + g1_p * mp
        xs = xs + g1_s * ms
        xp = xp + g2_p * _conv_ffn(_modulate(xp, norm_ffn[i], sh2_p, sc2_p), ffn_w_up[i], ffn_conv[i], ffn_w_down[i])
        xs = xs + g2_s * _conv_ffn(_modulate(xs, norm_ffn[i], sh2_s, sc2_s), ffn_w_up[i], ffn_conv[i], ffn_w_down[i])
    y_prompt = _rmsnorm(xp, norm_final)
    y_sample = _rmsnorm(xs, norm_final)
    new_cache_k = jnp.stack(new_k, axis=1)
    new_cache_v = jnp.stack(new_v, axis=1)
    return (y_prompt, y_sample, new_cache_k, new_cache_v)
```

```python
import functools
import math

import jax
import jax.numpy as jnp
import numpy as np
from jax import lax
from jax.experimental import pallas as pl
from jax.experimental.pallas import tpu as pltpu

F32 = jnp.float32
BF16 = jnp.bfloat16

D_MODEL = 2048
HEAD_DIM = 128
N_HEADS = 16
N_KV_HEADS = 4
GROUP = N_HEADS // N_KV_HEADS
Q_DIM = N_HEADS * HEAD_DIM
KV_DIM = N_KV_HEADS * HEAD_DIM
QKV_DIM = Q_DIM + 2 * KV_DIM
GRID_W = 64
WINDOW = 128
ROPE_THETA = 10000.0
ROPE_PAIRS = HEAD_DIM // 4
HY_BANDS = 16
HY_EMB = 1 + 2 * HY_BANDS
HY_EMB_PAD = 40
HY_FH = 64
HY_MIN_DECAY = math.log(1e-2) / 1.5
HY_MAX_DECAY = math.log(1e-2) / 0.3
D_FF = 5632
EPS = 1e-6
NEG_INF = -1e30

VMEM_LIMIT = 56 * 1024 * 1024
HALO = 16
FFT_N2 = 512
FFT_ROWS = 32


def _cparams(sem):
    return pltpu.CompilerParams(dimension_semantics=sem, vmem_limit_bytes=VMEM_LIMIT)


def _modulated_norm(x, g, shift, scale):
    ms = jnp.mean(x * x, axis=-1, keepdims=True)
    y = x * lax.rsqrt(ms + EPS) * g
    return y * (1.0 + scale) + shift


def _mod_kernel(c_ref, w_ref, b_ref, o_ref):
    c = c_ref[...]
    s = c * (1.0 / (1.0 + jnp.exp(-c)))
    o_ref[0] = jnp.dot(s, w_ref[0], preferred_element_type=F32,
                       precision=lax.Precision.HIGHEST) + b_ref[0]


def _modulation(cvec, w_mod, b_mod):
    depth, d, n = w_mod.shape
    rows = cvec.shape[0]
    bn = 1024
    return pl.pallas_call(
        _mod_kernel,
        grid=(depth, n // bn),
        in_specs=[
            pl.BlockSpec((rows, d), lambda l, j: (0, 0)),
            pl.BlockSpec((1, d, bn), lambda l, j: (l, 0, j)),
            pl.BlockSpec((1, 1, bn), lambda l, j: (l, 0, j)),
        ],
        out_specs=pl.BlockSpec((1, rows, bn), lambda l, j: (l, 0, j)),
        out_shape=jax.ShapeDtypeStruct((depth, rows, n), F32),
        compiler_params=_cparams(("parallel", "parallel")),
        name="modulation",
    )(cvec, w_mod, b_mod.reshape(depth, 1, n))


def _rope(x, cos, sa, sb):
    return x * cos + pltpu.roll(x, HEAD_DIM - 32, axis=1) * sa + pltpu.roll(x, 32, axis=1) * sb


def _qkv_kernel(*refs, rope, emit_kv):
    x_ref, g_ref, sh_ref, sc_ref, w_ref = refs[:5]
    pos = 5
    if rope:
        cos_ref, sa_ref, sb_ref = refs[pos:pos + 3]
        pos += 3
    o_ref = refs[pos]
    pos += 1
    if emit_kv:
        kv_ref = refs[pos]
        pos += 1
    h_scr = refs[pos]

    j = pl.program_id(2)
    n_q = Q_DIM // 512

    @pl.when(j == 0)
    def _():
        h_scr[...] = _modulated_norm(x_ref[0], g_ref[...], sh_ref[0], sc_ref[0]).astype(BF16)

    acc = jnp.dot(h_scr[...], w_ref[...], preferred_element_type=F32)

    if emit_kv:
        @pl.when(j >= n_q)
        def _():
            kv_ref[...] = acc

    def store(scale, rotate):
        for h in range(512 // HEAD_DIM):
            sl = slice(h * HEAD_DIM, (h + 1) * HEAD_DIM)
            xh = acc[:, sl]
            if rotate:
                xh = _rope(xh, cos_ref[...], sa_ref[...], sb_ref[...])
            if scale != 1.0:
                xh = xh * scale
            o_ref[:, sl] = xh.astype(BF16)

    @pl.when(j < n_q)
    def _():
        store(HEAD_DIM ** -0.5, rope)

    @pl.when(j == n_q)
    def _():
        store(1.0, rope)

    @pl.when(j > n_q)
    def _():
        store(1.0, False)


def _qkv_proj(x, g, shift, scale, w, rope_tabs, emit_kv, bm):
    G, S, D = x.shape
    nb = S // bm
    bn = 512
    nj = QKV_DIM // bn
    rope = rope_tabs is not None
    in_specs = [
        pl.BlockSpec((1, bm, D), lambda gi, i, j: (gi, i, 0)),
        pl.BlockSpec((1, D), lambda gi, i, j: (0, 0)),
        pl.BlockSpec((1, 1, D), lambda gi, i, j: (gi, 0, 0)),
        pl.BlockSpec((1, 1, D), lambda gi, i, j: (gi, 0, 0)),
        pl.BlockSpec((D, bn), lambda gi, i, j: (0, j)),
    ]
    args = [x, g, shift, scale, w]
    if rope:
        for t in rope_tabs:
            in_specs.append(pl.BlockSpec((bm, HEAD_DIM), lambda gi, i, j: (i, 0)))
            args.append(t)
    out_specs = [pl.BlockSpec((bm, bn), lambda gi, i, j: (gi * nb + i, j))]
    out_shape = [jax.ShapeDtypeStruct((G * S, QKV_DIM), BF16)]
    if emit_kv:
        n_q = Q_DIM // bn
        out_specs.append(pl.BlockSpec((bm, bn), lambda gi, i, j: (gi * nb + i, jnp.maximum(j - n_q, 0))))
        out_shape.append(jax.ShapeDtypeStruct((G * S, 2 * KV_DIM), F32))
    res = pl.pallas_call(
        functools.partial(_qkv_kernel, rope=rope, emit_kv=emit_kv),
        grid=(G, nb, nj),
        in_specs=in_specs,
        out_specs=out_specs,
        out_shape=out_shape,
        scratch_shapes=[pltpu.VMEM((bm, D), BF16)],
        compiler_params=_cparams(("parallel", "parallel", "arbitrary")),
        name="qkv_proj",
    )(*args)
    return res


def _attn_kernel(*refs, window, bq):
    sink_ref, q_ref = refs[:2]
    pos = 2
    if window:
        kp_ref, kc_ref, kn_ref, vp_ref, vc_ref, vn_ref, bias_ref = refs[pos:pos + 7]
        pos += 7
    ck_ref, cv_ref, o_ref = refs[pos:pos + 3]

    for h in range(N_KV_HEADS):
        hs = slice(h * HEAD_DIM, (h + 1) * HEAD_DIM)
        qs = jnp.concatenate(
            [q_ref[:, (h * GROUP + g) * HEAD_DIM:(h * GROUP + g + 1) * HEAD_DIM] for g in range(GROUP)],
            axis=0)
        sk = jnp.concatenate(
            [jnp.full((bq, 1), sink_ref[h * GROUP + g], F32) for g in range(GROUP)], axis=0)
        ck = ck_ref[0, :, hs]
        cv = cv_ref[0, :, hs]
        sc = lax.dot_general(qs, ck, (((1,), (1,)), ((), ())), preferred_element_type=F32)
        m = jnp.maximum(jnp.max(sc, axis=-1, keepdims=True), sk)
        if window:
            kw = jnp.concatenate([kp_ref[:, hs], kc_ref[:, hs], kn_ref[:, hs]], axis=0)
            vw = jnp.concatenate([vp_ref[:, hs], vc_ref[:, hs], vn_ref[:, hs]], axis=0)
            sw = lax.dot_general(qs, kw, (((1,), (1,)), ((), ())), preferred_element_type=F32)
            bias = bias_ref[0]
            sw = sw + jnp.concatenate([bias] * GROUP, axis=0)
            m = jnp.maximum(m, jnp.max(sw, axis=-1, keepdims=True))
            pw = jnp.exp(sw - m)
        pc = jnp.exp(sc - m)
        denom = jnp.sum(pc, axis=-1, keepdims=True) + jnp.exp(sk - m)
        o = jnp.dot(pc.astype(BF16), cv, preferred_element_type=F32)
        if window:
            denom = denom + jnp.sum(pw, axis=-1, keepdims=True)
            o = o + jnp.dot(pw.astype(BF16), vw, preferred_element_type=F32)
        o = o * (1.0 / denom)
        for g in range(GROUP):
            c0 = (h * GROUP + g) * HEAD_DIM
            o_ref[:, c0:c0 + HEAD_DIM] = o[g * bq:(g + 1) * bq].astype(BF16)


def _window_bias():
    i = np.arange(WINDOW)[:, None]
    j = np.arange(WINDOW)[None, :]
    prev = np.where(j >= i, 0.0, NEG_INF)
    cur = np.zeros((WINDOW, WINDOW))
    nxt = np.where(j <= i, 0.0, NEG_INF)
    dead = np.full((WINDOW, WINDOW), NEG_INF)
    first = np.concatenate([dead, cur, nxt], axis=1)
    mid = np.concatenate([prev, cur, nxt], axis=1)
    last = np.concatenate([prev, cur, dead], axis=1)
    return np.stack([first, mid, last]).astype(np.float32)


def _attention(qkv, sink, ctx_k, ctx_v, n_batch, seq, window):
    kcol = Q_DIM // KV_DIM
    vcol = kcol + 1
    n_ctx = ctx_k.shape[1]
    if window:
        bq = WINDOW
        nq = seq // bq
        in_specs = [
            pl.BlockSpec(memory_space=pltpu.SMEM),
            pl.BlockSpec((bq, Q_DIM), lambda b, n: (b * nq + n, 0)),
            pl.BlockSpec((bq, KV_DIM), lambda b, n: (b * nq + jnp.maximum(n - 1, 0), kcol)),
            pl.BlockSpec((bq, KV_DIM), lambda b, n: (b * nq + n, kcol)),
            pl.BlockSpec((bq, KV_DIM), lambda b, n: (b * nq + jnp.minimum(n + 1, nq - 1), kcol)),
            pl.BlockSpec((bq, KV_DIM), lambda b, n: (b * nq + jnp.maximum(n - 1, 0), vcol)),
            pl.BlockSpec((bq, KV_DIM), lambda b, n: (b * nq + n, vcol)),
            pl.BlockSpec((bq, KV_DIM), lambda b, n: (b * nq + jnp.minimum(n + 1, nq - 1), vcol)),
            pl.BlockSpec((1, bq, 3 * bq),
                         lambda b, n: (jnp.where(n == 0, 0, jnp.where(n == nq - 1, 2, 1)), 0, 0)),
            pl.BlockSpec((1, n_ctx, KV_DIM), lambda b, n: (b, 0, 0)),
            pl.BlockSpec((1, n_ctx, KV_DIM), lambda b, n: (b, 0, 0)),
        ]
        args = [sink, qkv, qkv, qkv, qkv, qkv, qkv, qkv, jnp.asarray(_window_bias()), ctx_k, ctx_v]
    else:
        bq = seq
        nq = 1
        in_specs = [
            pl.BlockSpec(memory_space=pltpu.SMEM),
            pl.BlockSpec((bq, Q_DIM), lambda b, n: (b, 0)),
            pl.BlockSpec((1, n_ctx, KV_DIM), lambda b, n: (b, 0, 0)),
            pl.BlockSpec((1, n_ctx, KV_DIM), lambda b, n: (b, 0, 0)),
        ]
        args = [sink, qkv, ctx_k, ctx_v]
    return pl.pallas_call(
        functools.partial(_attn_kernel, window=window, bq=bq),
        grid=(n_batch, nq),
        in_specs=in_specs,
        out_specs=pl.BlockSpec((bq, Q_DIM), lambda b, n: (b * nq + n, 0)),
        out_shape=jax.ShapeDtypeStruct((n_batch * seq, Q_DIM), BF16),
        compiler_params=_cparams(("parallel", "parallel")),
        name="attention",
    )(*args)


def _proj_res_kernel(a_ref, w_ref, x_ref, gate_ref, o_ref):
    acc = jnp.dot(a_ref[...], w_ref[...], preferred_element_type=F32)
    o_ref[0] = x_ref[0] + gate_ref[0] * acc


def _proj_residual(a, w, x, gate, bm):
    G, S, D = x.shape
    K = a.shape[1]
    nb = S // bm
    bn = 1024
    return pl.pallas_call(
        _proj_res_kernel,
        grid=(G, nb, D // bn),
        in_specs=[
            pl.BlockSpec((bm, K), lambda gi, i, j: (gi * nb + i, 0)),
            pl.BlockSpec((K, bn), lambda gi, i, j: (0, j)),
            pl.BlockSpec((1, bm, bn), lambda gi, i, j: (gi, i, j)),
            pl.BlockSpec((1, 1, bn), lambda gi, i, j: (gi, 0, j)),
        ],
        out_specs=pl.BlockSpec((1, bm, bn), lambda gi, i, j: (gi, i, j)),
        out_shape=jax.ShapeDtypeStruct((G, S, D), F32),
        compiler_params=_cparams(("parallel", "parallel", "parallel")),
        name="proj_residual",
    )(a, w, x, gate)


def _fill_h_ext(h_scr, xm_ref, xp_ref, xn_ref, g_ref, sh_ref, sc_ref, bm):
    g, sh, sc = g_ref[...], sh_ref[0], sc_ref[0]
    h_scr[0:HALO] = _modulated_norm(xp_ref[0], g, sh, sc).astype(BF16)
    h_scr[HALO:HALO + bm] = _modulated_norm(xm_ref[0], g, sh, sc).astype(BF16)
    h_scr[HALO + bm:] = _modulated_norm(xn_ref[0], g, sh, sc).astype(BF16)


def _dwconv3_rows(u_ext, cw, row0, bm, seq_len):
    assert seq_len & (seq_len - 1) == 0
    pos = (row0 + lax.broadcasted_iota(jnp.int32, (bm, 1), 0)) & (seq_len - 1)
    prev = jnp.where(pos == 0, 0.0, u_ext[HALO - 1:HALO - 1 + bm])
    nxt = jnp.where(pos == seq_len - 1, 0.0, u_ext[HALO + 1:HALO + 1 + bm])
    return prev * cw[0:1] + u_ext[HALO:HALO + bm] * cw[1:2] + nxt * cw[2:3]


def _halo_specs(bm, D, S):
    nh = bm // HALO
    last = S // HALO - 1
    return [
        pl.BlockSpec((1, bm, D), lambda gi, i, j: (gi, i, 0)),
        pl.BlockSpec((1, HALO, D), lambda gi, i, j: (gi, jnp.maximum(i * nh - 1, 0), 0)),
        pl.BlockSpec((1, HALO, D), lambda gi, i, j: (gi, jnp.minimum((i + 1) * nh, last), 0)),
    ]


def _ffn_kernel(*refs, bm, seq_len, final_norm):
    (xm_ref, xp_ref, xn_ref, g_ref, sh_ref, sc_ref, gate_ref,
     wa_ref, wb_ref, cw_ref, wd_ref) = refs[:11]
    pos = 11
    if final_norm:
        gf_ref = refs[pos]
        pos += 1
    o_ref, h_scr, acc_scr = refs[pos:pos + 3]

    i = pl.program_id(1)
    j = pl.program_id(2)

    @pl.when(j == 0)
    def _():
        _fill_h_ext(h_scr, xm_ref, xp_ref, xn_ref, g_ref, sh_ref, sc_ref, bm)

    ua = jnp.dot(h_scr[...], wa_ref[...], preferred_element_type=F32)
    ub = jnp.dot(h_scr[HALO:HALO + bm], wb_ref[...], preferred_element_type=F32)
    a = _dwconv3_rows(ua, cw_ref[...], i * bm, bm, seq_len)
    act = (a * (1.0 / (1.0 + jnp.exp(-a))) * ub).astype(BF16)
    part = jnp.dot(act, wd_ref[...], preferred_element_type=F32)

    @pl.when(j == 0)
    def _():
        acc_scr[...] = part

    @pl.when(j > 0)
    def _():
        acc_scr[...] += part

    @pl.when(j == pl.num_programs(2) - 1)
    def _():
        y = xm_ref[0] + gate_ref[0] * acc_scr[...]
        if final_norm:
            ms = jnp.mean(y * y, axis=-1, keepdims=True)
            y = y * lax.rsqrt(ms + EPS) * gf_ref[...]
        o_ref[0] = y


def _conv_ffn(x, g, shift, scale, gate, w_up, conv_w, w_down, seq_len, bm, bf, g_final=None):
    G, S, D = x.shape
    nb = S // bm
    nc = D_FF // bf
    final_norm = g_final is not None
    in_specs = _halo_specs(bm, D, S) + [
        pl.BlockSpec((1, D), lambda gi, i, j: (0, 0)),
        pl.BlockSpec((1, 1, D), lambda gi, i, j: (gi, 0, 0)),
        pl.BlockSpec((1, 1, D), lambda gi, i, j: (gi, 0, 0)),
        pl.BlockSpec((1, 1, D), lambda gi, i, j: (gi, 0, 0)),
        pl.BlockSpec((D, bf), lambda gi, i, j: (0, j)),
        pl.BlockSpec((D, bf), lambda gi, i, j: (0, nc + j)),
        pl.BlockSpec((3, bf), lambda gi, i, j: (0, j)),
        pl.BlockSpec((bf, D), lambda gi, i, j: (j, 0)),
    ]
    args = [x, x, x, g, shift, scale, gate, w_up, w_up, conv_w, w_down]
    if final_norm:
        in_specs.append(pl.BlockSpec((1, D), lambda gi, i, j: (0, 0)))
        args.append(g_final)
    return pl.pallas_call(
        functools.partial(_ffn_kernel, bm=bm, seq_len=seq_len, final_norm=final_norm),
        grid=(G, nb, nc),
        in_specs=in_specs,
        out_specs=pl.BlockSpec((1, bm, D), lambda gi, i, j: (gi, i, 0)),
        out_shape=jax.ShapeDtypeStruct((G, S, D), F32),
        scratch_shapes=[pltpu.VMEM((bm + 2 * HALO, D), BF16), pltpu.VMEM((bm, D), F32)],
        compiler_params=_cparams(("parallel", "parallel", "arbitrary")),
        name="conv_ffn",
    )(*args)


def _hy_in_kernel(xm_ref, xp_ref, xn_ref, g_ref, sh_ref, sc_ref, w_ref, cw_ref, o_ref, h_scr,
                  *, bm, seq_len):
    i = pl.program_id(1)
    j = pl.program_id(2)

    @pl.when(j == 0)
    def _():
        _fill_h_ext(h_scr, xm_ref, xp_ref, xn_ref, g_ref, sh_ref, sc_ref, bm)

    u = jnp.dot(h_scr[...], w_ref[...], preferred_element_type=F32)
    o_ref[0] = _dwconv3_rows(u, cw_ref[...], i * bm, bm, seq_len)


def _hyena_in(x, g, shift, scale, w_in, conv_w, seq_len, bm, bn):
    G, S, D = x.shape
    nb = S // bm
    N = w_in.shape[1]
    in_specs = _halo_specs(bm, D, S) + [
        pl.BlockSpec((1, D), lambda gi, i, j: (0, 0)),
        pl.BlockSpec((1, 1, D), lambda gi, i, j: (gi, 0, 0)),
        pl.BlockSpec((1, 1, D), lambda gi, i, j: (gi, 0, 0)),
        pl.BlockSpec((D, bn), lambda gi, i, j: (0, j)),
        pl.BlockSpec((3, bn), lambda gi, i, j: (0, j)),
    ]
    return pl.pallas_call(
        functools.partial(_hy_in_kernel, bm=bm, seq_len=seq_len),
        grid=(G, nb, N // bn),
        in_specs=in_specs,
        out_specs=pl.BlockSpec((1, bm, bn), lambda gi, i, j: (gi, i, j)),
        out_shape=jax.ShapeDtypeStruct((G, S, N), F32),
        scratch_shapes=[pltpu.VMEM((bm + 2 * HALO, D), BF16)],
        compiler_params=_cparams(("parallel", "parallel", "arbitrary")),
        name="hyena_in",
    )(x, x, x, g, shift, scale, w_in, conv_w)


class _FftPlan:
    def __init__(self, seq_len):
        self.L = seq_len
        self.n2 = FFT_N2
        self.n1 = max(1, 2 * seq_len // FFT_N2)
        self.n = self.n1 * self.n2
        self.classes = self.n1 // 2 + 1
        self.slab = min(self.n2, seq_len)
        self.data_slabs = seq_len // self.slab

    def stage1_coef(self, m, k1):
        th = 2.0 * math.pi * ((m * k1) % self.n1) / self.n1
        return _snap(math.cos(th)), _snap(-math.sin(th))

    def class_weight(self, k1):
        if self.n1 == 1:
            return 1.0
        return 1.0 if k1 in (0, self.n1 // 2) else 2.0


def _snap(v):
    for t in (0.0, 1.0, -1.0):
        if abs(v - t) < 1e-12:
            return t
    return v


@functools.lru_cache(maxsize=None)
def _dft_consts(n1, n2, lanes):
    n = n1 * n2
    k = np.arange(n2)
    ang = 2.0 * np.pi * np.outer(k, k) / n2
    fr, fi = np.cos(ang), -np.sin(ang)
    fwd = np.block([[fr, -fi], [fi, fr]]).astype(np.float32)
    inv = np.block([[fr, fi], [-fi, fr]]).astype(np.float32)
    classes = n1 // 2 + 1
    tw = 2.0 * np.pi * np.outer(np.arange(classes), np.arange(n2)) / n
    twc = np.repeat(np.cos(tw)[:, :, None], lanes, axis=2).astype(np.float32)
    tws = np.repeat((-np.sin(tw))[:, :, None], lanes, axis=2).astype(np.float32)
    return fwd, inv, twc, tws


def _axpy(acc, c, x):
    if c == 0.0:
        return acc
    t = x if c == 1.0 else (-x if c == -1.0 else c * x)
    return t if acc is None else acc + t


def _fft_forward_class(plan, k1, n_slabs, load_rows, twc_ref, tws_ref, abuf):
    n2, R = plan.n2, FFT_ROWS

    def body(r, carry):
        off = pl.multiple_of(r * R, R)
        ar = ai = None
        for m in range(n_slabs):
            c, s = plan.stage1_coef(m, k1)
            if c == 0.0 and s == 0.0:
                continue
            x = load_rows(m * n2 + off, R)
            ar = _axpy(ar, c, x)
            ai = _axpy(ai, s, x)
        if k1 == 0:
            pr, pi = ar, ai
        else:
            tc = twc_ref[k1, pl.ds(off, R), :]
            ts = tws_ref[k1, pl.ds(off, R), :]
            if ai is None:
                pr, pi = ar * tc, ar * ts
            else:
                pr, pi = ar * tc - ai * ts, ar * ts + ai * tc
        abuf[pl.ds(off, R), :] = pr.astype(BF16)
        if pi is None:
            abuf[pl.ds(n2 + off, R), :] = jnp.zeros((R, abuf.shape[1]), BF16)
        else:
            abuf[pl.ds(n2 + off, R), :] = pi.astype(BF16)
        return carry

    lax.fori_loop(0, n2 // R, body, 0)


def _fftconv_kernel(z_ref, gate_ref, bias_ref, h_ref, fwd_ref, inv_ref, twc_ref, tws_ref,
                    o_ref, abuf, ybuf, b_scr, acc_scr, *, plan):
    n2, L, R = plan.n2, plan.L, FFT_ROWS
    lanes = abuf.shape[1]

    if plan.n1 == 1:
        abuf[0:L] = z_ref[0].astype(BF16)
        abuf[L:] = jnp.zeros((2 * n2 - L, lanes), BF16)

    for k1 in range(plan.classes):
        if plan.n1 > 1:
            _fft_forward_class(plan, k1, plan.data_slabs,
                               lambda start, rows: z_ref[0, pl.ds(start, rows), :],
                               twc_ref, tws_ref, abuf)
        xf = jnp.dot(fwd_ref[...], abuf[...], preferred_element_type=F32)
        xr, xi = xf[:n2], xf[n2:]
        hr, hi = h_ref[k1, 0:n2], h_ref[k1, n2:]
        ybuf[0:n2] = (xr * hr - xi * hi).astype(BF16)
        ybuf[n2:] = (xr * hi + xi * hr).astype(BF16)
        b_scr[...] = jnp.dot(inv_ref[...], ybuf[...], preferred_element_type=F32)

        if plan.n1 == 1:
            continue
        wk = plan.class_weight(k1)

        def body(r, carry, k1=k1, wk=wk):
            off = pl.multiple_of(r * R, R)
            br = b_scr[pl.ds(off, R), :]
            bi = b_scr[pl.ds(n2 + off, R), :]
            if k1 == 0:
                pr, pi = br, bi
            else:
                tc = twc_ref[k1, pl.ds(off, R), :]
                ts = tws_ref[k1, pl.ds(off, R), :]
                pr, pi = br * tc + bi * ts, bi * tc - br * ts
            for m in range(plan.data_slabs):
                c, s = plan.stage1_coef(m, k1)
                t = _axpy(_axpy(None, wk * c, pr), wk * s, pi)
                rows = pl.ds(m * n2 + off, R)
                if k1 == 0:
                    acc_scr[rows, :] = t
                else:
                    acc_scr[rows, :] += t
            return carry

        lax.fori_loop(0, n2 // R, body, 0)

    inv_n = 1.0 / plan.n
    if plan.n1 == 1:
        z = z_ref[0]
        y = b_scr[0:L] * inv_n + bias_ref[...] * z
        o_ref[0] = (gate_ref[0] * y).astype(o_ref.dtype)
    else:
        def fin(r, carry):
            rows = pl.ds(pl.multiple_of(r * R, R), R)
            z = z_ref[0, rows, :]
            y = acc_scr[rows, :] * inv_n + bias_ref[...] * z
            o_ref[0, rows, :] = (gate_ref[0, rows, :] * y).astype(o_ref.dtype)
            return carry

        lax.fori_loop(0, L // R, fin, 0)


def _fftconv(plan, z, z_col, gate, gate_col, bias, hspec, n_seq, lanes, out_dtype):
    L, n2 = plan.L, plan.n2
    nc = D_MODEL // lanes
    fwd, inv, twc, tws = _dft_consts(plan.n1, n2, lanes)
    zc, gc = z_col // lanes, gate_col // lanes
    acc_rows = L if plan.n1 > 1 else 8
    return pl.pallas_call(
        functools.partial(_fftconv_kernel, plan=plan),
        grid=(nc, n_seq),
        in_specs=[
            pl.BlockSpec((1, L, lanes), lambda c, b: (b, 0, zc + c)),
            pl.BlockSpec((1, L, lanes), lambda c, b: (b, 0, gc + c)),
            pl.BlockSpec((1, lanes), lambda c, b: (0, c)),
            pl.BlockSpec((plan.classes, 2 * n2, lanes), lambda c, b: (0, 0, c)),
            pl.BlockSpec((2 * n2, 2 * n2), lambda c, b: (0, 0)),
            pl.BlockSpec((2 * n2, 2 * n2), lambda c, b: (0, 0)),
            pl.BlockSpec((plan.classes, n2, lanes), lambda c, b: (0, 0, 0)),
            pl.BlockSpec((plan.classes, n2, lanes), lambda c, b: (0, 0, 0)),
        ],
        out_specs=pl.BlockSpec((1, L, lanes), lambda c, b: (b, 0, c)),
        out_shape=jax.ShapeDtypeStruct((n_seq, L, D_MODEL), out_dtype),
        scratch_shapes=[
            pltpu.VMEM((2 * n2, lanes), BF16),
            pltpu.VMEM((2 * n2, lanes), BF16),
            pltpu.VMEM((2 * n2, lanes), F32),
            pltpu.VMEM((acc_rows, lanes), F32),
        ],
        compiler_params=_cparams(("parallel", "arbitrary")),
        name="hyena_fftconv",
    )(z, gate, bias, hspec, jnp.asarray(fwd).astype(BF16), jnp.asarray(inv).astype(BF16),
      jnp.asarray(twc), jnp.asarray(tws))


@functools.lru_cache(maxsize=None)
def _filter_tables(L, lanes):
    r = np.arange(2 * L)
    p = np.where(r < L, r, 2 * L - r)
    p[L] = 0
    t = p / (L - 1.0)
    w = 2.0 * np.pi * p / L
    f = np.linspace(1e-4, HY_BANDS - 1, HY_BANDS)
    feat = np.concatenate([t[:, None], np.cos(np.outer(w, f)), -np.sin(np.outer(w, f))], axis=1)
    feat = np.pad(feat, ((0, 0), (0, HY_EMB_PAD - HY_EMB))).astype(np.float32)
    t_tab = np.repeat(t[:, None], lanes, axis=1).astype(np.float32)
    return feat, t_tab


def _filter_hidden_kernel(feat_ref, w1_ref, b1_ref, w2_ref, b2_ref, fq_ref, o_ref):
    hp = lax.Precision.HIGHEST
    h = jnp.sin(fq_ref[...] * (jnp.dot(feat_ref[...], w1_ref[...], preferred_element_type=F32,
                                        precision=hp) + b1_ref[...]))
    h = jnp.sin(fq_ref[...] * (jnp.dot(h, w2_ref[...], preferred_element_type=F32,
                                        precision=hp) + b2_ref[...]))
    o_ref[...] = h


def _filter_hidden(feat, w1, b1, w2, b2, freq):
    rows = feat.shape[0]
    br = 1024 if rows % 1024 == 0 else rows
    vec = pl.BlockSpec((1, HY_FH), lambda i: (0, 0))
    return pl.pallas_call(
        _filter_hidden_kernel,
        grid=(rows // br,),
        in_specs=[
            pl.BlockSpec((br, HY_EMB_PAD), lambda i: (i, 0)),
            pl.BlockSpec((HY_EMB_PAD, HY_FH), lambda i: (0, 0)),
            vec,
            pl.BlockSpec((HY_FH, HY_FH), lambda i: (0, 0)),
            vec, vec,
        ],
        out_specs=pl.BlockSpec((br, HY_FH), lambda i: (i, 0)),
        out_shape=jax.ShapeDtypeStruct((rows, HY_FH), F32),
        compiler_params=_cparams(("parallel",)),
        name="hyena_filter_hidden",
    )(feat, w1, b1, w2, b2, freq)


def _filter_spec_kernel(hid_ref, w3a_ref, w3b_ref, t_ref, delta_ref, fwd_ref, twc_ref, tws_ref,
                        o_ref, f_scr, abuf, *, plan):
    L, n2 = plan.L, plan.n2
    hp = lax.Precision.HIGHEST
    absd = jnp.abs(delta_ref[...])
    top = jnp.dot(hid_ref[0:L], w3a_ref[0], preferred_element_type=F32, precision=hp)
    f_scr[0:L] = top * jnp.exp(-t_ref[0:L] * absd)
    bot = jnp.dot(hid_ref[L:], w3b_ref[0], preferred_element_type=F32, precision=hp)
    keep = lax.broadcasted_iota(jnp.int32, (L, 1), 0) > 0
    f_scr[L:] = jnp.where(keep, bot * jnp.exp(-t_ref[L:] * absd), 0.0)

    if plan.n1 == 1:
        abuf[0:n2] = f_scr[...].astype(BF16)
        abuf[n2:] = jnp.zeros((n2, abuf.shape[1]), BF16)
    for k1 in range(plan.classes):
        if plan.n1 > 1:
            _fft_forward_class(plan, k1, plan.n1,
                               lambda start, rows: f_scr[pl.ds(start, rows), :],
                               twc_ref, tws_ref, abuf)
        o_ref[0, k1] = jnp.dot(fwd_ref[...], abuf[...], preferred_element_type=F32)


def _filter_spectrum(plan, hidden, w_f3, lanes):
    L, n2 = plan.L, plan.n2
    nc = D_MODEL // lanes
    fwd, _, twc, tws = _dft_consts(plan.n1, n2, lanes)
    _, t_tab = _filter_tables(L, lanes)
    delta = np.linspace(HY_MIN_DECAY, HY_MAX_DECAY, D_MODEL).astype(np.float32)[None, :]
    w3 = w_f3.reshape(HY_FH, 2, 2, D_MODEL).transpose(1, 2, 0, 3)
    return pl.pallas_call(
        functools.partial(_filter_spec_kernel, plan=plan),
        grid=(2, nc),
        in_specs=[
            pl.BlockSpec((2 * L, HY_FH), lambda o, c: (0, 0)),
            pl.BlockSpec((None, 1, HY_FH, lanes), lambda o, c: (0, o, 0, c)),
            pl.BlockSpec((None, 1, HY_FH, lanes), lambda o, c: (1, o, 0, c)),
            pl.BlockSpec((2 * L, lanes), lambda o, c: (0, 0)),
            pl.BlockSpec((1, lanes), lambda o, c: (0, c)),
            pl.BlockSpec((2 * n2, 2 * n2), lambda o, c: (0, 0)),
            pl.BlockSpec((plan.classes, n2, lanes), lambda o, c: (0, 0, 0)),
            pl.BlockSpec((plan.classes, n2, lanes), lambda o, c: (0, 0, 0)),
        ],
        out_specs=pl.BlockSpec((1, plan.classes, 2 * n2, lanes), lambda o, c: (o, 0, 0, c)),
        out_shape=jax.ShapeDtypeStruct((2, plan.classes, 2 * n2, D_MODEL), F32),
        scratch_shapes=[pltpu.VMEM((2 * L, lanes), F32), pltpu.VMEM((2 * n2, lanes), BF16)],
        compiler_params=_cparams(("parallel", "parallel")),
        name="hyena_filter_spectrum",
    )(hidden, w3, w3, jnp.asarray(t_tab), jnp.asarray(delta), jnp.asarray(fwd).astype(BF16),
      jnp.asarray(twc), jnp.asarray(tws))


@functools.lru_cache(maxsize=None)
def _rope_tables(L):
    t = np.arange(L)
    inv = ROPE_THETA ** (-np.arange(ROPE_PAIRS, dtype=np.float32) / ROPE_PAIRS)
    ang_row = (t // GRID_W).astype(np.float32)[:, None] * inv
    ang_col = (t % GRID_W).astype(np.float32)[:, None] * inv
    zero = np.zeros_like(ang_row)
    cos = np.concatenate([np.cos(ang_row)] * 2 + [np.cos(ang_col)] * 2, axis=1)
    sa = np.concatenate([-np.sin(ang_row), zero, -np.sin(ang_col), zero], axis=1)
    sb = np.concatenate([zero, np.sin(ang_row), zero, np.sin(ang_col)], axis=1)
    return cos.astype(np.float32), sa.astype(np.float32), sb.astype(np.float32)


def _hyena_mixer(x, g, shift, scale, gate, w_in, conv_w, hspec, bias, w_out, seq_len, lanes, bm):
    G, S, D = x.shape
    plan = _FftPlan(seq_len)
    n_seq = G * S // seq_len
    u = _hyena_in(x, g, shift, scale, w_in, conv_w, seq_len, bm, 1024)
    u = u.reshape(n_seq, seq_len, 3 * D)
    z1 = _fftconv(plan, u, 2 * D, u, 0, bias[0:1], hspec[0], n_seq, lanes, F32)
    z2 = _fftconv(plan, z1, 0, u, D, bias[1:2], hspec[1], n_seq, lanes, BF16)
    return _proj_residual(z2.reshape(G * S, D), w_out, x, gate, bm)


def kernel(x_prompt, x_sample, cache_k, cache_v, c, c_ctx, w_mod, b_mod, norm_mix, norm_ffn, norm_final, w_qkv, w_o, attn_sink, hy_w_in, hy_conv, hy_w_f1, hy_b_f1, hy_w_f2, hy_b_f2, hy_w_f3, hy_freq, hy_bias, hy_w_out, ffn_w_up, ffn_conv, ffn_w_down):
    B, SEQ, D = x_prompt.shape
    DB, DSEQ, _ = x_sample.shape
    depth = w_mod.shape[0]

    n_cond = DB + 1
    cvec = jnp.concatenate([c, c_ctx[None, :], jnp.zeros((16 - n_cond, D), F32)], axis=0)
    mod = _modulation(cvec, w_mod, b_mod).reshape(depth, 16, 6, 1, D)

    xs = x_sample
    xp = x_prompt.reshape(1, B * SEQ, D)
    streams = [
        dict(x=xs, rows=slice(0, DB), seq=DSEQ, n_seq=DB, lanes=128),
        dict(x=xp, rows=slice(DB, DB + 1), seq=SEQ, n_seq=B, lanes=256),
    ]
    bm = 512
    rope_tabs = tuple(jnp.asarray(t) for t in _rope_tables(DSEQ))
    new_k, new_v = [], []

    for i in range(depth):
        g_mix = norm_mix[i][None, :]
        g_ffn = norm_ffn[i][None, :]
        if i % 2 == 0:
            a = i // 2
            wq = w_qkv[a].astype(BF16)
            wo = w_o[a].astype(BF16)
        else:
            hl = i // 2
            w_in = hy_w_in[hl].astype(BF16)
            w_out = hy_w_out[hl].astype(BF16)
            w1 = jnp.pad(hy_w_f1[hl], ((0, HY_EMB_PAD - HY_EMB), (0, 0)))
        w_up = ffn_w_up[i].astype(BF16)
        w_down = ffn_w_down[i].astype(BF16)
        g_final = norm_final[None, :] if i == depth - 1 else None

        for si, st in enumerate(streams):
            x = st["x"]
            m = mod[i, st["rows"]]
            sh1, sc1, g1, sh2, sc2, g2 = (m[:, t] for t in range(6))
            latent = si == 0
            if i % 2 == 0:
                res = _qkv_proj(x, g_mix, sh1, sc1, wq, rope_tabs if latent else None,
                                emit_kv=not latent, bm=bm)
                qkv = res[0]
                if latent:
                    ck = cache_k[:, a].reshape(DB, -1, KV_DIM).astype(BF16)
                    cv = cache_v[:, a].reshape(DB, -1, KV_DIM).astype(BF16)
                    att = _attention(qkv, attn_sink[a], ck, cv, DB, DSEQ, window=True)
                else:
                    kv = res[1]
                    new_k.append(kv[:, :KV_DIM].reshape(B, 1, SEQ, N_KV_HEADS, HEAD_DIM))
                    new_v.append(kv[:, KV_DIM:].reshape(B, 1, SEQ, N_KV_HEADS, HEAD_DIM))
                    ck = qkv[:, Q_DIM:Q_DIM + KV_DIM].reshape(B, SEQ, KV_DIM)
                    cv = qkv[:, Q_DIM + KV_DIM:].reshape(B, SEQ, KV_DIM)
                    att = _attention(qkv, attn_sink[a], ck, cv, B, SEQ, window=False)
                x = _proj_residual(att, wo, x, g1, bm)
            else:
                plan = _FftPlan(st["seq"])
                feat, _ = _filter_tables(st["seq"], st["lanes"])
                hidden = _filter_hidden(jnp.asarray(feat), w1, hy_b_f1[hl][None, :], hy_w_f2[hl],
                                        hy_b_f2[hl][None, :], hy_freq[hl][None, :])
                hspec = _filter_spectrum(plan, hidden, hy_w_f3[hl], st["lanes"])
                x = _hyena_mixer(x, g_mix, sh1, sc1, g1, w_in, hy_conv[hl], hspec, hy_bias[hl],
                                 w_out, st["seq"], st["lanes"], bm)
            x = _conv_ffn(x, g_ffn, sh2, sc2, g2, w_up, ffn_conv[i], w_down, st["seq"], bm, 512,
                          g_final=g_final)
            st["x"] = x

    y_sample = streams[0]["x"]
    y_prompt = streams[1]["x"].reshape(B, SEQ, D)
    new_cache_k = new_k[0] if len(new_k) == 1 else jnp.concatenate(new_k, axis=1)
    new_cache_v = new_v[0] if len(new_v) == 1 else jnp.concatenate(new_v, axis=1)
    return (y_prompt, y_sample, new_cache_k, new_cache_v)
```

```python
import functools
import math

import jax
import jax.numpy as jnp
import numpy as np
from jax import lax
from jax.experimental import pallas as pl
from jax.experimental.pallas import tpu as pltpu

F32 = jnp.float32
BF16 = jnp.bfloat16

D_MODEL = 2048
HEAD_DIM = 128
N_HEADS = 16
N_KV_HEADS = 4
GROUP = N_HEADS // N_KV_HEADS
Q_DIM = N_HEADS * HEAD_DIM
KV_DIM = N_KV_HEADS * HEAD_DIM
QKV_DIM = Q_DIM + 2 * KV_DIM
GRID_W = 64
WINDOW = 128
ROPE_THETA = 10000.0
ROPE_PAIRS = HEAD_DIM // 4
HY_BANDS = 16
HY_EMB = 1 + 2 * HY_BANDS
HY_EMB_PAD = 40
HY_FH = 64
HY_MIN_DECAY = math.log(1e-2) / 1.5
HY_MAX_DECAY = math.log(1e-2) / 0.3
D_FF = 5632
EPS = 1e-6
NEG_INF = -1e30

VMEM_LIMIT = 56 * 1024 * 1024
HALO = 16
FFT_N2 = 512
FFT_ROWS = 32
FFN_DOWN_COLS = 1024
TW_LANES = 128


def _cparams(sem):
    return pltpu.CompilerParams(dimension_semantics=sem, vmem_limit_bytes=VMEM_LIMIT)


def _modulated_norm(x, g, shift, scale):
    ms = jnp.mean(x * x, axis=-1, keepdims=True)
    y = x * lax.rsqrt(ms + EPS) * g
    return y * (1.0 + scale) + shift


def _mod_kernel(c_ref, w_ref, b_ref, o_ref):
    c = c_ref[...]
    s = c * (1.0 / (1.0 + jnp.exp(-c)))
    o_ref[0] = jnp.dot(s, w_ref[0], preferred_element_type=F32,
                       precision=lax.Precision.HIGHEST) + b_ref[0]


def _modulation(cvec, w_mod, b_mod):
    depth, d, n = w_mod.shape
    rows = cvec.shape[0]
    bn = 1024
    return pl.pallas_call(
        _mod_kernel,
        grid=(depth, n // bn),
        in_specs=[
            pl.BlockSpec((rows, d), lambda l, j: (0, 0)),
            pl.BlockSpec((1, d, bn), lambda l, j: (l, 0, j)),
            pl.BlockSpec((1, 1, bn), lambda l, j: (l, 0, j)),
        ],
        out_specs=pl.BlockSpec((1, rows, bn), lambda l, j: (l, 0, j)),
        out_shape=jax.ShapeDtypeStruct((depth, rows, n), F32),
        compiler_params=_cparams(("parallel", "parallel")),
        name="modulation",
    )(cvec, w_mod, b_mod.reshape(depth, 1, n))


def _rope(x, cos, sa, sb):
    return x * cos + pltpu.roll(x, HEAD_DIM - 32, axis=1) * sa + pltpu.roll(x, 32, axis=1) * sb


def _qkv_kernel(*refs, rope, emit_kv):
    x_ref, g_ref, sh_ref, sc_ref, w_ref = refs[:5]
    pos = 5
    if rope:
        cos_ref, sa_ref, sb_ref = refs[pos:pos + 3]
        pos += 3
    o_ref = refs[pos]
    pos += 1
    if emit_kv:
        kv_ref = refs[pos]
        pos += 1
    h_scr = refs[pos]

    j = pl.program_id(2)
    n_q = Q_DIM // 512

    @pl.when(j == 0)
    def _():
        h_scr[...] = _modulated_norm(x_ref[0], g_ref[...], sh_ref[0], sc_ref[0]).astype(BF16)

    acc = jnp.dot(h_scr[...], w_ref[...], preferred_element_type=F32)

    if emit_kv:
        @pl.when(j >= n_q)
        def _():
            kv_ref[...] = acc

    def store(scale, rotate):
        for h in range(512 // HEAD_DIM):
            sl = slice(h * HEAD_DIM, (h + 1) * HEAD_DIM)
            xh = acc[:, sl]
            if rotate:
                xh = _rope(xh, cos_ref[...], sa_ref[...], sb_ref[...])
            if scale != 1.0:
                xh = xh * scale
            o_ref[:, sl] = xh.astype(BF16)

    @pl.when(j < n_q)
    def _():
        store(HEAD_DIM ** -0.5, rope)

    @pl.when(j == n_q)
    def _():
        store(1.0, rope)

    @pl.when(j > n_q)
    def _():
        store(1.0, False)


def _qkv_proj(x, g, shift, scale, w, rope_tabs, emit_kv, bm):
    G, S, D = x.shape
    nb = S // bm
    bn = 512
    nj = QKV_DIM // bn
    rope = rope_tabs is not None
    in_specs = [
        pl.BlockSpec((1, bm, D), lambda gi, i, j: (gi, i, 0)),
        pl.BlockSpec((1, D), lambda gi, i, j: (0, 0)),
        pl.BlockSpec((1, 1, D), lambda gi, i, j: (gi, 0, 0)),
        pl.BlockSpec((1, 1, D), lambda gi, i, j: (gi, 0, 0)),
        pl.BlockSpec((D, bn), lambda gi, i, j: (0, j)),
    ]
    args = [x, g, shift, scale, w]
    if rope:
        for t in rope_tabs:
            in_specs.append(pl.BlockSpec((bm, HEAD_DIM), lambda gi, i, j: (i, 0)))
            args.append(t)
    out_specs = [pl.BlockSpec((bm, bn), lambda gi, i, j: (gi * nb + i, j))]
    out_shape = [jax.ShapeDtypeStruct((G * S, QKV_DIM), BF16)]
    if emit_kv:
        n_q = Q_DIM // bn
        out_specs.append(pl.BlockSpec((bm, bn), lambda gi, i, j: (gi * nb + i, jnp.maximum(j - n_q, 0))))
        out_shape.append(jax.ShapeDtypeStruct((G * S, 2 * KV_DIM), F32))
    res = pl.pallas_call(
        functools.partial(_qkv_kernel, rope=rope, emit_kv=emit_kv),
        grid=(G, nb, nj),
        in_specs=in_specs,
        out_specs=out_specs,
        out_shape=out_shape,
        scratch_shapes=[pltpu.VMEM((bm, D), BF16)],
        compiler_params=_cparams(("parallel", "parallel", "arbitrary")),
        name="qkv_proj",
    )(*args)
    return res


def _attn_kernel(*refs, window, bq):
    sink_ref, q_ref = refs[:2]
    pos = 2
    if window:
        kp_ref, kc_ref, kn_ref, vp_ref, vc_ref, vn_ref, bias_ref = refs[pos:pos + 7]
        pos += 7
    ck_ref, cv_ref, o_ref = refs[pos:pos + 3]

    for h in range(N_KV_HEADS):
        hs = slice(h * HEAD_DIM, (h + 1) * HEAD_DIM)
        qs = jnp.concatenate(
            [q_ref[:, (h * GROUP + g) * HEAD_DIM:(h * GROUP + g + 1) * HEAD_DIM] for g in range(GROUP)],
            axis=0)
        sk = jnp.concatenate(
            [jnp.full((bq, 1), sink_ref[h * GROUP + g], F32) for g in range(GROUP)], axis=0)
        ck = ck_ref[0, :, hs]
        cv = cv_ref[0, :, hs]
        sc = lax.dot_general(qs, ck, (((1,), (1,)), ((), ())), preferred_element_type=F32)
        m = jnp.maximum(jnp.max(sc, axis=-1, keepdims=True), sk)
        if window:
            kw = jnp.concatenate([kp_ref[:, hs], kc_ref[:, hs], kn_ref[:, hs]], axis=0)
            vw = jnp.concatenate([vp_ref[:, hs], vc_ref[:, hs], vn_ref[:, hs]], axis=0)
            sw = lax.dot_general(qs, kw, (((1,), (1,)), ((), ())), preferred_element_type=F32)
            bias = bias_ref[0]
            sw = sw + jnp.concatenate([bias] * GROUP, axis=0)
            m = jnp.maximum(m, jnp.max(sw, axis=-1, keepdims=True))
            pw = jnp.exp(sw - m)
        pc = jnp.exp(sc - m)
        denom = jnp.sum(pc, axis=-1, keepdims=True) + jnp.exp(sk - m)
        o = jnp.dot(pc.astype(BF16), cv, preferred_element_type=F32)
        if window:
            denom = denom + jnp.sum(pw, axis=-1, keepdims=True)
            o = o + jnp.dot(pw.astype(BF16), vw, preferred_element_type=F32)
        o = o * (1.0 / denom)
        for g in range(GROUP):
            c0 = (h * GROUP + g) * HEAD_DIM
            o_ref[:, c0:c0 + HEAD_DIM] = o[g * bq:(g + 1) * bq].astype(BF16)


def _window_bias():
    i = np.arange(WINDOW)[:, None]
    j = np.arange(WINDOW)[None, :]
    prev = np.where(j >= i, 0.0, NEG_INF)
    cur = np.zeros((WINDOW, WINDOW))
    nxt = np.where(j <= i, 0.0, NEG_INF)
    dead = np.full((WINDOW, WINDOW), NEG_INF)
    first = np.concatenate([dead, cur, nxt], axis=1)
    mid = np.concatenate([prev, cur, nxt], axis=1)
    last = np.concatenate([prev, cur, dead], axis=1)
    return np.stack([first, mid, last]).astype(np.float32)


def _attention(qkv, sink, ctx_k, ctx_v, n_batch, seq, window):
    kcol = Q_DIM // KV_DIM
    vcol = kcol + 1
    n_ctx = ctx_k.shape[1]
    if window:
        bq = WINDOW
        nq = seq // bq
        in_specs = [
            pl.BlockSpec(memory_space=pltpu.SMEM),
            pl.BlockSpec((bq, Q_DIM), lambda b, n: (b * nq + n, 0)),
            pl.BlockSpec((bq, KV_DIM), lambda b, n: (b * nq + jnp.maximum(n - 1, 0), kcol)),
            pl.BlockSpec((bq, KV_DIM), lambda b, n: (b * nq + n, kcol)),
            pl.BlockSpec((bq, KV_DIM), lambda b, n: (b * nq + jnp.minimum(n + 1, nq - 1), kcol)),
            pl.BlockSpec((bq, KV_DIM), lambda b, n: (b * nq + jnp.maximum(n - 1, 0), vcol)),
            pl.BlockSpec((bq, KV_DIM), lambda b, n: (b * nq + n, vcol)),
            pl.BlockSpec((bq, KV_DIM), lambda b, n: (b * nq + jnp.minimum(n + 1, nq - 1), vcol)),
            pl.BlockSpec((1, bq, 3 * bq),
                         lambda b, n: (jnp.where(n == 0, 0, jnp.where(n == nq - 1, 2, 1)), 0, 0)),
            pl.BlockSpec((1, n_ctx, KV_DIM), lambda b, n: (b, 0, 0)),
            pl.BlockSpec((1, n_ctx, KV_DIM), lambda b, n: (b, 0, 0)),
        ]
        args = [sink, qkv, qkv, qkv, qkv, qkv, qkv, qkv, jnp.asarray(_window_bias()), ctx_k, ctx_v]
    else:
        bq = seq
        nq = 1
        in_specs = [
            pl.BlockSpec(memory_space=pltpu.SMEM),
            pl.BlockSpec((bq, Q_DIM), lambda b, n: (b, 0)),
            pl.BlockSpec((1, n_ctx, KV_DIM), lambda b, n: (b, 0, 0)),
            pl.BlockSpec((1, n_ctx, KV_DIM), lambda b, n: (b, 0, 0)),
        ]
        args = [sink, qkv, ctx_k, ctx_v]
    return pl.pallas_call(
        functools.partial(_attn_kernel, window=window, bq=bq),
        grid=(n_batch, nq),
        in_specs=in_specs,
        out_specs=pl.BlockSpec((bq, Q_DIM), lambda b, n: (b * nq + n, 0)),
        out_shape=jax.ShapeDtypeStruct((n_batch * seq, Q_DIM), BF16),
        compiler_params=_cparams(("parallel", "parallel")),
        name="attention",
    )(*args)


def _proj_res_kernel(a_ref, w_ref, x_ref, gate_ref, o_ref):
    acc = jnp.dot(a_ref[...], w_ref[...], preferred_element_type=F32)
    o_ref[0] = x_ref[0] + gate_ref[0] * acc


def _proj_residual(a, w, x, gate, bm):
    G, S, D = x.shape
    K = a.shape[1]
    nb = S // bm
    bn = 1024
    return pl.pallas_call(
        _proj_res_kernel,
        grid=(G, nb, D // bn),
        in_specs=[
            pl.BlockSpec((bm, K), lambda gi, i, j: (gi * nb + i, 0)),
            pl.BlockSpec((K, bn), lambda gi, i, j: (0, j)),
            pl.BlockSpec((1, bm, bn), lambda gi, i, j: (gi, i, j)),
            pl.BlockSpec((1, 1, bn), lambda gi, i, j: (gi, 0, j)),
        ],
        out_specs=pl.BlockSpec((1, bm, bn), lambda gi, i, j: (gi, i, j)),
        out_shape=jax.ShapeDtypeStruct((G, S, D), F32),
        compiler_params=_cparams(("parallel", "parallel", "parallel")),
        name="proj_residual",
    )(a, w, x, gate)


def _fill_h_ext(h_scr, xm_ref, xp_ref, xn_ref, g_ref, sh_ref, sc_ref, bm):
    g, sh, sc = g_ref[...], sh_ref[0], sc_ref[0]
    h_scr[0:HALO] = _modulated_norm(xp_ref[0], g, sh, sc).astype(BF16)
    h_scr[HALO:HALO + bm] = _modulated_norm(xm_ref[0], g, sh, sc).astype(BF16)
    h_scr[HALO + bm:] = _modulated_norm(xn_ref[0], g, sh, sc).astype(BF16)


def _dwconv3_rows(u_ext, cw, row0, bm, seq_len):
    assert seq_len & (seq_len - 1) == 0
    pos = (row0 + lax.broadcasted_iota(jnp.int32, (bm, 1), 0)) & (seq_len - 1)
    prev = jnp.where(pos == 0, 0.0, u_ext[HALO - 1:HALO - 1 + bm])
    nxt = jnp.where(pos == seq_len - 1, 0.0, u_ext[HALO + 1:HALO + 1 + bm])
    return prev * cw[0:1] + u_ext[HALO:HALO + bm] * cw[1:2] + nxt * cw[2:3]


def _halo_specs(bm, D, S):
    nh = bm // HALO
    last = S // HALO - 1
    return [
        pl.BlockSpec((1, bm, D), lambda gi, i, j: (gi, i, 0)),
        pl.BlockSpec((1, HALO, D), lambda gi, i, j: (gi, jnp.maximum(i * nh - 1, 0), 0)),
        pl.BlockSpec((1, HALO, D), lambda gi, i, j: (gi, jnp.minimum((i + 1) * nh, last), 0)),
    ]


def _ffn_kernel(*refs, bm, seq_len, final_norm):
    (xm_ref, xp_ref, xn_ref, g_ref, sh_ref, sc_ref, gate_ref,
     wa_ref, wb_ref, cw_ref, wd_ref) = refs[:11]
    pos = 11
    if final_norm:
        gf_ref = refs[pos]
        pos += 1
    o_ref, h_scr = refs[pos:pos + 2]

    i = pl.program_id(1)
    j = pl.program_id(2)
    D = o_ref.shape[-1]

    @pl.when(j == 0)
    def _():
        _fill_h_ext(h_scr, xm_ref, xp_ref, xn_ref, g_ref, sh_ref, sc_ref, bm)
        o_ref[0] = jnp.zeros((bm, D), F32)

    ua = jnp.dot(h_scr[...], wa_ref[...], preferred_element_type=F32)
    ub = jnp.dot(h_scr[HALO:HALO + bm], wb_ref[...], preferred_element_type=F32)
    a = _dwconv3_rows(ua, cw_ref[...], i * bm, bm, seq_len)
    act = (a * (1.0 / (1.0 + jnp.exp(-a))) * ub).astype(BF16)
    for c0 in range(0, D, FFN_DOWN_COLS):
        cols = slice(c0, c0 + FFN_DOWN_COLS)
        o_ref[0, :, cols] += jnp.dot(act, wd_ref[:, cols], preferred_element_type=F32)

    @pl.when(j == pl.num_programs(2) - 1)
    def _():
        y = xm_ref[0] + gate_ref[0] * o_ref[0]
        if final_norm:
            ms = jnp.mean(y * y, axis=-1, keepdims=True)
            y = y * lax.rsqrt(ms + EPS) * gf_ref[...]
        o_ref[0] = y


def _conv_ffn(x, g, shift, scale, gate, w_up, conv_w, w_down, seq_len, bm, bf, g_final=None):
    G, S, D = x.shape
    nb = S // bm
    nc = D_FF // bf
    final_norm = g_final is not None
    in_specs = _halo_specs(bm, D, S) + [
        pl.BlockSpec((1, D), lambda gi, i, j: (0, 0)),
        pl.BlockSpec((1, 1, D), lambda gi, i, j: (gi, 0, 0)),
        pl.BlockSpec((1, 1, D), lambda gi, i, j: (gi, 0, 0)),
        pl.BlockSpec((1, 1, D), lambda gi, i, j: (gi, 0, 0)),
        pl.BlockSpec((D, bf), lambda gi, i, j: (0, j)),
        pl.BlockSpec((D, bf), lambda gi, i, j: (0, nc + j)),
        pl.BlockSpec((3, bf), lambda gi, i, j: (0, j)),
        pl.BlockSpec((bf, D), lambda gi, i, j: (j, 0)),
    ]
    args = [x, x, x, g, shift, scale, gate, w_up, w_up, conv_w, w_down]
    if final_norm:
        in_specs.append(pl.BlockSpec((1, D), lambda gi, i, j: (0, 0)))
        args.append(g_final)
    return pl.pallas_call(
        functools.partial(_ffn_kernel, bm=bm, seq_len=seq_len, final_norm=final_norm),
        grid=(G, nb, nc),
        in_specs=in_specs,
        out_specs=pl.BlockSpec((1, bm, D), lambda gi, i, j: (gi, i, 0), pipeline_mode=pl.Buffered(1)),
        out_shape=jax.ShapeDtypeStruct((G, S, D), F32),
        scratch_shapes=[pltpu.VMEM((bm + 2 * HALO, D), BF16)],
        compiler_params=_cparams(("parallel", "parallel", "arbitrary")),
        name="conv_ffn",
    )(*args)


def _hy_in_kernel(xm_ref, xp_ref, xn_ref, g_ref, sh_ref, sc_ref, w_ref, cw_ref, o_ref, h_scr,
                  *, bm, seq_len):
    i = pl.program_id(1)
    j = pl.program_id(2)

    @pl.when(j == 0)
    def _():
        _fill_h_ext(h_scr, xm_ref, xp_ref, xn_ref, g_ref, sh_ref, sc_ref, bm)

    u = jnp.dot(h_scr[...], w_ref[...], preferred_element_type=F32)
    o_ref[0] = _dwconv3_rows(u, cw_ref[...], i * bm, bm, seq_len)


def _hyena_in(x, g, shift, scale, w_in, conv_w, seq_len, bm, bn):
    G, S, D = x.shape
    nb = S // bm
    N = w_in.shape[1]
    in_specs = _halo_specs(bm, D, S) + [
        pl.BlockSpec((1, D), lambda gi, i, j: (0, 0)),
        pl.BlockSpec((1, 1, D), lambda gi, i, j: (gi, 0, 0)),
        pl.BlockSpec((1, 1, D), lambda gi, i, j: (gi, 0, 0)),
        pl.BlockSpec((D, bn), lambda gi, i, j: (0, j)),
        pl.BlockSpec((3, bn), lambda gi, i, j: (0, j)),
    ]
    return pl.pallas_call(
        functools.partial(_hy_in_kernel, bm=bm, seq_len=seq_len),
        grid=(G, nb, N // bn),
        in_specs=in_specs,
        out_specs=pl.BlockSpec((1, bm, bn), lambda gi, i, j: (gi, i, j)),
        out_shape=jax.ShapeDtypeStruct((G, S, N), F32),
        scratch_shapes=[pltpu.VMEM((bm + 2 * HALO, D), BF16)],
        compiler_params=_cparams(("parallel", "parallel", "arbitrary")),
        name="hyena_in",
    )(x, x, x, g, shift, scale, w_in, conv_w)


class _FftPlan:
    def __init__(self, seq_len):
        self.L = seq_len
        self.n2 = FFT_N2
        self.n1 = max(1, 2 * seq_len // FFT_N2)
        self.n = self.n1 * self.n2
        self.classes = self.n1 // 2 + 1
        self.slab = min(self.n2, seq_len)
        self.data_slabs = seq_len // self.slab

    def stage1_coef(self, m, k1):
        th = 2.0 * math.pi * ((m * k1) % self.n1) / self.n1
        return _snap(math.cos(th)), _snap(-math.sin(th))

    def class_weight(self, k1):
        if self.n1 == 1:
            return 1.0
        return 1.0 if k1 in (0, self.n1 // 2) else 2.0


def _snap(v):
    for t in (0.0, 1.0, -1.0):
        if abs(v - t) < 1e-12:
            return t
    return v


@functools.lru_cache(maxsize=None)
def _dft_consts(n1, n2):
    n = n1 * n2
    k = np.arange(n2)
    ang = 2.0 * np.pi * np.outer(k, k) / n2
    fr, fi = np.cos(ang), -np.sin(ang)
    fwd = np.block([[fr, -fi], [fi, fr]]).astype(np.float32)
    inv = np.block([[fr, fi], [-fi, fr]]).astype(np.float32)
    classes = n1 // 2 + 1
    tw = 2.0 * np.pi * np.outer(np.arange(classes), np.arange(n2)) / n
    twc = np.repeat(np.cos(tw)[:, :, None], TW_LANES, axis=2).astype(np.float32)
    tws = np.repeat((-np.sin(tw))[:, :, None], TW_LANES, axis=2).astype(np.float32)
    return fwd, inv, twc, tws


def _const_spec(shape):
    nd = len(shape)
    return pl.BlockSpec(shape, lambda *_: (0,) * nd, pipeline_mode=pl.Buffered(1))


def _lane_tile(t, lanes):
    reps = lanes // t.shape[1]
    return t if reps == 1 else jnp.concatenate([t] * reps, axis=1)


def _axpy(acc, c, x):
    if c == 0.0:
        return acc
    t = x if c == 1.0 else (-x if c == -1.0 else c * x)
    return t if acc is None else acc + t


def _fft_forward_class(plan, k1, n_slabs, load_rows, twc_ref, tws_ref, abuf, slot):
    n2, R = plan.n2, FFT_ROWS
    lanes = abuf.shape[-1]
    for off in range(0, n2, R):
        ar = ai = None
        for m in range(n_slabs):
            c, s = plan.stage1_coef(m, k1)
            x = load_rows(m * n2 + off, R)
            ar = _axpy(ar, c, x)
            ai = _axpy(ai, s, x)
        if k1 == 0:
            pr, pi = ar, ai
        else:
            tc = _lane_tile(twc_ref[k1, off:off + R, :], lanes)
            ts = _lane_tile(tws_ref[k1, off:off + R, :], lanes)
            if ai is None:
                pr, pi = ar * tc, ar * ts
            else:
                pr, pi = ar * tc - ai * ts, ar * ts + ai * tc
        abuf[slot, off:off + R, :] = pr.astype(BF16)
        if pi is None:
            abuf[slot, n2 + off:n2 + off + R, :] = jnp.zeros((R, lanes), BF16)
        else:
            abuf[slot, n2 + off:n2 + off + R, :] = pi.astype(BF16)


def _fftconv_kernel(z_ref, gate_ref, bias_ref, h_ref, fwd_ref, inv_ref, twc_ref, tws_ref,
                    o_ref, abuf, ybuf, b_scr, acc_scr, *, plan):
    n2, L, R = plan.n2, plan.L, FFT_ROWS
    lanes = o_ref.shape[-1]
    inv_n = 1.0 / plan.n

    def dots(k1, slot):
        xf = jnp.dot(fwd_ref[...], abuf[slot], preferred_element_type=F32)
        xr, xi = xf[:n2], xf[n2:]
        hr, hi = h_ref[k1, 0:n2], h_ref[k1, n2:]
        ybuf[slot, 0:n2] = (xr * hr - xi * hi).astype(BF16)
        ybuf[slot, n2:] = (xr * hi + xi * hr).astype(BF16)
        b_scr[slot] = jnp.dot(inv_ref[...], ybuf[slot], preferred_element_type=F32)

    if plan.n1 == 1:
        z = z_ref[0]
        abuf[0, 0:L] = z.astype(BF16)
        abuf[0, L:] = jnp.zeros((2 * n2 - L, lanes), BF16)
        dots(0, 0)
        y = b_scr[0, 0:L] * inv_n + bias_ref[...] * z
        o_ref[0] = (gate_ref[0] * y).astype(o_ref.dtype)
        return

    def forward(k1):
        _fft_forward_class(plan, k1, plan.data_slabs,
                           lambda start, rows: z_ref[0, start:start + rows, :],
                           twc_ref, tws_ref, abuf, k1 % 2)

    def inverse(k1, last):
        slot = k1 % 2
        wk = plan.class_weight(k1)
        for off in range(0, n2, R):
            br = b_scr[slot, off:off + R, :]
            bi = b_scr[slot, n2 + off:n2 + off + R, :]
            if k1 == 0:
                pr, pi = br, bi
            else:
                tc = _lane_tile(twc_ref[k1, off:off + R, :], lanes)
                ts = _lane_tile(tws_ref[k1, off:off + R, :], lanes)
                pr, pi = br * tc + bi * ts, bi * tc - br * ts
            for m in range(plan.data_slabs):
                c, s = plan.stage1_coef(m, k1)
                t = _axpy(_axpy(None, wk * c, pr), wk * s, pi)
                rows = slice(m * n2 + off, m * n2 + off + R)
                if last:
                    z = z_ref[0, rows, :]
                    y = (acc_scr[rows, :] + t) * inv_n + bias_ref[...] * z
                    o_ref[0, rows, :] = (gate_ref[0, rows, :] * y).astype(o_ref.dtype)
                elif k1 == 0:
                    acc_scr[rows, :] = t
                else:
                    acc_scr[rows, :] += t

    forward(0)
    for k1 in range(plan.classes):
        if k1 + 1 < plan.classes:
            forward(k1 + 1)
        dots(k1, k1 % 2)
        if k1 >= 1:
            inverse(k1 - 1, False)
    inverse(plan.classes - 1, True)


def _fftconv(plan, z, z_col, gate, gate_col, bias, hspec, n_seq, lanes, out_dtype):
    L, n2 = plan.L, plan.n2
    nc = D_MODEL // lanes
    fwd, inv, twc, tws = _dft_consts(plan.n1, n2)
    zc, gc = z_col // lanes, gate_col // lanes
    acc_rows = L if plan.n1 > 1 else 8
    return pl.pallas_call(
        functools.partial(_fftconv_kernel, plan=plan),
        grid=(nc, n_seq),
        in_specs=[
            pl.BlockSpec((1, L, lanes), lambda c, b: (b, 0, zc + c)),
            pl.BlockSpec((1, L, lanes), lambda c, b: (b, 0, gc + c)),
            pl.BlockSpec((1, lanes), lambda c, b: (0, c)),
            pl.BlockSpec((plan.classes, 2 * n2, lanes), lambda c, b: (0, 0, c),
                         pipeline_mode=pl.Buffered(1)),
            _const_spec((2 * n2, 2 * n2)),
            _const_spec((2 * n2, 2 * n2)),
            _const_spec((plan.classes, n2, TW_LANES)),
            _const_spec((plan.classes, n2, TW_LANES)),
        ],
        out_specs=pl.BlockSpec((1, L, lanes), lambda c, b: (b, 0, c)),
        out_shape=jax.ShapeDtypeStruct((n_seq, L, D_MODEL), out_dtype),
        scratch_shapes=[
            pltpu.VMEM((2, 2 * n2, lanes), BF16),
            pltpu.VMEM((2, 2 * n2, lanes), BF16),
            pltpu.VMEM((2, 2 * n2, lanes), F32),
            pltpu.VMEM((acc_rows, lanes), F32),
        ],
        compiler_params=_cparams(("parallel", "arbitrary")),
        name="hyena_fftconv",
    )(z, gate, bias, hspec, jnp.asarray(fwd).astype(BF16), jnp.asarray(inv).astype(BF16),
      jnp.asarray(twc), jnp.asarray(tws))


@functools.lru_cache(maxsize=None)
def _filter_tables(L):
    r = np.arange(2 * L)
    p = np.where(r < L, r, 2 * L - r)
    p[L] = 0
    t = p / (L - 1.0)
    w = 2.0 * np.pi * p / L
    f = np.linspace(1e-4, HY_BANDS - 1, HY_BANDS)
    feat = np.concatenate([t[:, None], np.cos(np.outer(w, f)), -np.sin(np.outer(w, f))], axis=1)
    feat = np.pad(feat, ((0, 0), (0, HY_EMB_PAD - HY_EMB))).astype(np.float32)
    t_tab = np.repeat(t[:, None], TW_LANES, axis=1).astype(np.float32)
    return feat, t_tab


def _filter_hidden_kernel(feat_ref, w1_ref, b1_ref, w2_ref, b2_ref, fq_ref, o_ref):
    hp = lax.Precision.HIGHEST
    h = jnp.sin(fq_ref[...] * (jnp.dot(feat_ref[...], w1_ref[...], preferred_element_type=F32,
                                        precision=hp) + b1_ref[...]))
    h = jnp.sin(fq_ref[...] * (jnp.dot(h, w2_ref[...], preferred_element_type=F32,
                                        precision=hp) + b2_ref[...]))
    o_ref[...] = h


def _filter_hidden(feat, w1, b1, w2, b2, freq):
    rows = feat.shape[0]
    br = 1024 if rows % 1024 == 0 else rows
    vec = pl.BlockSpec((1, HY_FH), lambda i: (0, 0))
    return pl.pallas_call(
        _filter_hidden_kernel,
        grid=(rows // br,),
        in_specs=[
            pl.BlockSpec((br, HY_EMB_PAD), lambda i: (i, 0)),
            pl.BlockSpec((HY_EMB_PAD, HY_FH), lambda i: (0, 0)),
            vec,
            pl.BlockSpec((HY_FH, HY_FH), lambda i: (0, 0)),
            vec, vec,
        ],
        out_specs=pl.BlockSpec((br, HY_FH), lambda i: (i, 0)),
        out_shape=jax.ShapeDtypeStruct((rows, HY_FH), F32),
        compiler_params=_cparams(("parallel",)),
        name="hyena_filter_hidden",
    )(feat, w1, b1, w2, b2, freq)


def _filter_spec_kernel(hid_ref, w3a_ref, w3b_ref, t_ref, delta_ref, fwd_ref, twc_ref, tws_ref,
                        o_ref, f_scr, abuf, *, plan):
    L, n2 = plan.L, plan.n2
    lanes = o_ref.shape[-1]
    hp = lax.Precision.HIGHEST
    absd = jnp.abs(delta_ref[...])
    rows_per = min(L, 512)

    def gen(r, carry):
        for base, w_ref in ((0, w3a_ref), (L, w3b_ref)):
            rows = pl.ds(pl.multiple_of(base + r * rows_per, rows_per), rows_per)
            f = jnp.dot(hid_ref[rows, :], w_ref[0], preferred_element_type=F32, precision=hp)
            f_scr[rows, :] = f * jnp.exp(-_lane_tile(t_ref[rows, :], lanes) * absd)
        return carry

    lax.fori_loop(0, L // rows_per, gen, 0)
    f_scr[L:L + 8] = jnp.where(lax.broadcasted_iota(jnp.int32, (8, 1), 0) > 0, f_scr[L:L + 8], 0.0)

    if plan.n1 == 1:
        abuf[0, 0:n2] = f_scr[...].astype(BF16)
        abuf[0, n2:] = jnp.zeros((n2, lanes), BF16)
    for k1 in range(plan.classes):
        slot = k1 % 2
        if plan.n1 > 1:
            _fft_forward_class(plan, k1, plan.n1,
                               lambda start, rows: f_scr[start:start + rows, :],
                               twc_ref, tws_ref, abuf, slot)
        o_ref[0, k1] = jnp.dot(fwd_ref[...], abuf[slot], preferred_element_type=F32)


def _filter_spectrum(plan, hidden, w_f3, lanes):
    L, n2 = plan.L, plan.n2
    nc = D_MODEL // lanes
    fwd, _, twc, tws = _dft_consts(plan.n1, n2)
    _, t_tab = _filter_tables(L)
    delta = np.linspace(HY_MIN_DECAY, HY_MAX_DECAY, D_MODEL).astype(np.float32)[None, :]
    w3 = w_f3.reshape(HY_FH, 2, 2, D_MODEL).transpose(1, 2, 0, 3)
    return pl.pallas_call(
        functools.partial(_filter_spec_kernel, plan=plan),
        grid=(2, nc),
        in_specs=[
            _const_spec((2 * L, HY_FH)),
            pl.BlockSpec((None, 1, HY_FH, lanes), lambda o, c: (0, o, 0, c)),
            pl.BlockSpec((None, 1, HY_FH, lanes), lambda o, c: (1, o, 0, c)),
            _const_spec((2 * L, TW_LANES)),
            pl.BlockSpec((1, lanes), lambda o, c: (0, c)),
            _const_spec((2 * n2, 2 * n2)),
            _const_spec((plan.classes, n2, TW_LANES)),
            _const_spec((plan.classes, n2, TW_LANES)),
        ],
        out_specs=pl.BlockSpec((1, plan.classes, 2 * n2, lanes), lambda o, c: (o, 0, 0, c)),
        out_shape=jax.ShapeDtypeStruct((2, plan.classes, 2 * n2, D_MODEL), F32),
        scratch_shapes=[pltpu.VMEM((2 * L, lanes), F32), pltpu.VMEM((2, 2 * n2, lanes), BF16)],
        compiler_params=_cparams(("parallel", "parallel")),
        name="hyena_filter_spectrum",
    )(hidden, w3, w3, jnp.asarray(t_tab), jnp.asarray(delta), jnp.asarray(fwd).astype(BF16),
      jnp.asarray(twc), jnp.asarray(tws))


@functools.lru_cache(maxsize=None)
def _rope_tables(L):
    t = np.arange(L)
    inv = ROPE_THETA ** (-np.arange(ROPE_PAIRS, dtype=np.float32) / ROPE_PAIRS)
    ang_row = (t // GRID_W).astype(np.float32)[:, None] * inv
    ang_col = (t % GRID_W).astype(np.float32)[:, None] * inv
    zero = np.zeros_like(ang_row)
    cos = np.concatenate([np.cos(ang_row)] * 2 + [np.cos(ang_col)] * 2, axis=1)
    sa = np.concatenate([-np.sin(ang_row), zero, -np.sin(ang_col), zero], axis=1)
    sb = np.concatenate([zero, np.sin(ang_row), zero, np.sin(ang_col)], axis=1)
    return cos.astype(np.float32), sa.astype(np.float32), sb.astype(np.float32)


def _hyena_mixer(x, g, shift, scale, gate, w_in, conv_w, hspec, bias, w_out, seq_len, lanes, bm):
    G, S, D = x.shape
    plan = _FftPlan(seq_len)
    n_seq = G * S // seq_len
    u = _hyena_in(x, g, shift, scale, w_in, conv_w, seq_len, bm, 1024)
    u = u.reshape(n_seq, seq_len, 3 * D)
    z1 = _fftconv(plan, u, 2 * D, u, 0, bias[0:1], hspec[0], n_seq, lanes, F32)
    z2 = _fftconv(plan, z1, 0, u, D, bias[1:2], hspec[1], n_seq, lanes, BF16)
    return _proj_residual(z2.reshape(G * S, D), w_out, x, gate, bm)


def kernel(x_prompt, x_sample, cache_k, cache_v, c, c_ctx, w_mod, b_mod, norm_mix, norm_ffn, norm_final, w_qkv, w_o, attn_sink, hy_w_in, hy_conv, hy_w_f1, hy_b_f1, hy_w_f2, hy_b_f2, hy_w_f3, hy_freq, hy_bias, hy_w_out, ffn_w_up, ffn_conv, ffn_w_down):
    B, SEQ, D = x_prompt.shape
    DB, DSEQ, _ = x_sample.shape
    depth = w_mod.shape[0]

    n_cond = DB + 1
    cvec = jnp.concatenate([c, c_ctx[None, :], jnp.zeros((16 - n_cond, D), F32)], axis=0)
    mod = _modulation(cvec, w_mod, b_mod).reshape(depth, 16, 6, 1, D)

    xs = x_sample
    xp = x_prompt.reshape(1, B * SEQ, D)
    streams = [
        dict(x=xs, rows=slice(0, DB), seq=DSEQ, n_seq=DB, lanes=256),
        dict(x=xp, rows=slice(DB, DB + 1), seq=SEQ, n_seq=B, lanes=256),
    ]
    bm = 1024
    rope_tabs = tuple(jnp.asarray(t) for t in _rope_tables(DSEQ))
    new_k, new_v = [], []

    for i in range(depth):
        g_mix = norm_mix[i][None, :]
        g_ffn = norm_ffn[i][None, :]
        if i % 2 == 0:
            a = i // 2
            wq = w_qkv[a].astype(BF16)
            wo = w_o[a].astype(BF16)
        else:
            hl = i // 2
            w_in = hy_w_in[hl].astype(BF16)
            w_out = hy_w_out[hl].astype(BF16)
            w1 = jnp.pad(hy_w_f1[hl], ((0, HY_EMB_PAD - HY_EMB), (0, 0)))
        w_up = ffn_w_up[i].astype(BF16)
        w_down = ffn_w_down[i].astype(BF16)
        g_final = norm_final[None, :] if i == depth - 1 else None

        for si, st in enumerate(streams):
            x = st["x"]
            m = mod[i, st["rows"]]
            sh1, sc1, g1, sh2, sc2, g2 = (m[:, t] for t in range(6))
            latent = si == 0
            if i % 2 == 0:
                res = _qkv_proj(x, g_mix, sh1, sc1, wq, rope_tabs if latent else None,
                                emit_kv=not latent, bm=bm)
                qkv = res[0]
                if latent:
                    ck = cache_k[:, a].reshape(DB, -1, KV_DIM).astype(BF16)
                    cv = cache_v[:, a].reshape(DB, -1, KV_DIM).astype(BF16)
                    att = _attention(qkv, attn_sink[a], ck, cv, DB, DSEQ, window=True)
                else:
                    kv = res[1]
                    new_k.append(kv[:, :KV_DIM].reshape(B, 1, SEQ, N_KV_HEADS, HEAD_DIM))
                    new_v.append(kv[:, KV_DIM:].reshape(B, 1, SEQ, N_KV_HEADS, HEAD_DIM))
                    ck = qkv[:, Q_DIM:Q_DIM + KV_DIM].reshape(B, SEQ, KV_DIM)
                    cv = qkv[:, Q_DIM + KV_DIM:].reshape(B, SEQ, KV_DIM)
                    att = _attention(qkv, attn_sink[a], ck, cv, B, SEQ, window=False)
                x = _proj_residual(att, wo, x, g1, bm)
            else:
                plan = _FftPlan(st["seq"])
                feat, _ = _filter_tables(st["seq"])
                hidden = _filter_hidden(jnp.asarray(feat), w1, hy_b_f1[hl][None, :], hy_w_f2[hl],
                                        hy_b_f2[hl][None, :], hy_freq[hl][None, :])
                hspec = _filter_spectrum(plan, hidden, hy_w_f3[hl], st["lanes"])
                x = _hyena_mixer(x, g_mix, sh1, sc1, g1, w_in, hy_conv[hl], hspec, hy_bias[hl],
                                 w_out, st["seq"], st["lanes"], bm)
            x = _conv_ffn(x, g_ffn, sh2, sc2, g2, w_up, ffn_conv[i], w_down, st["seq"], bm, 512,
                          g_final=g_final)
            st["x"] = x

    y_sample = streams[0]["x"]
    y_prompt = streams[1]["x"].reshape(B, SEQ, D)
    new_cache_k = new_k[0] if len(new_k) == 1 else jnp.concatenate(new_k, axis=1)
    new_cache_v = new_v[0] if len(new_v) == 1 else jnp.concatenate(new_v, axis=1)
    return (y_prompt, y_sample, new_cache_k, new_cache_v)
```

```python
import functools
import math

import jax
import jax.numpy as jnp
import numpy as np
from jax import lax
from jax.experimental import pallas as pl
from jax.experimental.pallas import tpu as pltpu

F32 = jnp.float32
BF16 = jnp.bfloat16

D_MODEL = 2048
HEAD_DIM = 128
N_HEADS = 16
N_KV_HEADS = 4
GROUP = N_HEADS // N_KV_HEADS
Q_DIM = N_HEADS * HEAD_DIM
KV_DIM = N_KV_HEADS * HEAD_DIM
QKV_DIM = Q_DIM + 2 * KV_DIM
GRID_W = 64
WINDOW = 128
ROPE_THETA = 10000.0
ROPE_PAIRS = HEAD_DIM // 4
HY_BANDS = 16
HY_EMB = 1 + 2 * HY_BANDS
HY_EMB_PAD = 40
HY_FH = 64
HY_MIN_DECAY = math.log(1e-2) / 1.5
HY_MAX_DECAY = math.log(1e-2) / 0.3
D_FF = 5632
EPS = 1e-6
NEG_INF = -1e30

VMEM_LIMIT = 56 * 1024 * 1024
HALO = 16
FFT_N2 = 512
FFT_ROWS = 32
QKV_BN = 1024
ATTN_ROWS = 32
FFN_DOWN_COLS = 1024
TW_LANES = 128


def _cparams(sem):
    return pltpu.CompilerParams(dimension_semantics=sem, vmem_limit_bytes=VMEM_LIMIT)


def _modulated_norm(x, g, shift, scale):
    ms = jnp.mean(x * x, axis=-1, keepdims=True)
    y = x * lax.rsqrt(ms + EPS) * g
    return y * (1.0 + scale) + shift


def _mod_kernel(c_ref, w_ref, b_ref, o_ref):
    c = c_ref[...]
    s = c * (1.0 / (1.0 + jnp.exp(-c)))
    o_ref[0] = jnp.dot(s, w_ref[0], preferred_element_type=F32,
                       precision=lax.Precision.HIGHEST) + b_ref[0]


def _modulation(cvec, w_mod, b_mod):
    depth, d, n = w_mod.shape
    rows = cvec.shape[0]
    bn = 1024
    return pl.pallas_call(
        _mod_kernel,
        grid=(depth, n // bn),
        in_specs=[
            pl.BlockSpec((rows, d), lambda l, j: (0, 0)),
            pl.BlockSpec((1, d, bn), lambda l, j: (l, 0, j)),
            pl.BlockSpec((1, 1, bn), lambda l, j: (l, 0, j)),
        ],
        out_specs=pl.BlockSpec((1, rows, bn), lambda l, j: (l, 0, j)),
        out_shape=jax.ShapeDtypeStruct((depth, rows, n), F32),
        compiler_params=_cparams(("parallel", "parallel")),
        name="modulation",
    )(cvec, w_mod, b_mod.reshape(depth, 1, n))


def _rope(x, cos, sa, sb):
    return x * cos + pltpu.roll(x, HEAD_DIM - 32, axis=1) * sa + pltpu.roll(x, 32, axis=1) * sb


def _qkv_kernel(*refs, rope, emit_kv):
    x_ref, g_ref, sh_ref, sc_ref, w_ref = refs[:5]
    pos = 5
    if rope:
        cos_ref, sa_ref, sb_ref = refs[pos:pos + 3]
        pos += 3
    o_ref = refs[pos]
    pos += 1
    if emit_kv:
        kv_ref = refs[pos]
        pos += 1
    h_scr = refs[pos]

    j = pl.program_id(2)
    n_q = Q_DIM // QKV_BN
    heads = QKV_BN // HEAD_DIM

    @pl.when(j == 0)
    def _():
        h_scr[...] = _modulated_norm(x_ref[0], g_ref[...], sh_ref[0], sc_ref[0]).astype(BF16)

    acc = jnp.dot(h_scr[...], w_ref[...], preferred_element_type=F32)

    def store(scale, n_rot):
        for h in range(heads):
            sl = slice(h * HEAD_DIM, (h + 1) * HEAD_DIM)
            xh = acc[:, sl]
            if rope and h < n_rot:
                xh = _rope(xh, cos_ref[...], sa_ref[...], sb_ref[...])
            if scale != 1.0:
                xh = xh * scale
            o_ref[:, sl] = xh.astype(BF16)

    @pl.when(j < n_q)
    def _():
        store(HEAD_DIM ** -0.5, heads)

    @pl.when(j == n_q)
    def _():
        if emit_kv:
            kv_ref[...] = acc
        store(1.0, N_KV_HEADS)


def _qkv_proj(x, g, shift, scale, w, rope_tabs, emit_kv, bm):
    G, S, D = x.shape
    nb = S // bm
    bn = QKV_BN
    assert 2 * KV_DIM == bn and Q_DIM % bn == 0
    nj = QKV_DIM // bn
    rope = rope_tabs is not None
    in_specs = [
        pl.BlockSpec((1, bm, D), lambda gi, i, j: (gi, i, 0)),
        pl.BlockSpec((1, D), lambda gi, i, j: (0, 0)),
        pl.BlockSpec((1, 1, D), lambda gi, i, j: (gi, 0, 0)),
        pl.BlockSpec((1, 1, D), lambda gi, i, j: (gi, 0, 0)),
        pl.BlockSpec((D, bn), lambda gi, i, j: (0, j)),
    ]
    args = [x, g, shift, scale, w]
    if rope:
        for t in rope_tabs:
            in_specs.append(pl.BlockSpec((bm, HEAD_DIM), lambda gi, i, j: (i, 0)))
            args.append(t)
    out_specs = [pl.BlockSpec((bm, bn), lambda gi, i, j: (gi * nb + i, j))]
    out_shape = [jax.ShapeDtypeStruct((G * S, QKV_DIM), BF16)]
    if emit_kv:
        out_specs.append(pl.BlockSpec((bm, bn), lambda gi, i, j: (gi * nb + i, 0)))
        out_shape.append(jax.ShapeDtypeStruct((G * S, 2 * KV_DIM), F32))
    res = pl.pallas_call(
        functools.partial(_qkv_kernel, rope=rope, emit_kv=emit_kv),
        grid=(G, nb, nj),
        in_specs=in_specs,
        out_specs=out_specs,
        out_shape=out_shape,
        scratch_shapes=[pltpu.VMEM((bm, D), BF16)],
        compiler_params=_cparams(("parallel", "parallel", "arbitrary")),
        name="qkv_proj",
    )(*args)
    return res


def _attn_kernel(*refs, window, bq):
    sink_ref, q_ref = refs[:2]
    pos = 2
    if window:
        kp_ref, kc_ref, kn_ref, vp_ref, vc_ref, vn_ref, bias_ref = refs[pos:pos + 7]
        pos += 7
    ck_ref, cv_ref, o_ref, k_scr, v_scr, s_scr, p_scr = refs[pos:pos + 7]

    n_ctx = ck_ref.shape[1]
    n_win = 3 * bq if window else 0
    tr = ATTN_ROWS

    for h in range(N_KV_HEADS):
        slot = h % 2
        hs = slice(h * HEAD_DIM, (h + 1) * HEAD_DIM)
        if window:
            for t, (k_ref, v_ref) in enumerate(((kp_ref, vp_ref), (kc_ref, vc_ref), (kn_ref, vn_ref))):
                k_scr[slot, t * bq:(t + 1) * bq] = k_ref[:, hs]
                v_scr[slot, t * bq:(t + 1) * bq] = v_ref[:, hs]
        k_scr[slot, n_win:] = ck_ref[0, :, hs]
        v_scr[slot, n_win:] = cv_ref[0, :, hs]
        qs = jnp.concatenate(
            [q_ref[:, (h * GROUP + g) * HEAD_DIM:(h * GROUP + g + 1) * HEAD_DIM] for g in range(GROUP)],
            axis=0)
        s_scr[slot] = lax.dot_general(qs, k_scr[slot], (((1,), (1,)), ((), ())),
                                      preferred_element_type=F32)
        for r0 in range(0, GROUP * bq, tr):
            sk = sink_ref[h * GROUP + r0 // bq]
            s = s_scr[slot, r0:r0 + tr, :]
            if window:
                q0 = r0 % bq
                s = s + bias_ref[0, q0:q0 + tr, :]
            m = jnp.maximum(jnp.max(s, axis=-1, keepdims=True), sk)
            p = jnp.exp(s - m)
            denom = jnp.sum(p, axis=-1, keepdims=True) + jnp.exp(sk - m)
            p_scr[slot, r0:r0 + tr, :] = (p * (1.0 / denom)).astype(BF16)
        o = jnp.dot(p_scr[slot], v_scr[slot], preferred_element_type=F32)
        for g in range(GROUP):
            c0 = (h * GROUP + g) * HEAD_DIM
            o_ref[:, c0:c0 + HEAD_DIM] = o[g * bq:(g + 1) * bq].astype(BF16)


def _window_bias(n_ctx):
    i = np.arange(WINDOW)[:, None]
    j = np.arange(WINDOW)[None, :]
    prev = np.where(j >= i, 0.0, NEG_INF)
    cur = np.zeros((WINDOW, WINDOW))
    nxt = np.where(j <= i, 0.0, NEG_INF)
    dead = np.full((WINDOW, WINDOW), NEG_INF)
    ctx = np.zeros((WINDOW, n_ctx))
    first = np.concatenate([dead, cur, nxt, ctx], axis=1)
    mid = np.concatenate([prev, cur, nxt, ctx], axis=1)
    last = np.concatenate([prev, cur, dead, ctx], axis=1)
    return np.stack([first, mid, last]).astype(np.float32)


def _attention(qkv, sink, ctx_k, ctx_v, n_batch, seq, window):
    kcol = Q_DIM // KV_DIM
    vcol = kcol + 1
    n_ctx = ctx_k.shape[1]
    if window:
        bq = WINDOW
        nq = seq // bq
        in_specs = [
            pl.BlockSpec(memory_space=pltpu.SMEM),
            pl.BlockSpec((bq, Q_DIM), lambda b, n: (b * nq + n, 0)),
            pl.BlockSpec((bq, KV_DIM), lambda b, n: (b * nq + jnp.maximum(n - 1, 0), kcol)),
            pl.BlockSpec((bq, KV_DIM), lambda b, n: (b * nq + n, kcol)),
            pl.BlockSpec((bq, KV_DIM), lambda b, n: (b * nq + jnp.minimum(n + 1, nq - 1), kcol)),
            pl.BlockSpec((bq, KV_DIM), lambda b, n: (b * nq + jnp.maximum(n - 1, 0), vcol)),
            pl.BlockSpec((bq, KV_DIM), lambda b, n: (b * nq + n, vcol)),
            pl.BlockSpec((bq, KV_DIM), lambda b, n: (b * nq + jnp.minimum(n + 1, nq - 1), vcol)),
            pl.BlockSpec((1, bq, 3 * bq + n_ctx),
                         lambda b, n: (jnp.where(n == 0, 0, jnp.where(n == nq - 1, 2, 1)), 0, 0)),
            pl.BlockSpec((1, n_ctx, KV_DIM), lambda b, n: (b, 0, 0)),
            pl.BlockSpec((1, n_ctx, KV_DIM), lambda b, n: (b, 0, 0)),
        ]
        args = [sink, qkv, qkv, qkv, qkv, qkv, qkv, qkv, jnp.asarray(_window_bias(n_ctx)), ctx_k, ctx_v]
    else:
        bq = seq
        nq = 1
        in_specs = [
            pl.BlockSpec(memory_space=pltpu.SMEM),
            pl.BlockSpec((bq, Q_DIM), lambda b, n: (b, 0)),
            pl.BlockSpec((1, n_ctx, KV_DIM), lambda b, n: (b, 0, 0)),
            pl.BlockSpec((1, n_ctx, KV_DIM), lambda b, n: (b, 0, 0)),
        ]
        args = [sink, qkv, ctx_k, ctx_v]
    n_keys = (3 * bq if window else 0) + n_ctx
    return pl.pallas_call(
        functools.partial(_attn_kernel, window=window, bq=bq),
        grid=(n_batch, nq),
        in_specs=in_specs,
        out_specs=pl.BlockSpec((bq, Q_DIM), lambda b, n: (b * nq + n, 0)),
        out_shape=jax.ShapeDtypeStruct((n_batch * seq, Q_DIM), BF16),
        scratch_shapes=[
            pltpu.VMEM((2, n_keys, HEAD_DIM), BF16),
            pltpu.VMEM((2, n_keys, HEAD_DIM), BF16),
            pltpu.VMEM((2, GROUP * bq, n_keys), F32),
            pltpu.VMEM((2, GROUP * bq, n_keys), BF16),
        ],
        compiler_params=_cparams(("parallel", "parallel")),
        name="attention",
    )(*args)


def _proj_res_kernel(a_ref, w_ref, x_ref, gate_ref, o_ref):
    acc = jnp.dot(a_ref[...], w_ref[...], preferred_element_type=F32)
    o_ref[0] = x_ref[0] + gate_ref[0] * acc


def _proj_residual(a, w, x, gate, bm):
    G, S, D = x.shape
    K = a.shape[1]
    nb = S // bm
    bn = 1024
    return pl.pallas_call(
        _proj_res_kernel,
        grid=(G, nb, D // bn),
        in_specs=[
            pl.BlockSpec((bm, K), lambda gi, i, j: (gi * nb + i, 0)),
            pl.BlockSpec((K, bn), lambda gi, i, j: (0, j)),
            pl.BlockSpec((1, bm, bn), lambda gi, i, j: (gi, i, j)),
            pl.BlockSpec((1, 1, bn), lambda gi, i, j: (gi, 0, j)),
        ],
        out_specs=pl.BlockSpec((1, bm, bn), lambda gi, i, j: (gi, i, j)),
        out_shape=jax.ShapeDtypeStruct((G, S, D), F32),
        compiler_params=_cparams(("parallel", "parallel", "parallel")),
        name="proj_residual",
    )(a, w, x, gate)


def _fill_h_ext(h_scr, xm_ref, xp_ref, xn_ref, g_ref, sh_ref, sc_ref, bm):
    g, sh, sc = g_ref[...], sh_ref[0], sc_ref[0]
    h_scr[0:HALO] = _modulated_norm(xp_ref[0], g, sh, sc).astype(BF16)
    h_scr[HALO:HALO + bm] = _modulated_norm(xm_ref[0], g, sh, sc).astype(BF16)
    h_scr[HALO + bm:] = _modulated_norm(xn_ref[0], g, sh, sc).astype(BF16)


def _dwconv3_rows(u_ext, cw, row0, bm, seq_len):
    assert seq_len & (seq_len - 1) == 0
    pos = (row0 + lax.broadcasted_iota(jnp.int32, (bm, 1), 0)) & (seq_len - 1)
    prev = jnp.where(pos == 0, 0.0, u_ext[HALO - 1:HALO - 1 + bm])
    nxt = jnp.where(pos == seq_len - 1, 0.0, u_ext[HALO + 1:HALO + 1 + bm])
    return prev * cw[0:1] + u_ext[HALO:HALO + bm] * cw[1:2] + nxt * cw[2:3]


def _halo_specs(bm, D, S):
    nh = bm // HALO
    last = S // HALO - 1
    return [
        pl.BlockSpec((1, bm, D), lambda gi, i, j: (gi, i, 0)),
        pl.BlockSpec((1, HALO, D), lambda gi, i, j: (gi, jnp.maximum(i * nh - 1, 0), 0)),
        pl.BlockSpec((1, HALO, D), lambda gi, i, j: (gi, jnp.minimum((i + 1) * nh, last), 0)),
    ]


def _ffn_kernel(*refs, bm, seq_len, final_norm):
    (xm_ref, xp_ref, xn_ref, g_ref, sh_ref, sc_ref, gate_ref,
     wa_ref, wb_ref, cw_ref, wd_ref) = refs[:11]
    pos = 11
    if final_norm:
        gf_ref = refs[pos]
        pos += 1
    o_ref, h_scr = refs[pos:pos + 2]

    i = pl.program_id(1)
    j = pl.program_id(2)
    D = o_ref.shape[-1]

    @pl.when(j == 0)
    def _():
        _fill_h_ext(h_scr, xm_ref, xp_ref, xn_ref, g_ref, sh_ref, sc_ref, bm)
        o_ref[0] = jnp.zeros((bm, D), F32)

    ua = jnp.dot(h_scr[...], wa_ref[...], preferred_element_type=F32)
    ub = jnp.dot(h_scr[HALO:HALO + bm], wb_ref[...], preferred_element_type=F32)
    a = _dwconv3_rows(ua, cw_ref[...], i * bm, bm, seq_len)
    act = (a * (1.0 / (1.0 + jnp.exp(-a))) * ub).astype(BF16)
    for c0 in range(0, D, FFN_DOWN_COLS):
        cols = slice(c0, c0 + FFN_DOWN_COLS)
        o_ref[0, :, cols] += jnp.dot(act, wd_ref[:, cols], preferred_element_type=F32)

    @pl.when(j == pl.num_programs(2) - 1)
    def _():
        y = xm_ref[0] + gate_ref[0] * o_ref[0]
        if final_norm:
            ms = jnp.mean(y * y, axis=-1, keepdims=True)
            y = y * lax.rsqrt(ms + EPS) * gf_ref[...]
        o_ref[0] = y


def _conv_ffn(x, g, shift, scale, gate, w_up, conv_w, w_down, seq_len, bm, bf, g_final=None):
    G, S, D = x.shape
    nb = S // bm
    nc = D_FF // bf
    final_norm = g_final is not None
    in_specs = _halo_specs(bm, D, S) + [
        pl.BlockSpec((1, D), lambda gi, i, j: (0, 0)),
        pl.BlockSpec((1, 1, D), lambda gi, i, j: (gi, 0, 0)),
        pl.BlockSpec((1, 1, D), lambda gi, i, j: (gi, 0, 0)),
        pl.BlockSpec((1, 1, D), lambda gi, i, j: (gi, 0, 0)),
        pl.BlockSpec((D, bf), lambda gi, i, j: (0, j)),
        pl.BlockSpec((D, bf), lambda gi, i, j: (0, nc + j)),
        pl.BlockSpec((3, bf), lambda gi, i, j: (0, j)),
        pl.BlockSpec((bf, D), lambda gi, i, j: (j, 0)),
    ]
    args = [x, x, x, g, shift, scale, gate, w_up, w_up, conv_w, w_down]
    if final_norm:
        in_specs.append(pl.BlockSpec((1, D), lambda gi, i, j: (0, 0)))
        args.append(g_final)
    return pl.pallas_call(
        functools.partial(_ffn_kernel, bm=bm, seq_len=seq_len, final_norm=final_norm),
        grid=(G, nb, nc),
        in_specs=in_specs,
        out_specs=pl.BlockSpec((1, bm, D), lambda gi, i, j: (gi, i, 0), pipeline_mode=pl.Buffered(1)),
        out_shape=jax.ShapeDtypeStruct((G, S, D), F32),
        scratch_shapes=[pltpu.VMEM((bm + 2 * HALO, D), BF16)],
        compiler_params=_cparams(("parallel", "parallel", "arbitrary")),
        name="conv_ffn",
    )(*args)


def _hy_in_kernel(xm_ref, xp_ref, xn_ref, g_ref, sh_ref, sc_ref, w_ref, cw_ref, o_ref, h_scr,
                  *, bm, seq_len):
    i = pl.program_id(1)
    j = pl.program_id(2)

    @pl.when(j == 0)
    def _():
        _fill_h_ext(h_scr, xm_ref, xp_ref, xn_ref, g_ref, sh_ref, sc_ref, bm)

    u = jnp.dot(h_scr[...], w_ref[...], preferred_element_type=F32)
    o_ref[0] = _dwconv3_rows(u, cw_ref[...], i * bm, bm, seq_len)


def _hyena_in(x, g, shift, scale, w_in, conv_w, seq_len, bm, bn):
    G, S, D = x.shape
    nb = S // bm
    N = w_in.shape[1]
    in_specs = _halo_specs(bm, D, S) + [
        pl.BlockSpec((1, D), lambda gi, i, j: (0, 0)),
        pl.BlockSpec((1, 1, D), lambda gi, i, j: (gi, 0, 0)),
        pl.BlockSpec((1, 1, D), lambda gi, i, j: (gi, 0, 0)),
        pl.BlockSpec((D, bn), lambda gi, i, j: (0, j)),
        pl.BlockSpec((3, bn), lambda gi, i, j: (0, j)),
    ]
    return pl.pallas_call(
        functools.partial(_hy_in_kernel, bm=bm, seq_len=seq_len),
        grid=(G, nb, N // bn),
        in_specs=in_specs,
        out_specs=pl.BlockSpec((1, bm, bn), lambda gi, i, j: (gi, i, j)),
        out_shape=jax.ShapeDtypeStruct((G, S, N), F32),
        scratch_shapes=[pltpu.VMEM((bm + 2 * HALO, D), BF16)],
        compiler_params=_cparams(("parallel", "parallel", "arbitrary")),
        name="hyena_in",
    )(x, x, x, g, shift, scale, w_in, conv_w)


class _FftPlan:
    def __init__(self, seq_len):
        self.L = seq_len
        self.n2 = FFT_N2
        self.n1 = max(1, 2 * seq_len // FFT_N2)
        self.n = self.n1 * self.n2
        self.classes = self.n1 // 2 + 1
        self.slab = min(self.n2, seq_len)
        self.data_slabs = seq_len // self.slab

    def stage1_coef(self, m, k1):
        th = 2.0 * math.pi * ((m * k1) % self.n1) / self.n1
        return _snap(math.cos(th)), _snap(-math.sin(th))

    def class_weight(self, k1):
        if self.n1 == 1:
            return 1.0
        return 1.0 if k1 in (0, self.n1 // 2) else 2.0


def _snap(v):
    for t in (0.0, 1.0, -1.0):
        if abs(v - t) < 1e-12:
            return t
    return v


@functools.lru_cache(maxsize=None)
def _dft_consts(n1, n2):
    n = n1 * n2
    k = np.arange(n2)
    ang = 2.0 * np.pi * np.outer(k, k) / n2
    fr, fi = np.cos(ang), -np.sin(ang)
    fwd = np.block([[fr, -fi], [fi, fr]]).astype(np.float32)
    inv = np.block([[fr, fi], [-fi, fr]]).astype(np.float32)
    classes = n1 // 2 + 1
    tw = 2.0 * np.pi * np.outer(np.arange(classes), np.arange(n2)) / n
    twc = np.repeat(np.cos(tw)[:, :, None], TW_LANES, axis=2).astype(np.float32)
    tws = np.repeat((-np.sin(tw))[:, :, None], TW_LANES, axis=2).astype(np.float32)
    return fwd, inv, twc, tws


def _const_spec(shape):
    nd = len(shape)
    return pl.BlockSpec(shape, lambda *_: (0,) * nd, pipeline_mode=pl.Buffered(1))


def _lane_tile(t, lanes):
    reps = lanes // t.shape[1]
    return t if reps == 1 else jnp.concatenate([t] * reps, axis=1)


def _axpy(acc, c, x):
    if c == 0.0:
        return acc
    t = x if c == 1.0 else (-x if c == -1.0 else c * x)
    return t if acc is None else acc + t


def _fft_forward_class(plan, k1, n_slabs, load_rows, twc_ref, tws_ref, abuf, slot):
    n2, R = plan.n2, FFT_ROWS
    lanes = abuf.shape[-1]
    for off in range(0, n2, R):
        ar = ai = None
        for m in range(n_slabs):
            c, s = plan.stage1_coef(m, k1)
            x = load_rows(m * n2 + off, R)
            ar = _axpy(ar, c, x)
            ai = _axpy(ai, s, x)
        if k1 == 0:
            pr, pi = ar, ai
        else:
            tc = _lane_tile(twc_ref[k1, off:off + R, :], lanes)
            ts = _lane_tile(tws_ref[k1, off:off + R, :], lanes)
            if ai is None:
                pr, pi = ar * tc, ar * ts
            else:
                pr, pi = ar * tc - ai * ts, ar * ts + ai * tc
        abuf[slot, off:off + R, :] = pr.astype(BF16)
        if pi is None:
            abuf[slot, n2 + off:n2 + off + R, :] = jnp.zeros((R, lanes), BF16)
        else:
            abuf[slot, n2 + off:n2 + off + R, :] = pi.astype(BF16)


def _fftconv_kernel(z_ref, gate_ref, bias_ref, h_ref, fwd_ref, inv_ref, twc_ref, tws_ref,
                    o_ref, abuf, ybuf, b_scr, acc_scr, *, plan):
    n2, L, R = plan.n2, plan.L, FFT_ROWS
    lanes = o_ref.shape[-1]
    inv_n = 1.0 / plan.n

    def dots(k1, slot):
        xf = jnp.dot(fwd_ref[...], abuf[slot], preferred_element_type=F32)
        xr, xi = xf[:n2], xf[n2:]
        hr, hi = h_ref[k1, 0:n2], h_ref[k1, n2:]
        ybuf[slot, 0:n2] = (xr * hr - xi * hi).astype(BF16)
        ybuf[slot, n2:] = (xr * hi + xi * hr).astype(BF16)
        b_scr[slot] = jnp.dot(inv_ref[...], ybuf[slot], preferred_element_type=F32)

    if plan.n1 == 1:
        hr, hi = h_ref[0, 0:n2], h_ref[0, n2:]
        for s in range(z_ref.shape[0]):
            z = z_ref[s]
            xf = jnp.dot(fwd_ref[:, 0:L], z.astype(BF16), preferred_element_type=F32)
            xr, xi = xf[:n2], xf[n2:]
            ybuf[s % 2, 0:n2] = (xr * hr - xi * hi).astype(BF16)
            ybuf[s % 2, n2:] = (xr * hi + xi * hr).astype(BF16)
            b = jnp.dot(inv_ref[0:L, :], ybuf[s % 2], preferred_element_type=F32)
            y = b * inv_n + bias_ref[...] * z
            o_ref[s] = (gate_ref[s] * y).astype(o_ref.dtype)
        return

    def forward(k1):
        _fft_forward_class(plan, k1, plan.data_slabs,
                           lambda start, rows: z_ref[0, start:start + rows, :],
                           twc_ref, tws_ref, abuf, k1 % 2)

    def inverse(k1, last):
        slot = k1 % 2
        wk = plan.class_weight(k1)
        for off in range(0, n2, R):
            br = b_scr[slot, off:off + R, :]
            bi = b_scr[slot, n2 + off:n2 + off + R, :]
            if k1 == 0:
                pr, pi = br, bi
            else:
                tc = _lane_tile(twc_ref[k1, off:off + R, :], lanes)
                ts = _lane_tile(tws_ref[k1, off:off + R, :], lanes)
                pr, pi = br * tc + bi * ts, bi * tc - br * ts
            for m in range(plan.data_slabs):
                c, s = plan.stage1_coef(m, k1)
                t = _axpy(_axpy(None, wk * c, pr), wk * s, pi)
                rows = slice(m * n2 + off, m * n2 + off + R)
                if last:
                    z = z_ref[0, rows, :]
                    y = (acc_scr[rows, :] + t) * inv_n + bias_ref[...] * z
                    o_ref[0, rows, :] = (gate_ref[0, rows, :] * y).astype(o_ref.dtype)
                elif k1 == 0:
                    acc_scr[rows, :] = t
                else:
                    acc_scr[rows, :] += t

    forward(0)
    for k1 in range(plan.classes):
        if k1 + 1 < plan.classes:
            forward(k1 + 1)
        dots(k1, k1 % 2)
        if k1 >= 1:
            inverse(k1 - 1, False)
    inverse(plan.classes - 1, True)


def _fftconv(plan, z, z_col, gate, gate_col, bias, hspec, n_seq, lanes, out_dtype):
    L, n2 = plan.L, plan.n2
    nc = D_MODEL // lanes
    fwd, inv, twc, tws = _dft_consts(plan.n1, n2)
    zc, gc = z_col // lanes, gate_col // lanes
    acc_rows = L if plan.n1 > 1 else 8
    sb = 1 if plan.n1 > 1 else min(n_seq, 4)
    return pl.pallas_call(
        functools.partial(_fftconv_kernel, plan=plan),
        grid=(nc, n_seq // sb),
        in_specs=[
            pl.BlockSpec((sb, L, lanes), lambda c, b: (b, 0, zc + c)),
            pl.BlockSpec((sb, L, lanes), lambda c, b: (b, 0, gc + c)),
            pl.BlockSpec((1, lanes), lambda c, b: (0, c)),
            pl.BlockSpec((plan.classes, 2 * n2, lanes), lambda c, b: (0, 0, c),
                         pipeline_mode=pl.Buffered(1)),
            _const_spec((2 * n2, 2 * n2)),
            _const_spec((2 * n2, 2 * n2)),
            _const_spec((plan.classes, n2, TW_LANES)),
            _const_spec((plan.classes, n2, TW_LANES)),
        ],
        out_specs=pl.BlockSpec((sb, L, lanes), lambda c, b: (b, 0, c)),
        out_shape=jax.ShapeDtypeStruct((n_seq, L, D_MODEL), out_dtype),
        scratch_shapes=[
            pltpu.VMEM((2, 2 * n2, lanes), BF16),
            pltpu.VMEM((2, 2 * n2, lanes), BF16),
            pltpu.VMEM((2, 2 * n2, lanes), F32),
            pltpu.VMEM((acc_rows, lanes), F32),
        ],
        compiler_params=_cparams(("parallel", "arbitrary")),
        name="hyena_fftconv",
    )(z, gate, bias, hspec, jnp.asarray(fwd).astype(BF16), jnp.asarray(inv).astype(BF16),
      jnp.asarray(twc), jnp.asarray(tws))


@functools.lru_cache(maxsize=None)
def _filter_tables(L):
    r = np.arange(2 * L)
    p = np.where(r < L, r, 2 * L - r)
    p[L] = 0
    t = p / (L - 1.0)
    w = 2.0 * np.pi * p / L
    f = np.linspace(1e-4, HY_BANDS - 1, HY_BANDS)
    feat = np.concatenate([t[:, None], np.cos(np.outer(w, f)), -np.sin(np.outer(w, f))], axis=1)
    feat = np.pad(feat, ((0, 0), (0, HY_EMB_PAD - HY_EMB))).astype(np.float32)
    t_tab = np.repeat(t[:, None], TW_LANES, axis=1).astype(np.float32)
    return feat, t_tab


def _filter_hidden_kernel(feat_ref, w1_ref, b1_ref, w2_ref, b2_ref, fq_ref, o_ref):
    hp = lax.Precision.HIGHEST
    h = jnp.sin(fq_ref[...] * (jnp.dot(feat_ref[...], w1_ref[...], preferred_element_type=F32,
                                        precision=hp) + b1_ref[...]))
    h = jnp.sin(fq_ref[...] * (jnp.dot(h, w2_ref[...], preferred_element_type=F32,
                                        precision=hp) + b2_ref[...]))
    o_ref[...] = h


def _filter_hidden(feat, w1, b1, w2, b2, freq):
    rows = feat.shape[0]
    br = 1024 if rows % 1024 == 0 else rows
    vec = pl.BlockSpec((1, HY_FH), lambda i: (0, 0))
    return pl.pallas_call(
        _filter_hidden_kernel,
        grid=(rows // br,),
        in_specs=[
            pl.BlockSpec((br, HY_EMB_PAD), lambda i: (i, 0)),
            pl.BlockSpec((HY_EMB_PAD, HY_FH), lambda i: (0, 0)),
            vec,
            pl.BlockSpec((HY_FH, HY_FH), lambda i: (0, 0)),
            vec, vec,
        ],
        out_specs=pl.BlockSpec((br, HY_FH), lambda i: (i, 0)),
        out_shape=jax.ShapeDtypeStruct((rows, HY_FH), F32),
        compiler_params=_cparams(("parallel",)),
        name="hyena_filter_hidden",
    )(feat, w1, b1, w2, b2, freq)


def _filter_spec_kernel(hid_ref, w3a_ref, w3b_ref, t_ref, delta_ref, fwd_ref, twc_ref, tws_ref,
                        o_ref, f_scr, abuf, *, plan):
    L, n2 = plan.L, plan.n2
    lanes = o_ref.shape[-1]
    hp = lax.Precision.HIGHEST
    absd = jnp.abs(delta_ref[...])
    rows_per = min(L, 512)

    def gen(r, carry):
        for base, w_ref in ((0, w3a_ref), (L, w3b_ref)):
            rows = pl.ds(pl.multiple_of(base + r * rows_per, rows_per), rows_per)
            f = jnp.dot(hid_ref[rows, :], w_ref[0], preferred_element_type=F32, precision=hp)
            f_scr[rows, :] = f * jnp.exp(-_lane_tile(t_ref[rows, :], lanes) * absd)
        return carry

    lax.fori_loop(0, L // rows_per, gen, 0)
    f_scr[L:L + 8] = jnp.where(lax.broadcasted_iota(jnp.int32, (8, 1), 0) > 0, f_scr[L:L + 8], 0.0)

    if plan.n1 == 1:
        abuf[0, 0:n2] = f_scr[...].astype(BF16)
        abuf[0, n2:] = jnp.zeros((n2, lanes), BF16)
    for k1 in range(plan.classes):
        slot = k1 % 2
        if plan.n1 > 1:
            _fft_forward_class(plan, k1, plan.n1,
                               lambda start, rows: f_scr[start:start + rows, :],
                               twc_ref, tws_ref, abuf, slot)
        o_ref[0, k1] = jnp.dot(fwd_ref[...], abuf[slot], preferred_element_type=F32)


def _filter_spectrum(plan, hidden, w_f3, lanes):
    L, n2 = plan.L, plan.n2
    nc = D_MODEL // lanes
    fwd, _, twc, tws = _dft_consts(plan.n1, n2)
    _, t_tab = _filter_tables(L)
    delta = np.linspace(HY_MIN_DECAY, HY_MAX_DECAY, D_MODEL).astype(np.float32)[None, :]
    w3 = w_f3.reshape(HY_FH, 2, 2, D_MODEL).transpose(1, 2, 0, 3)
    return pl.pallas_call(
        functools.partial(_filter_spec_kernel, plan=plan),
        grid=(2, nc),
        in_specs=[
            _const_spec((2 * L, HY_FH)),
            pl.BlockSpec((None, 1, HY_FH, lanes), lambda o, c: (0, o, 0, c)),
            pl.BlockSpec((None, 1, HY_FH, lanes), lambda o, c: (1, o, 0, c)),
            _const_spec((2 * L, TW_LANES)),
            pl.BlockSpec((1, lanes), lambda o, c: (0, c)),
            _const_spec((2 * n2, 2 * n2)),
            _const_spec((plan.classes, n2, TW_LANES)),
            _const_spec((plan.classes, n2, TW_LANES)),
        ],
        out_specs=pl.BlockSpec((1, plan.classes, 2 * n2, lanes), lambda o, c: (o, 0, 0, c)),
        out_shape=jax.ShapeDtypeStruct((2, plan.classes, 2 * n2, D_MODEL), F32),
        scratch_shapes=[pltpu.VMEM((2 * L, lanes), F32), pltpu.VMEM((2, 2 * n2, lanes), BF16)],
        compiler_params=_cparams(("parallel", "parallel")),
        name="hyena_filter_spectrum",
    )(hidden, w3, w3, jnp.asarray(t_tab), jnp.asarray(delta), jnp.asarray(fwd).astype(BF16),
      jnp.asarray(twc), jnp.asarray(tws))


@functools.lru_cache(maxsize=None)
def _rope_tables(L):
    t = np.arange(L)
    inv = ROPE_THETA ** (-np.arange(ROPE_PAIRS, dtype=np.float32) / ROPE_PAIRS)
    ang_row = (t // GRID_W).astype(np.float32)[:, None] * inv
    ang_col = (t % GRID_W).astype(np.float32)[:, None] * inv
    zero = np.zeros_like(ang_row)
    cos = np.concatenate([np.cos(ang_row)] * 2 + [np.cos(ang_col)] * 2, axis=1)
    sa = np.concatenate([-np.sin(ang_row), zero, -np.sin(ang_col), zero], axis=1)
    sb = np.concatenate([zero, np.sin(ang_row), zero, np.sin(ang_col)], axis=1)
    return cos.astype(np.float32), sa.astype(np.float32), sb.astype(np.float32)


def _hyena_mixer(x, g, shift, scale, gate, w_in, conv_w, hspec, bias, w_out, seq_len, lanes, bm):
    G, S, D = x.shape
    plan = _FftPlan(seq_len)
    n_seq = G * S // seq_len
    u = _hyena_in(x, g, shift, scale, w_in, conv_w, seq_len, bm, 1024)
    u = u.reshape(n_seq, seq_len, 3 * D)
    z1 = _fftconv(plan, u, 2 * D, u, 0, bias[0:1], hspec[0], n_seq, lanes, F32)
    z2 = _fftconv(plan, z1, 0, u, D, bias[1:2], hspec[1], n_seq, lanes, BF16)
    return _proj_residual(z2.reshape(G * S, D), w_out, x, gate, bm)


def kernel(x_prompt, x_sample, cache_k, cache_v, c, c_ctx, w_mod, b_mod, norm_mix, norm_ffn, norm_final, w_qkv, w_o, attn_sink, hy_w_in, hy_conv, hy_w_f1, hy_b_f1, hy_w_f2, hy_b_f2, hy_w_f3, hy_freq, hy_bias, hy_w_out, ffn_w_up, ffn_conv, ffn_w_down):
    B, SEQ, D = x_prompt.shape
    DB, DSEQ, _ = x_sample.shape
    depth = w_mod.shape[0]

    n_cond = DB + 1
    cvec = jnp.concatenate([c, c_ctx[None, :], jnp.zeros((16 - n_cond, D), F32)], axis=0)
    mod = _modulation(cvec, w_mod, b_mod).reshape(depth, 16, 6, 1, D)

    xs = x_sample
    xp = x_prompt.reshape(1, B * SEQ, D)
    streams = [
        dict(x=xs, rows=slice(0, DB), seq=DSEQ, n_seq=DB, lanes=256),
        dict(x=xp, rows=slice(DB, DB + 1), seq=SEQ, n_seq=B, lanes=256),
    ]
    bm = 1024
    rope_tabs = tuple(jnp.asarray(t) for t in _rope_tables(DSEQ))
    new_k, new_v = [], []

    for i in range(depth):
        g_mix = norm_mix[i][None, :]
        g_ffn = norm_ffn[i][None, :]
        if i % 2 == 0:
            a = i // 2
            wq = w_qkv[a].astype(BF16)
            wo = w_o[a].astype(BF16)
        else:
            hl = i // 2
            w_in = hy_w_in[hl].astype(BF16)
            w_out = hy_w_out[hl].astype(BF16)
            w1 = jnp.pad(hy_w_f1[hl], ((0, HY_EMB_PAD - HY_EMB), (0, 0)))
        w_up = ffn_w_up[i].astype(BF16)
        w_down = ffn_w_down[i].astype(BF16)
        g_final = norm_final[None, :] if i == depth - 1 else None

        for si, st in enumerate(streams):
            x = st["x"]
            m = mod[i, st["rows"]]
            sh1, sc1, g1, sh2, sc2, g2 = (m[:, t] for t in range(6))
            latent = si == 0
            if i % 2 == 0:
                res = _qkv_proj(x, g_mix, sh1, sc1, wq, rope_tabs if latent else None,
                                emit_kv=not latent, bm=bm)
                qkv = res[0]
                if latent:
                    ck = cache_k[:, a].reshape(DB, -1, KV_DIM).astype(BF16)
                    cv = cache_v[:, a].reshape(DB, -1, KV_DIM).astype(BF16)
                    att = _attention(qkv, attn_sink[a], ck, cv, DB, DSEQ, window=True)
                else:
                    kv = res[1]
                    new_k.append(kv[:, :KV_DIM].reshape(B, 1, SEQ, N_KV_HEADS, HEAD_DIM))
                    new_v.append(kv[:, KV_DIM:].reshape(B, 1, SEQ, N_KV_HEADS, HEAD_DIM))
                    ck = qkv[:, Q_DIM:Q_DIM + KV_DIM].reshape(B, SEQ, KV_DIM)
                    cv = qkv[:, Q_DIM + KV_DIM:].reshape(B, SEQ, KV_DIM)
                    att = _attention(qkv, attn_sink[a], ck, cv, B, SEQ, window=False)
                x = _proj_residual(att, wo, x, g1, bm)
            else:
                plan = _FftPlan(st["seq"])
                feat, _ = _filter_tables(st["seq"])
                hidden = _filter_hidden(jnp.asarray(feat), w1, hy_b_f1[hl][None, :], hy_w_f2[hl],
                                        hy_b_f2[hl][None, :], hy_freq[hl][None, :])
                hspec = _filter_spectrum(plan, hidden, hy_w_f3[hl], st["lanes"])
                x = _hyena_mixer(x, g_mix, sh1, sc1, g1, w_in, hy_conv[hl], hspec, hy_bias[hl],
                                 w_out, st["seq"], st["lanes"], bm)
            x = _conv_ffn(x, g_ffn, sh2, sc2, g2, w_up, ffn_conv[i], w_down, st["seq"], bm, 512,
                          g_final=g_final)
            st["x"] = x

    y_sample = streams[0]["x"]
    y_prompt = streams[1]["x"].reshape(B, SEQ, D)
    new_cache_k = new_k[0] if len(new_k) == 1 else jnp.concatenate(new_k, axis=1)
    new_cache_v = new_v[0] if len(new_v) == 1 else jnp.concatenate(new_v, axis=1)
    return (y_prompt, y_sample, new_cache_k, new_cache_v)
```

```python
import functools
import math

import jax
import jax.numpy as jnp
import numpy as np
from jax import lax
from jax.experimental import pallas as pl
from jax.experimental.pallas import tpu as pltpu

F32 = jnp.float32
BF16 = jnp.bfloat16

D_MODEL = 2048
HEAD_DIM = 128
N_HEADS = 16
N_KV_HEADS = 4
GROUP = N_HEADS // N_KV_HEADS
Q_DIM = N_HEADS * HEAD_DIM
KV_DIM = N_KV_HEADS * HEAD_DIM
QKV_DIM = Q_DIM + 2 * KV_DIM
GRID_W = 64
WINDOW = 128
ROPE_THETA = 10000.0
ROPE_PAIRS = HEAD_DIM // 4
HY_BANDS = 16
HY_EMB = 1 + 2 * HY_BANDS
HY_EMB_PAD = 40
HY_FH = 64
HY_MIN_DECAY = math.log(1e-2) / 1.5
HY_MAX_DECAY = math.log(1e-2) / 0.3
D_FF = 5632
EPS = 1e-6
NEG_INF = -1e30

VMEM_LIMIT = 56 * 1024 * 1024
NORM_ROWS = 16
HALO = 16
FFT_N2 = 512
FFT_ROWS = 32
FFT_FWD_ROWS = 16
QKV_BN = 1024
ATTN_ROWS = 32
FFN_DOWN_COLS = 1024
TW_LANES = 128


def _cparams(sem):
    return pltpu.CompilerParams(dimension_semantics=sem, vmem_limit_bytes=VMEM_LIMIT)


def _store_modulated_norm(dst_ref, dst0, x_ref, n_rows, g_ref, sh_ref, sc_ref):
    gain = g_ref[...] * (1.0 + sc_ref[0])
    shift = sh_ref[0]
    for r in range(0, n_rows, NORM_ROWS):
        x = x_ref[0, r:r + NORM_ROWS, :]
        ms = jnp.mean(x * x, axis=-1, keepdims=True)
        y = x * lax.rsqrt(ms + EPS) * gain + shift
        dst_ref[dst0 + r:dst0 + r + NORM_ROWS, :] = y.astype(BF16)


def _mod_kernel(c_ref, w_ref, b_ref, o_ref):
    c = c_ref[...]
    s = c * (1.0 / (1.0 + jnp.exp(-c)))
    o_ref[0] = jnp.dot(s, w_ref[0], preferred_element_type=F32,
                       precision=lax.Precision.HIGHEST) + b_ref[0]


def _modulation(cvec, w_mod, b_mod):
    depth, d, n = w_mod.shape
    rows = cvec.shape[0]
    bn = 1024
    return pl.pallas_call(
        _mod_kernel,
        grid=(depth, n // bn),
        in_specs=[
            pl.BlockSpec((rows, d), lambda l, j: (0, 0)),
            pl.BlockSpec((1, d, bn), lambda l, j: (l, 0, j)),
            pl.BlockSpec((1, 1, bn), lambda l, j: (l, 0, j)),
        ],
        out_specs=pl.BlockSpec((1, rows, bn), lambda l, j: (l, 0, j)),
        out_shape=jax.ShapeDtypeStruct((depth, rows, n), F32),
        compiler_params=_cparams(("parallel", "parallel")),
        name="modulation",
    )(cvec, w_mod, b_mod.reshape(depth, 1, n))


def _qkv_kernel(*refs, rope, emit_kv):
    x_ref, g_ref, sh_ref, sc_ref, w_ref = refs[:5]
    pos = 5
    if rope:
        tabs = (refs[pos:pos + 3], refs[pos + 3:pos + 6])
        pos += 6
    o_ref = refs[pos]
    pos += 1
    if emit_kv:
        kv_ref = refs[pos]
        pos += 1
    h_scr = refs[pos]

    j = pl.program_id(2)
    n_q = Q_DIM // QKV_BN
    heads = QKV_BN // HEAD_DIM

    @pl.when(j == 0)
    def _():
        _store_modulated_norm(h_scr, 0, x_ref, h_scr.shape[0], g_ref, sh_ref, sc_ref)

    acc = jnp.dot(h_scr[...], w_ref[...], preferred_element_type=F32)

    if emit_kv:
        kv_ref[...] = acc
    q_scale = jnp.where(j < n_q, HEAD_DIM ** -0.5, 1.0)
    for h in range(heads):
        sl = slice(h * HEAD_DIM, (h + 1) * HEAD_DIM)
        xh = acc[:, sl]
        if rope:
            cos_ref, sa_ref, sb_ref = tabs[h // N_KV_HEADS]
            xh = (xh * cos_ref[...] + pltpu.roll(xh, HEAD_DIM - 32, axis=1) * sa_ref[...]
                  + pltpu.roll(xh, 32, axis=1) * sb_ref[...])
        else:
            xh = xh * q_scale
        o_ref[:, sl] = xh.astype(BF16)


def _qkv_proj(x, g, shift, scale, w, rope_tabs, emit_kv, bm):
    G, S, D = x.shape
    nb = S // bm
    bn = QKV_BN
    assert 2 * KV_DIM == bn and Q_DIM % bn == 0 and bn // HEAD_DIM == 2 * N_KV_HEADS
    nj = QKV_DIM // bn
    n_q = Q_DIM // bn
    rope = rope_tabs is not None
    in_specs = [
        pl.BlockSpec((1, bm, D), lambda gi, i, j: (gi, i, 0)),
        pl.BlockSpec((1, D), lambda gi, i, j: (0, 0)),
        pl.BlockSpec((1, 1, D), lambda gi, i, j: (gi, 0, 0)),
        pl.BlockSpec((1, 1, D), lambda gi, i, j: (gi, 0, 0)),
        pl.BlockSpec((D, bn), lambda gi, i, j: (0, j)),
    ]
    args = [x, g, shift, scale, w]
    if rope:
        for other in (1, 2):
            for t in rope_tabs:
                in_specs.append(pl.BlockSpec(
                    (None, bm, HEAD_DIM),
                    lambda gi, i, j, other=other: (jnp.where(j < n_q, 0, other), i, 0)))
                args.append(t)
    out_specs = [pl.BlockSpec((bm, bn), lambda gi, i, j: (gi * nb + i, j))]
    out_shape = [jax.ShapeDtypeStruct((G * S, QKV_DIM), BF16)]
    if emit_kv:
        out_specs.append(pl.BlockSpec((bm, bn), lambda gi, i, j: (gi * nb + i, 0)))
        out_shape.append(jax.ShapeDtypeStruct((G * S, 2 * KV_DIM), F32))
    res = pl.pallas_call(
        functools.partial(_qkv_kernel, rope=rope, emit_kv=emit_kv),
        grid=(G, nb, nj),
        in_specs=in_specs,
        out_specs=out_specs,
        out_shape=out_shape,
        scratch_shapes=[pltpu.VMEM((bm, D), BF16)],
        compiler_params=_cparams(("parallel", "parallel", "arbitrary")),
        name="qkv_proj",
    )(*args)
    return res


def _attn_kernel(*refs, window, bq):
    sink_ref, q_ref = refs[:2]
    pos = 2
    if window:
        kp_ref, kc_ref, kn_ref, vp_ref, vc_ref, vn_ref, bias_ref = refs[pos:pos + 7]
        pos += 7
    ck_ref, cv_ref, o_ref, k_scr, v_scr, s_scr, p_scr = refs[pos:pos + 7]

    n_ctx = ck_ref.shape[1]
    n_win = 3 * bq if window else 0
    tr = ATTN_ROWS

    for h in range(N_KV_HEADS):
        slot = h % 2
        hs = slice(h * HEAD_DIM, (h + 1) * HEAD_DIM)
        if window:
            for t, (k_ref, v_ref) in enumerate(((kp_ref, vp_ref), (kc_ref, vc_ref), (kn_ref, vn_ref))):
                k_scr[slot, t * bq:(t + 1) * bq] = k_ref[:, hs]
                v_scr[slot, t * bq:(t + 1) * bq] = v_ref[:, hs]
        k_scr[slot, n_win:] = ck_ref[0, :, hs]
        v_scr[slot, n_win:] = cv_ref[0, :, hs]
        qs = jnp.concatenate(
            [q_ref[:, (h * GROUP + g) * HEAD_DIM:(h * GROUP + g + 1) * HEAD_DIM] for g in range(GROUP)],
            axis=0)
        s_scr[slot] = lax.dot_general(qs, k_scr[slot], (((1,), (1,)), ((), ())),
                                      preferred_element_type=F32)
        for r0 in range(0, GROUP * bq, tr):
            sk = sink_ref[h * GROUP + r0 // bq]
            s = s_scr[slot, r0:r0 + tr, :]
            if window:
                q0 = r0 % bq
                s = s + bias_ref[0, q0:q0 + tr, :]
            m = jnp.maximum(jnp.max(s, axis=-1, keepdims=True), sk)
            p = jnp.exp(s - m)
            denom = jnp.sum(p, axis=-1, keepdims=True) + jnp.exp(sk - m)
            p_scr[slot, r0:r0 + tr, :] = (p * (1.0 / denom)).astype(BF16)
        o = jnp.dot(p_scr[slot], v_scr[slot], preferred_element_type=F32)
        for g in range(GROUP):
            c0 = (h * GROUP + g) * HEAD_DIM
            o_ref[:, c0:c0 + HEAD_DIM] = o[g * bq:(g + 1) * bq].astype(BF16)


def _window_bias(n_ctx):
    i = np.arange(WINDOW)[:, None]
    j = np.arange(WINDOW)[None, :]
    prev = np.where(j >= i, 0.0, NEG_INF)
    cur = np.zeros((WINDOW, WINDOW))
    nxt = np.where(j <= i, 0.0, NEG_INF)
    dead = np.full((WINDOW, WINDOW), NEG_INF)
    ctx = np.zeros((WINDOW, n_ctx))
    first = np.concatenate([dead, cur, nxt, ctx], axis=1)
    mid = np.concatenate([prev, cur, nxt, ctx], axis=1)
    last = np.concatenate([prev, cur, dead, ctx], axis=1)
    return np.stack([first, mid, last]).astype(np.float32)


def _attention(qkv, sink, ctx_k, ctx_v, n_batch, seq, window):
    kcol = Q_DIM // KV_DIM
    vcol = kcol + 1
    n_ctx = ctx_k.shape[1]
    if window:
        bq = WINDOW
        nq = seq // bq
        in_specs = [
            pl.BlockSpec(memory_space=pltpu.SMEM),
            pl.BlockSpec((bq, Q_DIM), lambda b, n: (b * nq + n, 0)),
            pl.BlockSpec((bq, KV_DIM), lambda b, n: (b * nq + jnp.maximum(n - 1, 0), kcol)),
            pl.BlockSpec((bq, KV_DIM), lambda b, n: (b * nq + n, kcol)),
            pl.BlockSpec((bq, KV_DIM), lambda b, n: (b * nq + jnp.minimum(n + 1, nq - 1), kcol)),
            pl.BlockSpec((bq, KV_DIM), lambda b, n: (b * nq + jnp.maximum(n - 1, 0), vcol)),
            pl.BlockSpec((bq, KV_DIM), lambda b, n: (b * nq + n, vcol)),
            pl.BlockSpec((bq, KV_DIM), lambda b, n: (b * nq + jnp.minimum(n + 1, nq - 1), vcol)),
            pl.BlockSpec((1, bq, 3 * bq + n_ctx),
                         lambda b, n: (jnp.where(n == 0, 0, jnp.where(n == nq - 1, 2, 1)), 0, 0)),
            pl.BlockSpec((1, n_ctx, KV_DIM), lambda b, n: (b, 0, 0)),
            pl.BlockSpec((1, n_ctx, KV_DIM), lambda b, n: (b, 0, 0)),
        ]
        args = [sink, qkv, qkv, qkv, qkv, qkv, qkv, qkv, jnp.asarray(_window_bias(n_ctx)), ctx_k, ctx_v]
    else:
        bq = seq
        nq = 1
        in_specs = [
            pl.BlockSpec(memory_space=pltpu.SMEM),
            pl.BlockSpec((bq, Q_DIM), lambda b, n: (b, 0)),
            pl.BlockSpec((1, n_ctx, KV_DIM), lambda b, n: (b, 0, 0)),
            pl.BlockSpec((1, n_ctx, KV_DIM), lambda b, n: (b, 0, 0)),
        ]
        args = [sink, qkv, ctx_k, ctx_v]
    n_keys = (3 * bq if window else 0) + n_ctx
    return pl.pallas_call(
        functools.partial(_attn_kernel, window=window, bq=bq),
        grid=(n_batch, nq),
        in_specs=in_specs,
        out_specs=pl.BlockSpec((bq, Q_DIM), lambda b, n: (b * nq + n, 0)),
        out_shape=jax.ShapeDtypeStruct((n_batch * seq, Q_DIM), BF16),
        scratch_shapes=[
            pltpu.VMEM((2, n_keys, HEAD_DIM), BF16),
            pltpu.VMEM((2, n_keys, HEAD_DIM), BF16),
            pltpu.VMEM((2, GROUP * bq, n_keys), F32),
            pltpu.VMEM((2, GROUP * bq, n_keys), BF16),
        ],
        compiler_params=_cparams(("parallel", "parallel")),
        name="attention",
    )(*args)


def _proj_res_kernel(a_ref, w_ref, x_ref, gate_ref, o_ref):
    acc = jnp.dot(a_ref[...], w_ref[...], preferred_element_type=F32)
    o_ref[0] = x_ref[0] + gate_ref[0] * acc


def _proj_residual(a, w, x, gate, bm):
    G, S, D = x.shape
    K = a.shape[1]
    nb = S // bm
    bn = 1024
    return pl.pallas_call(
        _proj_res_kernel,
        grid=(G, nb, D // bn),
        in_specs=[
            pl.BlockSpec((bm, K), lambda gi, i, j: (gi * nb + i, 0)),
            pl.BlockSpec((K, bn), lambda gi, i, j: (0, j)),
            pl.BlockSpec((1, bm, bn), lambda gi, i, j: (gi, i, j)),
            pl.BlockSpec((1, 1, bn), lambda gi, i, j: (gi, 0, j)),
        ],
        out_specs=pl.BlockSpec((1, bm, bn), lambda gi, i, j: (gi, i, j)),
        out_shape=jax.ShapeDtypeStruct((G, S, D), F32),
        compiler_params=_cparams(("parallel", "parallel", "parallel")),
        name="proj_residual",
    )(a, w, x, gate)


def _fill_h_ext(h_scr, xm_ref, xp_ref, xn_ref, g_ref, sh_ref, sc_ref, bm):
    _store_modulated_norm(h_scr, 0, xp_ref, HALO, g_ref, sh_ref, sc_ref)
    _store_modulated_norm(h_scr, HALO, xm_ref, bm, g_ref, sh_ref, sc_ref)
    _store_modulated_norm(h_scr, HALO + bm, xn_ref, HALO, g_ref, sh_ref, sc_ref)


def _dwconv3_rows(u_ext, cw, row0, bm, seq_len):
    assert seq_len & (seq_len - 1) == 0
    pos = (row0 + lax.broadcasted_iota(jnp.int32, (bm, 1), 0)) & (seq_len - 1)
    prev = jnp.where(pos == 0, 0.0, u_ext[HALO - 1:HALO - 1 + bm])
    nxt = jnp.where(pos == seq_len - 1, 0.0, u_ext[HALO + 1:HALO + 1 + bm])
    return prev * cw[0:1] + u_ext[HALO:HALO + bm] * cw[1:2] + nxt * cw[2:3]


def _halo_specs(bm, D, S):
    nh = bm // HALO
    last = S // HALO - 1
    return [
        pl.BlockSpec((1, bm, D), lambda gi, i, j: (gi, i, 0)),
        pl.BlockSpec((1, HALO, D), lambda gi, i, j: (gi, jnp.maximum(i * nh - 1, 0), 0)),
        pl.BlockSpec((1, HALO, D), lambda gi, i, j: (gi, jnp.minimum((i + 1) * nh, last), 0)),
    ]


def _ffn_kernel(*refs, bm, seq_len, final_norm):
    (xm_ref, xp_ref, xn_ref, g_ref, sh_ref, sc_ref, gate_ref,
     wa_ref, wb_ref, cw_ref, wd_ref) = refs[:11]
    pos = 11
    if final_norm:
        gf_ref = refs[pos]
        pos += 1
    o_ref, h_scr = refs[pos:pos + 2]

    i = pl.program_id(1)
    j = pl.program_id(2)
    D = o_ref.shape[-1]

    @pl.when(j == 0)
    def _():
        _fill_h_ext(h_scr, xm_ref, xp_ref, xn_ref, g_ref, sh_ref, sc_ref, bm)
        o_ref[0] = jnp.zeros((bm, D), F32)

    ua = jnp.dot(h_scr[...], wa_ref[...], preferred_element_type=F32)
    ub = jnp.dot(h_scr[HALO:HALO + bm], wb_ref[...], preferred_element_type=F32)
    a = _dwconv3_rows(ua, cw_ref[...], i * bm, bm, seq_len)
    act = (a * (1.0 / (1.0 + jnp.exp(-a))) * ub).astype(BF16)
    for c0 in range(0, D, FFN_DOWN_COLS):
        cols = slice(c0, c0 + FFN_DOWN_COLS)
        o_ref[0, :, cols] += jnp.dot(act, wd_ref[:, cols], preferred_element_type=F32)

    @pl.when(j == pl.num_programs(2) - 1)
    def _():
        y = xm_ref[0] + gate_ref[0] * o_ref[0]
        if final_norm:
            ms = jnp.mean(y * y, axis=-1, keepdims=True)
            y = y * lax.rsqrt(ms + EPS) * gf_ref[...]
        o_ref[0] = y


def _conv_ffn(x, g, shift, scale, gate, w_up, conv_w, w_down, seq_len, bm, bf, g_final=None):
    G, S, D = x.shape
    nb = S // bm
    nc = D_FF // bf
    final_norm = g_final is not None
    in_specs = _halo_specs(bm, D, S) + [
        pl.BlockSpec((1, D), lambda gi, i, j: (0, 0)),
        pl.BlockSpec((1, 1, D), lambda gi, i, j: (gi, 0, 0)),
        pl.BlockSpec((1, 1, D), lambda gi, i, j: (gi, 0, 0)),
        pl.BlockSpec((1, 1, D), lambda gi, i, j: (gi, 0, 0)),
        pl.BlockSpec((D, bf), lambda gi, i, j: (0, j)),
        pl.BlockSpec((D, bf), lambda gi, i, j: (0, nc + j)),
        pl.BlockSpec((3, bf), lambda gi, i, j: (0, j)),
        pl.BlockSpec((bf, D), lambda gi, i, j: (j, 0)),
    ]
    args = [x, x, x, g, shift, scale, gate, w_up, w_up, conv_w, w_down]
    if final_norm:
        in_specs.append(pl.BlockSpec((1, D), lambda gi, i, j: (0, 0)))
        args.append(g_final)
    return pl.pallas_call(
        functools.partial(_ffn_kernel, bm=bm, seq_len=seq_len, final_norm=final_norm),
        grid=(G, nb, nc),
        in_specs=in_specs,
        out_specs=pl.BlockSpec((1, bm, D), lambda gi, i, j: (gi, i, 0), pipeline_mode=pl.Buffered(1)),
        out_shape=jax.ShapeDtypeStruct((G, S, D), F32),
        scratch_shapes=[pltpu.VMEM((bm + 2 * HALO, D), BF16)],
        compiler_params=_cparams(("parallel", "parallel", "arbitrary")),
        name="conv_ffn",
    )(*args)


def _hy_in_kernel(xm_ref, xp_ref, xn_ref, g_ref, sh_ref, sc_ref, w_ref, cw_ref, o_ref, h_scr,
                  *, bm, seq_len):
    i = pl.program_id(1)
    j = pl.program_id(2)

    @pl.when(j == 0)
    def _():
        _fill_h_ext(h_scr, xm_ref, xp_ref, xn_ref, g_ref, sh_ref, sc_ref, bm)

    u = jnp.dot(h_scr[...], w_ref[...], preferred_element_type=F32)
    o_ref[0] = _dwconv3_rows(u, cw_ref[...], i * bm, bm, seq_len)


def _hyena_in(x, g, shift, scale, w_in, conv_w, seq_len, bm, bn):
    G, S, D = x.shape
    nb = S // bm
    N = w_in.shape[1]
    in_specs = _halo_specs(bm, D, S) + [
        pl.BlockSpec((1, D), lambda gi, i, j: (0, 0)),
        pl.BlockSpec((1, 1, D), lambda gi, i, j: (gi, 0, 0)),
        pl.BlockSpec((1, 1, D), lambda gi, i, j: (gi, 0, 0)),
        pl.BlockSpec((D, bn), lambda gi, i, j: (0, j)),
        pl.BlockSpec((3, bn), lambda gi, i, j: (0, j)),
    ]
    return pl.pallas_call(
        functools.partial(_hy_in_kernel, bm=bm, seq_len=seq_len),
        grid=(G, nb, N // bn),
        in_specs=in_specs,
        out_specs=pl.BlockSpec((1, bm, bn), lambda gi, i, j: (gi, i, j)),
        out_shape=jax.ShapeDtypeStruct((G, S, N), F32),
        scratch_shapes=[pltpu.VMEM((bm + 2 * HALO, D), BF16)],
        compiler_params=_cparams(("parallel", "parallel", "arbitrary")),
        name="hyena_in",
    )(x, x, x, g, shift, scale, w_in, conv_w)


class _FftPlan:
    def __init__(self, seq_len):
        self.L = seq_len
        self.n2 = FFT_N2
        self.n1 = max(1, 2 * seq_len // FFT_N2)
        self.n = self.n1 * self.n2
        self.classes = self.n1 // 2 + 1
        self.slab = min(self.n2, seq_len)
        self.data_slabs = seq_len // self.slab

    def stage1_coef(self, m, k1):
        th = 2.0 * math.pi * ((m * k1) % self.n1) / self.n1
        return _snap(math.cos(th)), _snap(-math.sin(th))

    def class_groups(self):
        half = self.n1 // 2
        groups = [(k, half - k) for k in range((half + 1) // 2)]
        if half % 2 == 0:
            groups.append((half // 2,))
        return groups

    def class_weight(self, k1):
        if self.n1 == 1:
            return 1.0
        return 1.0 if k1 in (0, self.n1 // 2) else 2.0


def _snap(v):
    for t in (0.0, 1.0, -1.0):
        if abs(v - t) < 1e-12:
            return t
    return v


@functools.lru_cache(maxsize=None)
def _dft_consts(n1, n2):
    n = n1 * n2
    k = np.arange(n2)
    ang = 2.0 * np.pi * np.outer(k, k) / n2
    fr, fi = np.cos(ang), -np.sin(ang)
    fwd = np.block([[fr, -fi], [fi, fr]]).astype(np.float32)
    inv = np.block([[fr, fi], [-fi, fr]]).astype(np.float32)
    classes = n1 // 2 + 1
    tw = 2.0 * np.pi * np.outer(np.arange(classes), np.arange(n2)) / n
    twc = np.repeat(np.cos(tw)[:, :, None], TW_LANES, axis=2).astype(np.float32)
    tws = np.repeat((-np.sin(tw))[:, :, None], TW_LANES, axis=2).astype(np.float32)
    return fwd, inv, twc, tws


def _const_spec(shape):
    nd = len(shape)
    return pl.BlockSpec(shape, lambda *_: (0,) * nd, pipeline_mode=pl.Buffered(1))


def _lane_tile(t, lanes):
    reps = lanes // t.shape[1]
    return t if reps == 1 else jnp.concatenate([t] * reps, axis=1)


def _scale(c, x):
    return x if c == 1.0 else (-x if c == -1.0 else c * x)


def _lin2(c1, x1, c2, x2):
    if x1 is None or c1 == 0.0:
        x1 = None
    if x2 is None or c2 == 0.0:
        x2 = None
    if x1 is None and x2 is None:
        return None
    if x2 is None:
        return _scale(c1, x1)
    if x1 is None:
        return _scale(c2, x2)
    if abs(abs(c1) - abs(c2)) < 1e-12:
        s = (x1 + x2) if (c1 > 0) == (c2 > 0) else (x1 - x2)
        return _scale(c1, s)
    return c1 * x1 + c2 * x2


def _rdft_half(xs):
    n = len(xs)
    if n == 1:
        return [(xs[0], None)]
    if n == 2:
        return [(_lin2(1.0, xs[0], 1.0, xs[1]), None), (_lin2(1.0, xs[0], -1.0, xs[1]), None)]
    half, quarter = n // 2, n // 4
    ev, od = _rdft_half(xs[0::2]), _rdft_half(xs[1::2])
    out = [None] * (half + 1)
    for j in range(quarter + 1):
        th = 2.0 * math.pi * j / n
        wr, wi = _snap(math.cos(th)), _snap(-math.sin(th))
        (er, ei), (orr, oi) = ev[j], od[j]
        tr = _lin2(wr, orr, -wi, oi)
        ti = _lin2(wi, orr, wr, oi)
        out[j] = (_lin2(1.0, er, 1.0, tr), _lin2(1.0, ei, 1.0, ti))
        if half - j != j:
            out[half - j] = (_lin2(1.0, er, -1.0, tr), _lin2(1.0, ti, -1.0, ei))
    return out


def _fft_forward_all(plan, n_slabs, src_ref, src_lead, twc_ref, tws_ref, abuf):
    n2, R = plan.n2, FFT_FWD_ROWS
    lanes = abuf.shape[-1]

    def body(r, carry):
        off = pl.multiple_of(r * R, R)
        for l0 in range(0, lanes, TW_LANES):
            ls = slice(l0, l0 + TW_LANES)
            xs = [src_ref[src_lead + (pl.ds(m * n2 + off, R), ls)] if m < n_slabs else None
                  for m in range(plan.n1)]
            for k1, (ar, ai) in enumerate(_rdft_half(xs)):
                if k1 == 0:
                    abuf[0, pl.ds(off, R), ls] = ar.astype(BF16)
                    continue
                tc = twc_ref[k1, pl.ds(off, R), :]
                ts = tws_ref[k1, pl.ds(off, R), :]
                abuf[k1, pl.ds(off, R), ls] = _lin2(1.0, ar * tc, -1.0, None if ai is None else ai * ts).astype(BF16)
                abuf[k1, pl.ds(n2 + off, R), ls] = _lin2(1.0, ar * ts, 1.0, None if ai is None else ai * tc).astype(BF16)
        return carry

    lax.fori_loop(0, n2 // R, body, 0)


def _fftconv_kernel(z_ref, gate_ref, bias_ref, h_ref, fwd_ref, inv_ref, twc_ref, tws_ref,
                    o_ref, abuf, ybuf, b_scr, acc_scr, *, plan):
    n2, L, R = plan.n2, plan.L, FFT_ROWS
    lanes = o_ref.shape[-1]
    inv_n = 1.0 / plan.n

    def dots(k1, slot):
        if k1 == 0:
            xf = jnp.dot(fwd_ref[:, 0:n2], abuf[0, 0:n2], preferred_element_type=F32)
        else:
            xf = jnp.dot(fwd_ref[...], abuf[k1], preferred_element_type=F32)
        xr, xi = xf[:n2], xf[n2:]
        hr, hi = h_ref[k1, 0:n2], h_ref[k1, n2:]
        ys = slot % ybuf.shape[0]
        ybuf[ys, 0:n2] = (xr * hr - xi * hi).astype(BF16)
        ybuf[ys, n2:] = (xr * hi + xi * hr).astype(BF16)
        if k1 == 0:
            b_scr[slot, 0:n2] = jnp.dot(inv_ref[0:n2, :], ybuf[ys], preferred_element_type=F32)
        else:
            b_scr[slot] = jnp.dot(inv_ref[...], ybuf[ys], preferred_element_type=F32)

    if plan.n1 == 1:
        hr, hi = h_ref[0, 0:n2], h_ref[0, n2:]
        for s in range(z_ref.shape[0]):
            z = z_ref[s]
            xf = jnp.dot(fwd_ref[:, 0:L], z.astype(BF16), preferred_element_type=F32)
            xr, xi = xf[:n2], xf[n2:]
            ybuf[s % 2, 0:n2] = (xr * hr - xi * hi).astype(BF16)
            ybuf[s % 2, n2:] = (xr * hi + xi * hr).astype(BF16)
            b = jnp.dot(inv_ref[0:L, :], ybuf[s % 2], preferred_element_type=F32)
            y = b * inv_n + bias_ref[...] * z
            o_ref[s] = (gate_ref[s] * y).astype(o_ref.dtype)
        return

    n_slots = b_scr.shape[0]

    def inverse(group, slots, first, last):
        wk = plan.class_weight(group[-1])
        for off in range(0, n2, R):
            vals = []
            for k1, slot in zip(group, slots):
                br = b_scr[slot, off:off + R, :]
                if k1 == 0:
                    vals.append((br, None))
                    continue
                bi = b_scr[slot, n2 + off:n2 + off + R, :]
                tc = _lane_tile(twc_ref[k1, off:off + R, :], lanes)
                ts = _lane_tile(tws_ref[k1, off:off + R, :], lanes)
                vals.append((br * tc + bi * ts, bi * tc - br * ts))
            if len(group) == 2:
                (ar, ai), (br_, bi_) = vals
                by_parity = [(_lin2(1.0, ar, 1.0, br_), _lin2(1.0, ai, -1.0, bi_)),
                             (_lin2(1.0, ar, -1.0, br_), _lin2(1.0, ai, 1.0, bi_))]
            else:
                by_parity = [vals[0], vals[0]]
            for m in range(plan.data_slabs):
                c, s = plan.stage1_coef(m, group[0])
                pr, pi = by_parity[m % 2]
                t = _lin2(wk * c, pr, wk * s, pi)
                rows = slice(m * n2 + off, m * n2 + off + R)
                if last:
                    z = z_ref[0, rows, :]
                    tot = t if first else acc_scr[rows, :] + t
                    y = tot * inv_n + bias_ref[...] * z
                    o_ref[0, rows, :] = (gate_ref[0, rows, :] * y).astype(o_ref.dtype)
                elif first:
                    acc_scr[rows, :] = t
                else:
                    acc_scr[rows, :] += t

    _fft_forward_all(plan, plan.data_slabs, z_ref, (0,), twc_ref, tws_ref, abuf)
    groups = plan.class_groups()
    issued = 0
    pending = None
    for gi, group in enumerate(groups):
        slots = []
        for k1 in group:
            slots.append(issued % n_slots)
            dots(k1, issued % n_slots)
            issued += 1
        if pending is not None:
            inverse(*pending, first=pending_first, last=False)
        pending, pending_first = (group, slots), gi == 0
    inverse(*pending, first=len(groups) == 1, last=True)


def _fftconv(plan, z, z_col, gate, gate_col, bias, hspec, n_seq, lanes, out_dtype):
    L, n2 = plan.L, plan.n2
    nc = D_MODEL // lanes
    fwd, inv, twc, tws = _dft_consts(plan.n1, n2)
    zc, gc = z_col // lanes, gate_col // lanes
    acc_rows = L if plan.n1 > 1 else 8
    sb = 1 if plan.n1 > 1 else min(n_seq, 4)
    return pl.pallas_call(
        functools.partial(_fftconv_kernel, plan=plan),
        grid=(nc, n_seq // sb),
        in_specs=[
            pl.BlockSpec((sb, L, lanes), lambda c, b: (b, 0, zc + c)),
            pl.BlockSpec((sb, L, lanes), lambda c, b: (b, 0, gc + c)),
            pl.BlockSpec((1, lanes), lambda c, b: (0, c)),
            pl.BlockSpec((plan.classes, 2 * n2, lanes), lambda c, b: (0, 0, c),
                         pipeline_mode=pl.Buffered(1)),
            _const_spec((2 * n2, 2 * n2)),
            _const_spec((2 * n2, 2 * n2)),
            _const_spec((plan.classes, n2, TW_LANES)),
            _const_spec((plan.classes, n2, TW_LANES)),
        ],
        out_specs=pl.BlockSpec((sb, L, lanes), lambda c, b: (b, 0, c)),
        out_shape=jax.ShapeDtypeStruct((n_seq, L, D_MODEL), out_dtype),
        scratch_shapes=[
            pltpu.VMEM((plan.classes, 2 * n2, lanes) if plan.n1 > 1 else (1, 16, lanes), BF16),
            pltpu.VMEM((2, 2 * n2, lanes), BF16),
            pltpu.VMEM((4, 2 * n2, lanes), F32),
            pltpu.VMEM((acc_rows, lanes), F32),
        ],
        compiler_params=_cparams(("parallel", "arbitrary")),
        name="hyena_fftconv",
    )(z, gate, bias, hspec, jnp.asarray(fwd).astype(BF16), jnp.asarray(inv).astype(BF16),
      jnp.asarray(twc), jnp.asarray(tws))


@functools.lru_cache(maxsize=None)
def _filter_tables(L):
    r = np.arange(2 * L)
    p = np.where(r < L, r, 2 * L - r)
    p[L] = 0
    t = p / (L - 1.0)
    w = 2.0 * np.pi * p / L
    f = np.linspace(1e-4, HY_BANDS - 1, HY_BANDS)
    feat = np.concatenate([t[:, None], np.cos(np.outer(w, f)), -np.sin(np.outer(w, f))], axis=1)
    feat = np.pad(feat, ((0, 0), (0, HY_EMB_PAD - HY_EMB))).astype(np.float32)
    t_tab = np.repeat(t[:, None], TW_LANES, axis=1).astype(np.float32)
    return feat, t_tab


def _filter_hidden_kernel(feat_ref, w1_ref, b1_ref, w2_ref, b2_ref, fq_ref, o_ref):
    hp = lax.Precision.HIGHEST
    h = jnp.sin(fq_ref[...] * (jnp.dot(feat_ref[...], w1_ref[...], preferred_element_type=F32,
                                        precision=hp) + b1_ref[...]))
    h = jnp.sin(fq_ref[...] * (jnp.dot(h, w2_ref[...], preferred_element_type=F32,
                                        precision=hp) + b2_ref[...]))
    o_ref[...] = h


def _filter_hidden(feat, w1, b1, w2, b2, freq):
    rows = feat.shape[0]
    br = 1024 if rows % 1024 == 0 else rows
    vec = pl.BlockSpec((1, HY_FH), lambda i: (0, 0))
    return pl.pallas_call(
        _filter_hidden_kernel,
        grid=(rows // br,),
        in_specs=[
            pl.BlockSpec((br, HY_EMB_PAD), lambda i: (i, 0)),
            pl.BlockSpec((HY_EMB_PAD, HY_FH), lambda i: (0, 0)),
            vec,
            pl.BlockSpec((HY_FH, HY_FH), lambda i: (0, 0)),
            vec, vec,
        ],
        out_specs=pl.BlockSpec((br, HY_FH), lambda i: (i, 0)),
        out_shape=jax.ShapeDtypeStruct((rows, HY_FH), F32),
        compiler_params=_cparams(("parallel",)),
        name="hyena_filter_hidden",
    )(feat, w1, b1, w2, b2, freq)


def _filter_spec_kernel(hid_ref, w3a_ref, w3b_ref, t_ref, delta_ref, fwd_ref, twc_ref, tws_ref,
                        o_ref, f_scr, abuf, *, plan):
    L, n2 = plan.L, plan.n2
    lanes = o_ref.shape[-1]
    hp = lax.Precision.HIGHEST
    absd = jnp.abs(delta_ref[...])
    rows_per = min(L, 512)

    def gen(r, carry):
        for base, w_ref in ((0, w3a_ref), (L, w3b_ref)):
            rows = pl.ds(pl.multiple_of(base + r * rows_per, rows_per), rows_per)
            f = jnp.dot(hid_ref[rows, :], w_ref[0], preferred_element_type=F32, precision=hp)
            f_scr[rows, :] = f * jnp.exp(-_lane_tile(t_ref[rows, :], lanes) * absd)
        return carry

    lax.fori_loop(0, L // rows_per, gen, 0)
    f_scr[L:L + 8] = jnp.where(lax.broadcasted_iota(jnp.int32, (8, 1), 0) > 0, f_scr[L:L + 8], 0.0)

    if plan.n1 == 1:
        abuf[0, 0:n2] = f_scr[...].astype(BF16)
    else:
        _fft_forward_all(plan, plan.n1, f_scr, (), twc_ref, tws_ref, abuf)
    o_ref[0, 0] = jnp.dot(fwd_ref[:, 0:n2], abuf[0, 0:n2], preferred_element_type=F32)
    for k1 in range(1, plan.classes):
        o_ref[0, k1] = jnp.dot(fwd_ref[...], abuf[k1], preferred_element_type=F32)


def _filter_spectrum(plan, hidden, w_f3, lanes):
    L, n2 = plan.L, plan.n2
    nc = D_MODEL // lanes
    fwd, _, twc, tws = _dft_consts(plan.n1, n2)
    _, t_tab = _filter_tables(L)
    delta = np.linspace(HY_MIN_DECAY, HY_MAX_DECAY, D_MODEL).astype(np.float32)[None, :]
    w3 = w_f3.reshape(HY_FH, 2, 2, D_MODEL).transpose(1, 2, 0, 3)
    return pl.pallas_call(
        functools.partial(_filter_spec_kernel, plan=plan),
        grid=(2, nc),
        in_specs=[
            _const_spec((2 * L, HY_FH)),
            pl.BlockSpec((None, 1, HY_FH, lanes), lambda o, c: (0, o, 0, c)),
            pl.BlockSpec((None, 1, HY_FH, lanes), lambda o, c: (1, o, 0, c)),
            _const_spec((2 * L, TW_LANES)),
            pl.BlockSpec((1, lanes), lambda o, c: (0, c)),
            _const_spec((2 * n2, 2 * n2)),
            _const_spec((plan.classes, n2, TW_LANES)),
            _const_spec((plan.classes, n2, TW_LANES)),
        ],
        out_specs=pl.BlockSpec((1, plan.classes, 2 * n2, lanes), lambda o, c: (o, 0, 0, c)),
        out_shape=jax.ShapeDtypeStruct((2, plan.classes, 2 * n2, D_MODEL), F32),
        scratch_shapes=[pltpu.VMEM((2 * L, lanes), F32),
                        pltpu.VMEM((plan.classes, 2 * n2, lanes), BF16)],
        compiler_params=_cparams(("parallel", "parallel")),
        name="hyena_filter_spectrum",
    )(hidden, w3, w3, jnp.asarray(t_tab), jnp.asarray(delta), jnp.asarray(fwd).astype(BF16),
      jnp.asarray(twc), jnp.asarray(tws))


@functools.lru_cache(maxsize=None)
def _rope_tables(L):
    t = np.arange(L)
    inv = ROPE_THETA ** (-np.arange(ROPE_PAIRS, dtype=np.float32) / ROPE_PAIRS)
    ang_row = (t // GRID_W).astype(np.float32)[:, None] * inv
    ang_col = (t % GRID_W).astype(np.float32)[:, None] * inv
    zero = np.zeros_like(ang_row)
    cos = np.concatenate([np.cos(ang_row)] * 2 + [np.cos(ang_col)] * 2, axis=1)
    sa = np.concatenate([-np.sin(ang_row), zero, -np.sin(ang_col), zero], axis=1)
    sb = np.concatenate([zero, np.sin(ang_row), zero, np.sin(ang_col)], axis=1)
    q_scale = HEAD_DIM ** -0.5
    variants = lambda t, ident: np.stack([t * q_scale, t, np.full_like(t, ident)]).astype(np.float32)
    return variants(cos, 1.0), variants(sa, 0.0), variants(sb, 0.0)


def _hyena_mixer(x, g, shift, scale, gate, w_in, conv_w, hspec, bias, w_out, seq_len, lanes, bm):
    G, S, D = x.shape
    plan = _FftPlan(seq_len)
    n_seq = G * S // seq_len
    u = _hyena_in(x, g, shift, scale, w_in, conv_w, seq_len, bm, 1024)
    u = u.reshape(n_seq, seq_len, 3 * D)
    z1 = _fftconv(plan, u, 2 * D, u, 0, bias[0:1], hspec[0], n_seq, lanes, F32)
    z2 = _fftconv(plan, z1, 0, u, D, bias[1:2], hspec[1], n_seq, lanes, BF16)
    return _proj_residual(z2.reshape(G * S, D), w_out, x, gate, bm)


def kernel(x_prompt, x_sample, cache_k, cache_v, c, c_ctx, w_mod, b_mod, norm_mix, norm_ffn, norm_final, w_qkv, w_o, attn_sink, hy_w_in, hy_conv, hy_w_f1, hy_b_f1, hy_w_f2, hy_b_f2, hy_w_f3, hy_freq, hy_bias, hy_w_out, ffn_w_up, ffn_conv, ffn_w_down):
    B, SEQ, D = x_prompt.shape
    DB, DSEQ, _ = x_sample.shape
    depth = w_mod.shape[0]

    n_cond = DB + 1
    cvec = jnp.concatenate([c, c_ctx[None, :], jnp.zeros((16 - n_cond, D), F32)], axis=0)
    mod = _modulation(cvec, w_mod, b_mod).reshape(depth, 16, 6, 1, D)

    xs = x_sample
    xp = x_prompt.reshape(1, B * SEQ, D)
    streams = [
        dict(x=xs, rows=slice(0, DB), seq=DSEQ, n_seq=DB, lanes=256),
        dict(x=xp, rows=slice(DB, DB + 1), seq=SEQ, n_seq=B, lanes=256),
    ]
    bm = 1024
    rope_tabs = tuple(jnp.asarray(t) for t in _rope_tables(DSEQ))
    new_k, new_v = [], []

    for i in range(depth):
        g_mix = norm_mix[i][None, :]
        g_ffn = norm_ffn[i][None, :]
        if i % 2 == 0:
            a = i // 2
            wq = w_qkv[a].astype(BF16)
            wo = w_o[a].astype(BF16)
        else:
            hl = i // 2
            w_in = hy_w_in[hl].astype(BF16)
            w_out = hy_w_out[hl].astype(BF16)
            w1 = jnp.pad(hy_w_f1[hl], ((0, HY_EMB_PAD - HY_EMB), (0, 0)))
        w_up = ffn_w_up[i].astype(BF16)
        w_down = ffn_w_down[i].astype(BF16)
        g_final = norm_final[None, :] if i == depth - 1 else None

        for si, st in enumerate(streams):
            x = st["x"]
            m = mod[i, st["rows"]]
            sh1, sc1, g1, sh2, sc2, g2 = (m[:, t] for t in range(6))
            latent = si == 0
            if i % 2 == 0:
                res = _qkv_proj(x, g_mix, sh1, sc1, wq, rope_tabs if latent else None,
                                emit_kv=not latent, bm=bm)
                qkv = res[0]
                if latent:
                    ck = cache_k[:, a].reshape(DB, -1, KV_DIM).astype(BF16)
                    cv = cache_v[:, a].reshape(DB, -1, KV_DIM).astype(BF16)
                    att = _attention(qkv, attn_sink[a], ck, cv, DB, DSEQ, window=True)
                else:
                    kv = res[1]
                    new_k.append(kv[:, :KV_DIM].reshape(B, 1, SEQ, N_KV_HEADS, HEAD_DIM))
                    new_v.append(kv[:, KV_DIM:].reshape(B, 1, SEQ, N_KV_HEADS, HEAD_DIM))
                    ck = qkv[:, Q_DIM:Q_DIM + KV_DIM].reshape(B, SEQ, KV_DIM)
                    cv = qkv[:, Q_DIM + KV_DIM:].reshape(B, SEQ, KV_DIM)
                    att = _attention(qkv, attn_sink[a], ck, cv, B, SEQ, window=False)
                x = _proj_residual(att, wo, x, g1, bm)
            else:
                plan = _FftPlan(st["seq"])
                feat, _ = _filter_tables(st["seq"])
                hidden = _filter_hidden(jnp.asarray(feat), w1, hy_b_f1[hl][None, :], hy_w_f2[hl],
                                        hy_b_f2[hl][None, :], hy_freq[hl][None, :])
                hspec = _filter_spectrum(plan, hidden, hy_w_f3[hl], st["lanes"])
                x = _hyena_mixer(x, g_mix, sh1, sc1, g1, w_in, hy_conv[hl], hspec, hy_bias[hl],
                                 w_out, st["seq"], st["lanes"], bm)
            x = _conv_ffn(x, g_ffn, sh2, sc2, g2, w_up, ffn_conv[i], w_down, st["seq"], bm, 512,
                          g_final=g_final)
            st["x"] = x

    y_sample = streams[0]["x"]
    y_prompt = streams[1]["x"].reshape(B, SEQ, D)
    new_cache_k = new_k[0] if len(new_k) == 1 else jnp.concatenate(new_k, axis=1)
    new_cache_v = new_v[0] if len(new_v) == 1 else jnp.concatenate(new_v, axis=1)
    return (y_prompt, y_sample, new_cache_k, new_cache_v)
```

```python
import functools
import math

import jax
import jax.numpy as jnp
import numpy as np
from jax import lax
from jax.experimental import pallas as pl
from jax.experimental.pallas import tpu as pltpu

F32 = jnp.float32
BF16 = jnp.bfloat16

D_MODEL = 2048
HEAD_DIM = 128
N_HEADS = 16
N_KV_HEADS = 4
GROUP = N_HEADS // N_KV_HEADS
Q_DIM = N_HEADS * HEAD_DIM
KV_DIM = N_KV_HEADS * HEAD_DIM
QKV_DIM = Q_DIM + 2 * KV_DIM
GRID_W = 64
WINDOW = 128
ROPE_THETA = 10000.0
ROPE_PAIRS = HEAD_DIM // 4
HY_BANDS = 16
HY_EMB = 1 + 2 * HY_BANDS
HY_EMB_PAD = 40
HY_FH = 64
HY_MIN_DECAY = math.log(1e-2) / 1.5
HY_MAX_DECAY = math.log(1e-2) / 0.3
D_FF = 5632
EPS = 1e-6
NEG_INF = -1e30

VMEM_LIMIT = 56 * 1024 * 1024
NORM_ROWS = 16
HALO = 16
FFT_N2 = 256
FFT_N2_SINGLE = 512
FFT_ROWS = 32
FFT_FWD_ROWS = 16
SUB_COLS = 256
QKV_BN = 1024
ATTN_ROWS = 32
FFN_DOWN_COLS = 1024
TW_LANES = 128


def _cparams(sem):
    return pltpu.CompilerParams(dimension_semantics=sem, vmem_limit_bytes=VMEM_LIMIT)


def _store_modulated_norm(dst_ref, dst0, x_ref, n_rows, g_ref, sh_ref, sc_ref):
    gain = g_ref[...] * (1.0 + sc_ref[0])
    shift = sh_ref[0]
    for r in range(0, n_rows, NORM_ROWS):
        x = x_ref[0, r:r + NORM_ROWS, :]
        ms = jnp.mean(x * x, axis=-1, keepdims=True)
        y = x * lax.rsqrt(ms + EPS) * gain + shift
        dst_ref[dst0 + r:dst0 + r + NORM_ROWS, :] = y.astype(BF16)


def _mod_kernel(c_ref, w_ref, b_ref, o_ref):
    c = c_ref[...]
    s = c * (1.0 / (1.0 + jnp.exp(-c)))
    o_ref[0] = jnp.dot(s, w_ref[0], preferred_element_type=F32,
                       precision=lax.Precision.HIGHEST) + b_ref[0]


def _modulation(cvec, w_mod, b_mod):
    depth, d, n = w_mod.shape
    rows = cvec.shape[0]
    bn = 1024
    return pl.pallas_call(
        _mod_kernel,
        grid=(depth, n // bn),
        in_specs=[
            pl.BlockSpec((rows, d), lambda l, j: (0, 0)),
            pl.BlockSpec((1, d, bn), lambda l, j: (l, 0, j)),
            pl.BlockSpec((1, 1, bn), lambda l, j: (l, 0, j)),
        ],
        out_specs=pl.BlockSpec((1, rows, bn), lambda l, j: (l, 0, j)),
        out_shape=jax.ShapeDtypeStruct((depth, rows, n), F32),
        compiler_params=_cparams(("parallel", "parallel")),
        name="modulation",
    )(cvec, w_mod, b_mod.reshape(depth, 1, n))


def _qkv_kernel(*refs, rope, emit_kv):
    x_ref, g_ref, sh_ref, sc_ref, w_ref = refs[:5]
    pos = 5
    if rope:
        tabs = (refs[pos:pos + 3], refs[pos + 3:pos + 6])
        pos += 6
    o_ref = refs[pos]
    pos += 1
    if emit_kv:
        kv_ref = refs[pos]
        pos += 1
    h_scr = refs[pos]

    j = pl.program_id(2)
    n_q = Q_DIM // QKV_BN
    heads = QKV_BN // HEAD_DIM

    @pl.when(j == 0)
    def _():
        _store_modulated_norm(h_scr, 0, x_ref, h_scr.shape[0], g_ref, sh_ref, sc_ref)

    q_scale = jnp.where(j < n_q, HEAD_DIM ** -0.5, 1.0)
    for c0 in range(0, QKV_BN, SUB_COLS):
        cols = slice(c0, c0 + SUB_COLS)
        acc = jnp.dot(h_scr[...], w_ref[:, cols], preferred_element_type=F32)
        if emit_kv:
            kv_ref[:, cols] = acc
        for h in range(SUB_COLS // HEAD_DIM):
            xh = acc[:, h * HEAD_DIM:(h + 1) * HEAD_DIM]
            if rope:
                cos_ref, sa_ref, sb_ref = tabs[c0 // (QKV_BN // 2)]
                xh = (xh * cos_ref[...] + pltpu.roll(xh, HEAD_DIM - 32, axis=1) * sa_ref[...]
                      + pltpu.roll(xh, 32, axis=1) * sb_ref[...])
            else:
                xh = xh * q_scale
            o_ref[:, c0 + h * HEAD_DIM:c0 + (h + 1) * HEAD_DIM] = xh.astype(BF16)


def _qkv_proj(x, g, shift, scale, w, rope_tabs, emit_kv, bm):
    G, S, D = x.shape
    nb = S // bm
    bn = QKV_BN
    assert 2 * KV_DIM == bn and Q_DIM % bn == 0 and bn // HEAD_DIM == 2 * N_KV_HEADS
    nj = QKV_DIM // bn
    n_q = Q_DIM // bn
    rope = rope_tabs is not None
    in_specs = [
        pl.BlockSpec((1, bm, D), lambda gi, i, j: (gi, i, 0)),
        pl.BlockSpec((1, D), lambda gi, i, j: (0, 0)),
        pl.BlockSpec((1, 1, D), lambda gi, i, j: (gi, 0, 0)),
        pl.BlockSpec((1, 1, D), lambda gi, i, j: (gi, 0, 0)),
        pl.BlockSpec((D, bn), lambda gi, i, j: (0, j)),
    ]
    args = [x, g, shift, scale, w]
    if rope:
        for other in (1, 2):
            for t in rope_tabs:
                in_specs.append(pl.BlockSpec(
                    (None, bm, HEAD_DIM),
                    lambda gi, i, j, other=other: (jnp.where(j < n_q, 0, other), i, 0)))
                args.append(t)
    out_specs = [pl.BlockSpec((bm, bn), lambda gi, i, j: (gi * nb + i, j))]
    out_shape = [jax.ShapeDtypeStruct((G * S, QKV_DIM), BF16)]
    if emit_kv:
        out_specs.append(pl.BlockSpec((bm, bn), lambda gi, i, j: (gi * nb + i, 0)))
        out_shape.append(jax.ShapeDtypeStruct((G * S, 2 * KV_DIM), F32))
    res = pl.pallas_call(
        functools.partial(_qkv_kernel, rope=rope, emit_kv=emit_kv),
        grid=(G, nb, nj),
        in_specs=in_specs,
        out_specs=out_specs,
        out_shape=out_shape,
        scratch_shapes=[pltpu.VMEM((bm, D), BF16)],
        compiler_params=_cparams(("parallel", "parallel", "arbitrary")),
        name="qkv_proj",
    )(*args)
    return res


def _attn_kernel(*refs, window, bq):
    sink_ref, q_ref = refs[:2]
    pos = 2
    if window:
        kp_ref, kc_ref, kn_ref, vp_ref, vc_ref, vn_ref, bias_ref = refs[pos:pos + 7]
        pos += 7
    ck_ref, cv_ref, o_ref, k_scr, v_scr, s_scr, p_scr = refs[pos:pos + 7]

    n_ctx = ck_ref.shape[1]
    n_win = 3 * bq if window else 0
    tr = ATTN_ROWS

    for h in range(N_KV_HEADS):
        slot = h % 2
        hs = slice(h * HEAD_DIM, (h + 1) * HEAD_DIM)
        if window:
            for t, (k_ref, v_ref) in enumerate(((kp_ref, vp_ref), (kc_ref, vc_ref), (kn_ref, vn_ref))):
                k_scr[slot, t * bq:(t + 1) * bq] = k_ref[:, hs]
                v_scr[slot, t * bq:(t + 1) * bq] = v_ref[:, hs]
        k_scr[slot, n_win:] = ck_ref[0, :, hs]
        v_scr[slot, n_win:] = cv_ref[0, :, hs]
        qs = jnp.concatenate(
            [q_ref[:, (h * GROUP + g) * HEAD_DIM:(h * GROUP + g + 1) * HEAD_DIM] for g in range(GROUP)],
            axis=0)
        s_scr[slot] = lax.dot_general(qs, k_scr[slot], (((1,), (1,)), ((), ())),
                                      preferred_element_type=F32)
        for r0 in range(0, GROUP * bq, tr):
            sk = sink_ref[h * GROUP + r0 // bq]
            s = s_scr[slot, r0:r0 + tr, :]
            if window:
                q0 = r0 % bq
                s = s + bias_ref[0, q0:q0 + tr, :]
            m = jnp.maximum(jnp.max(s, axis=-1, keepdims=True), sk)
            p = jnp.exp(s - m)
            denom = jnp.sum(p, axis=-1, keepdims=True) + jnp.exp(sk - m)
            p_scr[slot, r0:r0 + tr, :] = (p * (1.0 / denom)).astype(BF16)
        o = jnp.dot(p_scr[slot], v_scr[slot], preferred_element_type=F32)
        for g in range(GROUP):
            c0 = (h * GROUP + g) * HEAD_DIM
            o_ref[:, c0:c0 + HEAD_DIM] = o[g * bq:(g + 1) * bq].astype(BF16)


def _window_bias(n_ctx):
    i = np.arange(WINDOW)[:, None]
    j = np.arange(WINDOW)[None, :]
    prev = np.where(j >= i, 0.0, NEG_INF)
    cur = np.zeros((WINDOW, WINDOW))
    nxt = np.where(j <= i, 0.0, NEG_INF)
    dead = np.full((WINDOW, WINDOW), NEG_INF)
    ctx = np.zeros((WINDOW, n_ctx))
    first = np.concatenate([dead, cur, nxt, ctx], axis=1)
    mid = np.concatenate([prev, cur, nxt, ctx], axis=1)
    last = np.concatenate([prev, cur, dead, ctx], axis=1)
    return np.stack([first, mid, last]).astype(np.float32)


def _attention(qkv, sink, ctx_k, ctx_v, n_batch, seq, window):
    kcol = Q_DIM // KV_DIM
    vcol = kcol + 1
    n_ctx = ctx_k.shape[1]
    if window:
        bq = WINDOW
        nq = seq // bq
        in_specs = [
            pl.BlockSpec(memory_space=pltpu.SMEM),
            pl.BlockSpec((bq, Q_DIM), lambda b, n: (b * nq + n, 0)),
            pl.BlockSpec((bq, KV_DIM), lambda b, n: (b * nq + jnp.maximum(n - 1, 0), kcol)),
            pl.BlockSpec((bq, KV_DIM), lambda b, n: (b * nq + n, kcol)),
            pl.BlockSpec((bq, KV_DIM), lambda b, n: (b * nq + jnp.minimum(n + 1, nq - 1), kcol)),
            pl.BlockSpec((bq, KV_DIM), lambda b, n: (b * nq + jnp.maximum(n - 1, 0), vcol)),
            pl.BlockSpec((bq, KV_DIM), lambda b, n: (b * nq + n, vcol)),
            pl.BlockSpec((bq, KV_DIM), lambda b, n: (b * nq + jnp.minimum(n + 1, nq - 1), vcol)),
            pl.BlockSpec((1, bq, 3 * bq + n_ctx),
                         lambda b, n: (jnp.where(n == 0, 0, jnp.where(n == nq - 1, 2, 1)), 0, 0)),
            pl.BlockSpec((1, n_ctx, KV_DIM), lambda b, n: (b, 0, 0)),
            pl.BlockSpec((1, n_ctx, KV_DIM), lambda b, n: (b, 0, 0)),
        ]
        args = [sink, qkv, qkv, qkv, qkv, qkv, qkv, qkv, jnp.asarray(_window_bias(n_ctx)), ctx_k, ctx_v]
    else:
        bq = seq
        nq = 1
        in_specs = [
            pl.BlockSpec(memory_space=pltpu.SMEM),
            pl.BlockSpec((bq, Q_DIM), lambda b, n: (b, 0)),
            pl.BlockSpec((1, n_ctx, KV_DIM), lambda b, n: (b, 0, 0)),
            pl.BlockSpec((1, n_ctx, KV_DIM), lambda b, n: (b, 0, 0)),
        ]
        args = [sink, qkv, ctx_k, ctx_v]
    n_keys = (3 * bq if window else 0) + n_ctx
    return pl.pallas_call(
        functools.partial(_attn_kernel, window=window, bq=bq),
        grid=(n_batch, nq),
        in_specs=in_specs,
        out_specs=pl.BlockSpec((bq, Q_DIM), lambda b, n: (b * nq + n, 0)),
        out_shape=jax.ShapeDtypeStruct((n_batch * seq, Q_DIM), BF16),
        scratch_shapes=[
            pltpu.VMEM((2, n_keys, HEAD_DIM), BF16),
            pltpu.VMEM((2, n_keys, HEAD_DIM), BF16),
            pltpu.VMEM((2, GROUP * bq, n_keys), F32),
            pltpu.VMEM((2, GROUP * bq, n_keys), BF16),
        ],
        compiler_params=_cparams(("parallel", "parallel")),
        name="attention",
    )(*args)


def _proj_res_kernel(a_ref, w_ref, x_ref, gate_ref, o_ref):
    half_w = o_ref.shape[-1] // 2
    for half in range(2):
        cols = slice(half * half_w, (half + 1) * half_w)
        acc = jnp.dot(a_ref[...], w_ref[:, cols], preferred_element_type=F32)
        o_ref[0, :, cols] = x_ref[0, :, cols] + gate_ref[0, :, cols] * acc


def _proj_residual(a, w, x, gate, bm):
    G, S, D = x.shape
    K = a.shape[1]
    nb = S // bm
    bn = 1024
    return pl.pallas_call(
        _proj_res_kernel,
        grid=(G, nb, D // bn),
        in_specs=[
            pl.BlockSpec((bm, K), lambda gi, i, j: (gi * nb + i, 0)),
            pl.BlockSpec((K, bn), lambda gi, i, j: (0, j)),
            pl.BlockSpec((1, bm, bn), lambda gi, i, j: (gi, i, j)),
            pl.BlockSpec((1, 1, bn), lambda gi, i, j: (gi, 0, j)),
        ],
        out_specs=pl.BlockSpec((1, bm, bn), lambda gi, i, j: (gi, i, j)),
        out_shape=jax.ShapeDtypeStruct((G, S, D), F32),
        compiler_params=_cparams(("parallel", "parallel", "parallel")),
        name="proj_residual",
    )(a, w, x, gate)


def _fill_h_ext(h_scr, xm_ref, xp_ref, xn_ref, g_ref, sh_ref, sc_ref, bm):
    _store_modulated_norm(h_scr, 0, xp_ref, HALO, g_ref, sh_ref, sc_ref)
    _store_modulated_norm(h_scr, HALO, xm_ref, bm, g_ref, sh_ref, sc_ref)
    _store_modulated_norm(h_scr, HALO + bm, xn_ref, HALO, g_ref, sh_ref, sc_ref)


def _dwconv3_rows(u_ext, cw, row0, bm, seq_len):
    assert seq_len & (seq_len - 1) == 0
    pos = (row0 + lax.broadcasted_iota(jnp.int32, (bm, 1), 0)) & (seq_len - 1)
    prev = jnp.where(pos == 0, 0.0, u_ext[HALO - 1:HALO - 1 + bm])
    nxt = jnp.where(pos == seq_len - 1, 0.0, u_ext[HALO + 1:HALO + 1 + bm])
    return prev * cw[0:1] + u_ext[HALO:HALO + bm] * cw[1:2] + nxt * cw[2:3]


def _halo_specs(bm, D, S):
    nh = bm // HALO
    last = S // HALO - 1
    return [
        pl.BlockSpec((1, bm, D), lambda gi, i, j: (gi, i, 0)),
        pl.BlockSpec((1, HALO, D), lambda gi, i, j: (gi, jnp.maximum(i * nh - 1, 0), 0)),
        pl.BlockSpec((1, HALO, D), lambda gi, i, j: (gi, jnp.minimum((i + 1) * nh, last), 0)),
    ]


def _ffn_kernel(*refs, bm, seq_len, final_norm):
    (xm_ref, xp_ref, xn_ref, g_ref, sh_ref, sc_ref, gate_ref,
     wa_ref, wb_ref, cw_ref, wd_ref) = refs[:11]
    pos = 11
    if final_norm:
        gf_ref = refs[pos]
        pos += 1
    o_ref, h_scr = refs[pos:pos + 2]

    i = pl.program_id(1)
    j = pl.program_id(2)
    D = o_ref.shape[-1]

    @pl.when(j == 0)
    def _():
        _fill_h_ext(h_scr, xm_ref, xp_ref, xn_ref, g_ref, sh_ref, sc_ref, bm)
        o_ref[0] = jnp.zeros((bm, D), F32)

    ua = jnp.dot(h_scr[...], wa_ref[...], preferred_element_type=F32)
    ub = jnp.dot(h_scr[HALO:HALO + bm], wb_ref[...], preferred_element_type=F32)
    a = _dwconv3_rows(ua, cw_ref[...], i * bm, bm, seq_len)
    act = (a * (1.0 / (1.0 + jnp.exp(-a))) * ub).astype(BF16)
    for c0 in range(0, D, FFN_DOWN_COLS):
        cols = slice(c0, c0 + FFN_DOWN_COLS)
        o_ref[0, :, cols] += jnp.dot(act, wd_ref[:, cols], preferred_element_type=F32)

    @pl.when(j == pl.num_programs(2) - 1)
    def _():
        y = xm_ref[0] + gate_ref[0] * o_ref[0]
        if final_norm:
            ms = jnp.mean(y * y, axis=-1, keepdims=True)
            y = y * lax.rsqrt(ms + EPS) * gf_ref[...]
        o_ref[0] = y


def _conv_ffn(x, g, shift, scale, gate, w_up, conv_w, w_down, seq_len, bm, bf, g_final=None):
    G, S, D = x.shape
    nb = S // bm
    nc = D_FF // bf
    final_norm = g_final is not None
    in_specs = _halo_specs(bm, D, S) + [
        pl.BlockSpec((1, D), lambda gi, i, j: (0, 0)),
        pl.BlockSpec((1, 1, D), lambda gi, i, j: (gi, 0, 0)),
        pl.BlockSpec((1, 1, D), lambda gi, i, j: (gi, 0, 0)),
        pl.BlockSpec((1, 1, D), lambda gi, i, j: (gi, 0, 0)),
        pl.BlockSpec((D, bf), lambda gi, i, j: (0, j)),
        pl.BlockSpec((D, bf), lambda gi, i, j: (0, nc + j)),
        pl.BlockSpec((3, bf), lambda gi, i, j: (0, j)),
        pl.BlockSpec((bf, D), lambda gi, i, j: (j, 0)),
    ]
    args = [x, x, x, g, shift, scale, gate, w_up, w_up, conv_w, w_down]
    if final_norm:
        in_specs.append(pl.BlockSpec((1, D), lambda gi, i, j: (0, 0)))
        args.append(g_final)
    return pl.pallas_call(
        functools.partial(_ffn_kernel, bm=bm, seq_len=seq_len, final_norm=final_norm),
        grid=(G, nb, nc),
        in_specs=in_specs,
        out_specs=pl.BlockSpec((1, bm, D), lambda gi, i, j: (gi, i, 0), pipeline_mode=pl.Buffered(1)),
        out_shape=jax.ShapeDtypeStruct((G, S, D), F32),
        scratch_shapes=[pltpu.VMEM((bm + 2 * HALO, D), BF16)],
        compiler_params=_cparams(("parallel", "parallel", "arbitrary")),
        name="conv_ffn",
    )(*args)


def _hy_in_kernel(xm_ref, xp_ref, xn_ref, g_ref, sh_ref, sc_ref, w_ref, cw_ref, o_ref, h_scr,
                  *, bm, seq_len):
    i = pl.program_id(1)
    j = pl.program_id(2)

    @pl.when(j == 0)
    def _():
        _fill_h_ext(h_scr, xm_ref, xp_ref, xn_ref, g_ref, sh_ref, sc_ref, bm)

    for c0 in range(0, o_ref.shape[-1], SUB_COLS):
        cols = slice(c0, c0 + SUB_COLS)
        u = jnp.dot(h_scr[...], w_ref[:, cols], preferred_element_type=F32)
        o_ref[0, :, cols] = _dwconv3_rows(u, cw_ref[:, cols], i * bm, bm, seq_len)


def _hyena_in(x, g, shift, scale, w_in, conv_w, seq_len, bm, bn):
    G, S, D = x.shape
    nb = S // bm
    N = w_in.shape[1]
    in_specs = _halo_specs(bm, D, S) + [
        pl.BlockSpec((1, D), lambda gi, i, j: (0, 0)),
        pl.BlockSpec((1, 1, D), lambda gi, i, j: (gi, 0, 0)),
        pl.BlockSpec((1, 1, D), lambda gi, i, j: (gi, 0, 0)),
        pl.BlockSpec((D, bn), lambda gi, i, j: (0, j)),
        pl.BlockSpec((3, bn), lambda gi, i, j: (0, j)),
    ]
    return pl.pallas_call(
        functools.partial(_hy_in_kernel, bm=bm, seq_len=seq_len),
        grid=(G, nb, N // bn),
        in_specs=in_specs,
        out_specs=pl.BlockSpec((1, bm, bn), lambda gi, i, j: (gi, i, j)),
        out_shape=jax.ShapeDtypeStruct((G, S, N), F32),
        scratch_shapes=[pltpu.VMEM((bm + 2 * HALO, D), BF16)],
        compiler_params=_cparams(("parallel", "parallel", "arbitrary")),
        name="hyena_in",
    )(x, x, x, g, shift, scale, w_in, conv_w)


class _FftPlan:
    def __init__(self, seq_len):
        self.L = seq_len
        self.n2 = FFT_N2 if 2 * seq_len > FFT_N2_SINGLE else 2 * seq_len
        self.n1 = 2 * seq_len // self.n2
        self.n = self.n1 * self.n2
        self.classes = self.n1 // 2 + 1
        self.slab = min(self.n2, seq_len)
        self.data_slabs = seq_len // self.slab

    def stage1_coef(self, m, k1):
        th = 2.0 * math.pi * ((m * k1) % self.n1) / self.n1
        return _snap(math.cos(th)), _snap(-math.sin(th))

    def class_groups(self):
        half = self.n1 // 2
        groups = [(k, half - k) for k in range((half + 1) // 2)]
        if half % 2 == 0:
            groups.append((half // 2,))
        return groups

    def class_weight(self, k1):
        if self.n1 == 1:
            return 1.0
        return 1.0 if k1 in (0, self.n1 // 2) else 2.0


def _snap(v):
    for t in (0.0, 1.0, -1.0):
        if abs(v - t) < 1e-12:
            return t
    return v


@functools.lru_cache(maxsize=None)
def _dft_consts(n1, n2):
    n = n1 * n2
    k = np.arange(n2)
    ang = 2.0 * np.pi * np.outer(k, k) / n2
    fr, fi = np.cos(ang), -np.sin(ang)
    fwd = np.block([[fr, -fi], [fi, fr]]).astype(np.float32)
    inv = np.block([[fr, fi], [-fi, fr]]).astype(np.float32)
    classes = n1 // 2 + 1
    tw = 2.0 * np.pi * np.outer(np.arange(classes), np.arange(n2)) / n
    twc = np.repeat(np.cos(tw)[:, :, None], TW_LANES, axis=2).astype(np.float32)
    tws = np.repeat((-np.sin(tw))[:, :, None], TW_LANES, axis=2).astype(np.float32)
    return fwd, inv, twc, tws


def _const_spec(shape):
    nd = len(shape)
    return pl.BlockSpec(shape, lambda *_: (0,) * nd, pipeline_mode=pl.Buffered(1))


def _lane_tile(t, lanes):
    reps = lanes // t.shape[1]
    return t if reps == 1 else jnp.concatenate([t] * reps, axis=1)


def _scale(c, x):
    return x if c == 1.0 else (-x if c == -1.0 else c * x)


def _lin2(c1, x1, c2, x2):
    if x1 is None or c1 == 0.0:
        x1 = None
    if x2 is None or c2 == 0.0:
        x2 = None
    if x1 is None and x2 is None:
        return None
    if x2 is None:
        return _scale(c1, x1)
    if x1 is None:
        return _scale(c2, x2)
    if abs(abs(c1) - abs(c2)) < 1e-12:
        s = (x1 + x2) if (c1 > 0) == (c2 > 0) else (x1 - x2)
        return _scale(c1, s)
    return c1 * x1 + c2 * x2


def _rdft_half(xs):
    n = len(xs)
    if n == 1:
        return [(xs[0], None)]
    if n == 2:
        return [(_lin2(1.0, xs[0], 1.0, xs[1]), None), (_lin2(1.0, xs[0], -1.0, xs[1]), None)]
    half, quarter = n // 2, n // 4
    ev, od = _rdft_half(xs[0::2]), _rdft_half(xs[1::2])
    out = [None] * (half + 1)
    for j in range(quarter + 1):
        th = 2.0 * math.pi * j / n
        wr, wi = _snap(math.cos(th)), _snap(-math.sin(th))
        (er, ei), (orr, oi) = ev[j], od[j]
        tr = _lin2(wr, orr, -wi, oi)
        ti = _lin2(wi, orr, wr, oi)
        out[j] = (_lin2(1.0, er, 1.0, tr), _lin2(1.0, ei, 1.0, ti))
        if half - j != j:
            out[half - j] = (_lin2(1.0, er, -1.0, tr), _lin2(1.0, ti, -1.0, ei))
    return out


def _fft_forward_all(plan, n_slabs, src_ref, src_lead, twc_ref, tws_ref, abuf):
    n2, R = plan.n2, FFT_FWD_ROWS
    lanes = abuf.shape[-1]

    def body(r, carry):
        off = pl.multiple_of(r * R, R)
        for l0 in range(0, lanes, TW_LANES):
            ls = slice(l0, l0 + TW_LANES)
            xs = [src_ref[src_lead + (pl.ds(m * n2 + off, R), ls)] if m < n_slabs else None
                  for m in range(plan.n1)]
            for k1, (ar, ai) in enumerate(_rdft_half(xs)):
                if k1 == 0:
                    abuf[0, pl.ds(off, R), ls] = ar.astype(BF16)
                    continue
                tc = twc_ref[k1, pl.ds(off, R), :]
                ts = tws_ref[k1, pl.ds(off, R), :]
                abuf[k1, pl.ds(off, R), ls] = _lin2(1.0, ar * tc, -1.0, None if ai is None else ai * ts).astype(BF16)
                abuf[k1, pl.ds(n2 + off, R), ls] = _lin2(1.0, ar * ts, 1.0, None if ai is None else ai * tc).astype(BF16)
        return carry

    lax.fori_loop(0, n2 // R, body, 0)


def _fftconv_kernel(z_ref, gate_ref, bias_ref, h_ref, fwd_ref, inv_ref, twc_ref, tws_ref,
                    o_ref, abuf, ybuf, b_scr, acc_scr, *, plan):
    n2, L, R = plan.n2, plan.L, FFT_ROWS
    lanes = o_ref.shape[-1]
    inv_n = 1.0 / plan.n

    def dots(k1, slot):
        if k1 == 0:
            xf = jnp.dot(fwd_ref[:, 0:n2], abuf[0, 0:n2], preferred_element_type=F32)
        else:
            xf = jnp.dot(fwd_ref[...], abuf[k1], preferred_element_type=F32)
        xr, xi = xf[:n2], xf[n2:]
        hr, hi = h_ref[k1, 0:n2], h_ref[k1, n2:]
        ys = slot % ybuf.shape[0]
        ybuf[ys, 0:n2] = (xr * hr - xi * hi).astype(BF16)
        ybuf[ys, n2:] = (xr * hi + xi * hr).astype(BF16)
        if k1 == 0:
            b_scr[slot, 0:n2] = jnp.dot(inv_ref[0:n2, :], ybuf[ys], preferred_element_type=F32)
        else:
            b_scr[slot] = jnp.dot(inv_ref[...], ybuf[ys], preferred_element_type=F32)

    if plan.n1 == 1:
        hr, hi = h_ref[0, 0:n2], h_ref[0, n2:]
        for s in range(z_ref.shape[0]):
            z = z_ref[s]
            xf = jnp.dot(fwd_ref[:, 0:L], z.astype(BF16), preferred_element_type=F32)
            xr, xi = xf[:n2], xf[n2:]
            ybuf[s % 2, 0:n2] = (xr * hr - xi * hi).astype(BF16)
            ybuf[s % 2, n2:] = (xr * hi + xi * hr).astype(BF16)
            b = jnp.dot(inv_ref[0:L, :], ybuf[s % 2], preferred_element_type=F32)
            y = b * inv_n + bias_ref[...] * z
            o_ref[s] = (gate_ref[s] * y).astype(o_ref.dtype)
        return

    n_slots = b_scr.shape[0]

    def inverse(group, slots, first, last):
        wk = plan.class_weight(group[-1])
        for off in range(0, n2, R):
            vals = []
            for k1, slot in zip(group, slots):
                br = b_scr[slot, off:off + R, :]
                if k1 == 0:
                    vals.append((br, None))
                    continue
                bi = b_scr[slot, n2 + off:n2 + off + R, :]
                tc = _lane_tile(twc_ref[k1, off:off + R, :], lanes)
                ts = _lane_tile(tws_ref[k1, off:off + R, :], lanes)
                vals.append((br * tc + bi * ts, bi * tc - br * ts))
            if len(group) == 2:
                (ar, ai), (br_, bi_) = vals
                by_parity = [(_lin2(1.0, ar, 1.0, br_), _lin2(1.0, ai, -1.0, bi_)),
                             (_lin2(1.0, ar, -1.0, br_), _lin2(1.0, ai, 1.0, bi_))]
            else:
                by_parity = [vals[0], vals[0]]
            for m in range(plan.data_slabs):
                c, s = plan.stage1_coef(m, group[0])
                pr, pi = by_parity[m % 2]
                t = _lin2(wk * c, pr, wk * s, pi)
                rows = slice(m * n2 + off, m * n2 + off + R)
                if last:
                    z = z_ref[0, rows, :]
                    tot = t if first else acc_scr[rows, :] + t
                    y = tot * inv_n + bias_ref[...] * z
                    o_ref[0, rows, :] = (gate_ref[0, rows, :] * y).astype(o_ref.dtype)
                elif first:
                    acc_scr[rows, :] = t
                else:
                    acc_scr[rows, :] += t

    _fft_forward_all(plan, plan.data_slabs, z_ref, (0,), twc_ref, tws_ref, abuf)
    groups = plan.class_groups()
    issued = 0
    pending = None
    for gi, group in enumerate(groups):
        slots = []
        for k1 in group:
            slots.append(issued % n_slots)
            dots(k1, issued % n_slots)
            issued += 1
        if pending is not None:
            inverse(*pending, first=pending_first, last=False)
        pending, pending_first = (group, slots), gi == 0
    inverse(*pending, first=len(groups) == 1, last=True)


def _fftconv(plan, z, z_col, gate, gate_col, bias, hspec, n_seq, lanes, out_dtype):
    L, n2 = plan.L, plan.n2
    nc = D_MODEL // lanes
    fwd, inv, twc, tws = _dft_consts(plan.n1, n2)
    zc, gc = z_col // lanes, gate_col // lanes
    acc_rows = L if plan.n1 > 1 else 8
    sb = 1 if plan.n1 > 1 else min(n_seq, 4)
    return pl.pallas_call(
        functools.partial(_fftconv_kernel, plan=plan),
        grid=(nc, n_seq // sb),
        in_specs=[
            pl.BlockSpec((sb, L, lanes), lambda c, b: (b, 0, zc + c)),
            pl.BlockSpec((sb, L, lanes), lambda c, b: (b, 0, gc + c)),
            pl.BlockSpec((1, lanes), lambda c, b: (0, c)),
            pl.BlockSpec((plan.classes, 2 * n2, lanes), lambda c, b: (0, 0, c),
                         pipeline_mode=pl.Buffered(1)),
            _const_spec((2 * n2, 2 * n2)),
            _const_spec((2 * n2, 2 * n2)),
            _const_spec((plan.classes, n2, TW_LANES)),
            _const_spec((plan.classes, n2, TW_LANES)),
        ],
        out_specs=pl.BlockSpec((sb, L, lanes), lambda c, b: (b, 0, c)),
        out_shape=jax.ShapeDtypeStruct((n_seq, L, D_MODEL), out_dtype),
        scratch_shapes=[
            pltpu.VMEM((plan.classes, 2 * n2, lanes) if plan.n1 > 1 else (1, 16, lanes), BF16),
            pltpu.VMEM((2, 2 * n2, lanes), BF16),
            pltpu.VMEM((4, 2 * n2, lanes), F32),
            pltpu.VMEM((acc_rows, lanes), F32),
        ],
        compiler_params=_cparams(("parallel", "arbitrary")),
        name="hyena_fftconv",
    )(z, gate, bias, hspec, jnp.asarray(fwd).astype(BF16), jnp.asarray(inv).astype(BF16),
      jnp.asarray(twc), jnp.asarray(tws))


@functools.lru_cache(maxsize=None)
def _filter_tables(L):
    r = np.arange(2 * L)
    p = np.where(r < L, r, 2 * L - r)
    p[L] = 0
    t = p / (L - 1.0)
    w = 2.0 * np.pi * p / L
    f = np.linspace(1e-4, HY_BANDS - 1, HY_BANDS)
    feat = np.concatenate([t[:, None], np.cos(np.outer(w, f)), -np.sin(np.outer(w, f))], axis=1)
    feat = np.pad(feat, ((0, 0), (0, HY_EMB_PAD - HY_EMB))).astype(np.float32)
    t_tab = np.repeat(t[:, None], TW_LANES, axis=1).astype(np.float32)
    return feat, t_tab


def _filter_hidden_kernel(feat_ref, w1_ref, b1_ref, w2_ref, b2_ref, fq_ref, o_ref):
    hp = lax.Precision.HIGHEST
    h = jnp.sin(fq_ref[...] * (jnp.dot(feat_ref[...], w1_ref[...], preferred_element_type=F32,
                                        precision=hp) + b1_ref[...]))
    h = jnp.sin(fq_ref[...] * (jnp.dot(h, w2_ref[...], preferred_element_type=F32,
                                        precision=hp) + b2_ref[...]))
    o_ref[...] = h


def _filter_hidden(feat, w1, b1, w2, b2, freq):
    rows = feat.shape[0]
    br = 1024 if rows % 1024 == 0 else rows
    vec = pl.BlockSpec((1, HY_FH), lambda i: (0, 0))
    return pl.pallas_call(
        _filter_hidden_kernel,
        grid=(rows // br,),
        in_specs=[
            pl.BlockSpec((br, HY_EMB_PAD), lambda i: (i, 0)),
            pl.BlockSpec((HY_EMB_PAD, HY_FH), lambda i: (0, 0)),
            vec,
            pl.BlockSpec((HY_FH, HY_FH), lambda i: (0, 0)),
            vec, vec,
        ],
        out_specs=pl.BlockSpec((br, HY_FH), lambda i: (i, 0)),
        out_shape=jax.ShapeDtypeStruct((rows, HY_FH), F32),
        compiler_params=_cparams(("parallel",)),
        name="hyena_filter_hidden",
    )(feat, w1, b1, w2, b2, freq)


def _filter_spec_kernel(hid_ref, w3a_ref, w3b_ref, t_ref, delta_ref, fwd_ref, twc_ref, tws_ref,
                        o_ref, f_scr, abuf, *, plan):
    L, n2 = plan.L, plan.n2
    lanes = o_ref.shape[-1]
    hp = lax.Precision.HIGHEST
    absd = jnp.abs(delta_ref[...])
    rows_per = min(L, 512)

    def gen(r, carry):
        for base, w_ref in ((0, w3a_ref), (L, w3b_ref)):
            rows = pl.ds(pl.multiple_of(base + r * rows_per, rows_per), rows_per)
            f = jnp.dot(hid_ref[rows, :], w_ref[0], preferred_element_type=F32, precision=hp)
            f_scr[rows, :] = f * jnp.exp(-_lane_tile(t_ref[rows, :], lanes) * absd)
        return carry

    lax.fori_loop(0, L // rows_per, gen, 0)
    f_scr[L:L + 8] = jnp.where(lax.broadcasted_iota(jnp.int32, (8, 1), 0) > 0, f_scr[L:L + 8], 0.0)

    if plan.n1 == 1:
        abuf[0, 0:n2] = f_scr[...].astype(BF16)
    else:
        _fft_forward_all(plan, plan.n1, f_scr, (), twc_ref, tws_ref, abuf)
    o_ref[0, 0] = jnp.dot(fwd_ref[:, 0:n2], abuf[0, 0:n2], preferred_element_type=F32)
    for k1 in range(1, plan.classes):
        o_ref[0, k1] = jnp.dot(fwd_ref[...], abuf[k1], preferred_element_type=F32)


def _filter_spectrum(plan, hidden, w_f3, lanes):
    L, n2 = plan.L, plan.n2
    nc = D_MODEL // lanes
    fwd, _, twc, tws = _dft_consts(plan.n1, n2)
    _, t_tab = _filter_tables(L)
    delta = np.linspace(HY_MIN_DECAY, HY_MAX_DECAY, D_MODEL).astype(np.float32)[None, :]
    w3 = w_f3.reshape(HY_FH, 2, 2, D_MODEL).transpose(1, 2, 0, 3)
    return pl.pallas_call(
        functools.partial(_filter_spec_kernel, plan=plan),
        grid=(2, nc),
        in_specs=[
            _const_spec((2 * L, HY_FH)),
            pl.BlockSpec((None, 1, HY_FH, lanes), lambda o, c: (0, o, 0, c)),
            pl.BlockSpec((None, 1, HY_FH, lanes), lambda o, c: (1, o, 0, c)),
            _const_spec((2 * L, TW_LANES)),
            pl.BlockSpec((1, lanes), lambda o, c: (0, c)),
            _const_spec((2 * n2, 2 * n2)),
            _const_spec((plan.classes, n2, TW_LANES)),
            _const_spec((plan.classes, n2, TW_LANES)),
        ],
        out_specs=pl.BlockSpec((1, plan.classes, 2 * n2, lanes), lambda o, c: (o, 0, 0, c)),
        out_shape=jax.ShapeDtypeStruct((2, plan.classes, 2 * n2, D_MODEL), F32),
        scratch_shapes=[pltpu.VMEM((2 * L, lanes), F32),
                        pltpu.VMEM((plan.classes, 2 * n2, lanes), BF16)],
        compiler_params=_cparams(("parallel", "parallel")),
        name="hyena_filter_spectrum",
    )(hidden, w3, w3, jnp.asarray(t_tab), jnp.asarray(delta), jnp.asarray(fwd).astype(BF16),
      jnp.asarray(twc), jnp.asarray(tws))


@functools.lru_cache(maxsize=None)
def _rope_tables(L):
    t = np.arange(L)
    inv = ROPE_THETA ** (-np.arange(ROPE_PAIRS, dtype=np.float32) / ROPE_PAIRS)
    ang_row = (t // GRID_W).astype(np.float32)[:, None] * inv
    ang_col = (t % GRID_W).astype(np.float32)[:, None] * inv
    zero = np.zeros_like(ang_row)
    cos = np.concatenate([np.cos(ang_row)] * 2 + [np.cos(ang_col)] * 2, axis=1)
    sa = np.concatenate([-np.sin(ang_row), zero, -np.sin(ang_col), zero], axis=1)
    sb = np.concatenate([zero, np.sin(ang_row), zero, np.sin(ang_col)], axis=1)
    q_scale = HEAD_DIM ** -0.5
    variants = lambda t, ident: np.stack([t * q_scale, t, np.full_like(t, ident)]).astype(np.float32)
    return variants(cos, 1.0), variants(sa, 0.0), variants(sb, 0.0)


def _hyena_mixer(x, g, shift, scale, gate, w_in, conv_w, hspec, bias, w_out, seq_len, lanes, bm):
    G, S, D = x.shape
    plan = _FftPlan(seq_len)
    n_seq = G * S // seq_len
    u = _hyena_in(x, g, shift, scale, w_in, conv_w, seq_len, bm, 1024)
    u = u.reshape(n_seq, seq_len, 3 * D)
    z1 = _fftconv(plan, u, 2 * D, u, 0, bias[0:1], hspec[0], n_seq, lanes, F32)
    z2 = _fftconv(plan, z1, 0, u, D, bias[1:2], hspec[1], n_seq, lanes, BF16)
    return _proj_residual(z2.reshape(G * S, D), w_out, x, gate, bm)


def kernel(x_prompt, x_sample, cache_k, cache_v, c, c_ctx, w_mod, b_mod, norm_mix, norm_ffn, norm_final, w_qkv, w_o, attn_sink, hy_w_in, hy_conv, hy_w_f1, hy_b_f1, hy_w_f2, hy_b_f2, hy_w_f3, hy_freq, hy_bias, hy_w_out, ffn_w_up, ffn_conv, ffn_w_down):
    B, SEQ, D = x_prompt.shape
    DB, DSEQ, _ = x_sample.shape
    depth = w_mod.shape[0]

    n_cond = DB + 1
    cvec = jnp.concatenate([c, c_ctx[None, :], jnp.zeros((16 - n_cond, D), F32)], axis=0)
    mod = _modulation(cvec, w_mod, b_mod).reshape(depth, 16, 6, 1, D)

    xs = x_sample
    xp = x_prompt.reshape(1, B * SEQ, D)
    streams = [
        dict(x=xs, rows=slice(0, DB), seq=DSEQ, n_seq=DB, lanes=256),
        dict(x=xp, rows=slice(DB, DB + 1), seq=SEQ, n_seq=B, lanes=256),
    ]
    bm = 1024
    rope_tabs = tuple(jnp.asarray(t) for t in _rope_tables(DSEQ))
    new_k, new_v = [], []

    for i in range(depth):
        g_mix = norm_mix[i][None, :]
        g_ffn = norm_ffn[i][None, :]
        if i % 2 == 0:
            a = i // 2
            wq = w_qkv[a].astype(BF16)
            wo = w_o[a].astype(BF16)
        else:
            hl = i // 2
            w_in = hy_w_in[hl].astype(BF16)
            w_out = hy_w_out[hl].astype(BF16)
            w1 = jnp.pad(hy_w_f1[hl], ((0, HY_EMB_PAD - HY_EMB), (0, 0)))
        w_up = ffn_w_up[i].astype(BF16)
        w_down = ffn_w_down[i].astype(BF16)
        g_final = norm_final[None, :] if i == depth - 1 else None

        for si, st in enumerate(streams):
            x = st["x"]
            m = mod[i, st["rows"]]
            sh1, sc1, g1, sh2, sc2, g2 = (m[:, t] for t in range(6))
            latent = si == 0
            if i % 2 == 0:
                res = _qkv_proj(x, g_mix, sh1, sc1, wq, rope_tabs if latent else None,
                                emit_kv=not latent, bm=bm)
                qkv = res[0]
                if latent:
                    ck = cache_k[:, a].reshape(DB, -1, KV_DIM).astype(BF16)
                    cv = cache_v[:, a].reshape(DB, -1, KV_DIM).astype(BF16)
                    att = _attention(qkv, attn_sink[a], ck, cv, DB, DSEQ, window=True)
                else:
                    kv = res[1]
                    new_k.append(kv[:, :KV_DIM].reshape(B, 1, SEQ, N_KV_HEADS, HEAD_DIM))
                    new_v.append(kv[:, KV_DIM:].reshape(B, 1, SEQ, N_KV_HEADS, HEAD_DIM))
                    ck = qkv[:, Q_DIM:Q_DIM + KV_DIM].reshape(B, SEQ, KV_DIM)
                    cv = qkv[:, Q_DIM + KV_DIM:].reshape(B, SEQ, KV_DIM)
                    att = _attention(qkv, attn_sink[a], ck, cv, B, SEQ, window=False)
                x = _proj_residual(att, wo, x, g1, bm)
            else:
                plan = _FftPlan(st["seq"])
                feat, _ = _filter_tables(st["seq"])
                hidden = _filter_hidden(jnp.asarray(feat), w1, hy_b_f1[hl][None, :], hy_w_f2[hl],
                                        hy_b_f2[hl][None, :], hy_freq[hl][None, :])
                hspec = _filter_spectrum(plan, hidden, hy_w_f3[hl], st["lanes"])
                x = _hyena_mixer(x, g_mix, sh1, sc1, g1, w_in, hy_conv[hl], hspec, hy_bias[hl],
                                 w_out, st["seq"], st["lanes"], bm)
            x = _conv_ffn(x, g_ffn, sh2, sc2, g2, w_up, ffn_conv[i], w_down, st["seq"], bm, 512,
                          g_final=g_final)
            st["x"] = x

    y_sample = streams[0]["x"]
    y_prompt = streams[1]["x"].reshape(B, SEQ, D)
    new_cache_k = new_k[0] if len(new_k) == 1 else jnp.concatenate(new_k, axis=1)
    new_cache_v = new_v[0] if len(new_v) == 1 else jnp.concatenate(new_v, axis=1)
    return (y_prompt, y_sample, new_cache_k, new_cache_v)
```

```python
import functools
import math

import jax
import jax.numpy as jnp
import numpy as np
from jax import lax
from jax.experimental import pallas as pl
from jax.experimental.pallas import tpu as pltpu

F32 = jnp.float32
BF16 = jnp.bfloat16

D_MODEL = 2048
HEAD_DIM = 128
N_HEADS = 16
N_KV_HEADS = 4
GROUP = N_HEADS // N_KV_HEADS
Q_DIM = N_HEADS * HEAD_DIM
KV_DIM = N_KV_HEADS * HEAD_DIM
QKV_DIM = Q_DIM + 2 * KV_DIM
GRID_W = 64
WINDOW = 128
ROPE_THETA = 10000.0
ROPE_PAIRS = HEAD_DIM // 4
HY_BANDS = 16
HY_EMB = 1 + 2 * HY_BANDS
HY_EMB_PAD = 40
HY_FH = 64
HY_MIN_DECAY = math.log(1e-2) / 1.5
HY_MAX_DECAY = math.log(1e-2) / 0.3
D_FF = 5632
EPS = 1e-6
NEG_INF = -1e30

VMEM_LIMIT = 56 * 1024 * 1024
NORM_ROWS = 16
HALO = 16
FFT_N2 = 256
FFT_N2_SINGLE = 512
FFT_ROWS = 32
FFT_FWD_ROWS = 16
SUB_COLS = 256
QKV_BN = 1024
ATTN_ROWS = 32
FFN_DOWN_COLS = 1024
TW_LANES = 128


def _cparams(sem):
    return pltpu.CompilerParams(dimension_semantics=sem, vmem_limit_bytes=VMEM_LIMIT)


def _store_modulated_norm(dst_ref, dst0, x_ref, n_rows, g_ref, sh_ref, sc_ref):
    gain = g_ref[...] * (1.0 + sc_ref[0])
    shift = sh_ref[0]
    for r in range(0, n_rows, NORM_ROWS):
        x = x_ref[0, r:r + NORM_ROWS, :]
        ms = jnp.mean(x * x, axis=-1, keepdims=True)
        y = x * lax.rsqrt(ms + EPS) * gain + shift
        dst_ref[dst0 + r:dst0 + r + NORM_ROWS, :] = y.astype(BF16)


def _mod_kernel(c_ref, w_ref, b_ref, o_ref):
    c = c_ref[...]
    s = c * (1.0 / (1.0 + jnp.exp(-c)))
    o_ref[0] = jnp.dot(s, w_ref[0], preferred_element_type=F32,
                       precision=lax.Precision.HIGHEST) + b_ref[0]


def _modulation(cvec, w_mod, b_mod):
    depth, d, n = w_mod.shape
    rows = cvec.shape[0]
    bn = 1024
    return pl.pallas_call(
        _mod_kernel,
        grid=(depth, n // bn),
        in_specs=[
            pl.BlockSpec((rows, d), lambda l, j: (0, 0)),
            pl.BlockSpec((1, d, bn), lambda l, j: (l, 0, j)),
            pl.BlockSpec((1, 1, bn), lambda l, j: (l, 0, j)),
        ],
        out_specs=pl.BlockSpec((1, rows, bn), lambda l, j: (l, 0, j)),
        out_shape=jax.ShapeDtypeStruct((depth, rows, n), F32),
        compiler_params=_cparams(("parallel", "parallel")),
        name="modulation",
    )(cvec, w_mod, b_mod.reshape(depth, 1, n))


def _qkv_kernel(*refs, rope, emit_kv):
    x_ref, g_ref, sh_ref, sc_ref, w_ref = refs[:5]
    pos = 5
    if rope:
        tabs = (refs[pos:pos + 3], refs[pos + 3:pos + 6])
        pos += 6
    o_ref = refs[pos]
    pos += 1
    if emit_kv:
        kv_ref = refs[pos]
        pos += 1
    h_scr = refs[pos]

    j = pl.program_id(2)
    n_q = Q_DIM // QKV_BN
    heads = QKV_BN // HEAD_DIM

    @pl.when(j == 0)
    def _():
        _store_modulated_norm(h_scr, 0, x_ref, h_scr.shape[0], g_ref, sh_ref, sc_ref)

    q_scale = jnp.where(j < n_q, HEAD_DIM ** -0.5, 1.0)
    for c0 in range(0, QKV_BN, SUB_COLS):
        cols = slice(c0, c0 + SUB_COLS)
        acc = jnp.dot(h_scr[...], w_ref[:, cols], preferred_element_type=F32)
        if emit_kv:
            kv_ref[:, cols] = acc
        for h in range(SUB_COLS // HEAD_DIM):
            xh = acc[:, h * HEAD_DIM:(h + 1) * HEAD_DIM]
            if rope:
                cos_ref, sa_ref, sb_ref = tabs[c0 // (QKV_BN // 2)]
                xh = (xh * cos_ref[...] + pltpu.roll(xh, HEAD_DIM - 32, axis=1) * sa_ref[...]
                      + pltpu.roll(xh, 32, axis=1) * sb_ref[...])
            else:
                xh = xh * q_scale
            o_ref[:, c0 + h * HEAD_DIM:c0 + (h + 1) * HEAD_DIM] = xh.astype(BF16)


def _qkv_proj(x, g, shift, scale, w, rope_tabs, emit_kv, bm):
    G, S, D = x.shape
    nb = S // bm
    bn = QKV_BN
    assert 2 * KV_DIM == bn and Q_DIM % bn == 0 and bn // HEAD_DIM == 2 * N_KV_HEADS
    nj = QKV_DIM // bn
    n_q = Q_DIM // bn
    rope = rope_tabs is not None
    in_specs = [
        pl.BlockSpec((1, bm, D), lambda gi, i, j: (gi, i, 0)),
        pl.BlockSpec((1, D), lambda gi, i, j: (0, 0)),
        pl.BlockSpec((1, 1, D), lambda gi, i, j: (gi, 0, 0)),
        pl.BlockSpec((1, 1, D), lambda gi, i, j: (gi, 0, 0)),
        pl.BlockSpec((D, bn), lambda gi, i, j: (0, j)),
    ]
    args = [x, g, shift, scale, w]
    if rope:
        for other in (1, 2):
            for t in rope_tabs:
                in_specs.append(pl.BlockSpec(
                    (None, bm, HEAD_DIM),
                    lambda gi, i, j, other=other: (jnp.where(j < n_q, 0, other), i, 0)))
                args.append(t)
    out_specs = [pl.BlockSpec((bm, bn), lambda gi, i, j: (gi * nb + i, j))]
    out_shape = [jax.ShapeDtypeStruct((G * S, QKV_DIM), BF16)]
    if emit_kv:
        out_specs.append(pl.BlockSpec((bm, bn), lambda gi, i, j: (gi * nb + i, 0)))
        out_shape.append(jax.ShapeDtypeStruct((G * S, 2 * KV_DIM), F32))
    res = pl.pallas_call(
        functools.partial(_qkv_kernel, rope=rope, emit_kv=emit_kv),
        grid=(G, nb, nj),
        in_specs=in_specs,
        out_specs=out_specs,
        out_shape=out_shape,
        scratch_shapes=[pltpu.VMEM((bm, D), BF16)],
        compiler_params=_cparams(("parallel", "parallel", "arbitrary")),
        name="qkv_proj",
    )(*args)
    return res


def _attn_kernel(*refs, window, bq):
    sink_ref, q_ref = refs[:2]
    pos = 2
    if window:
        kp_ref, kc_ref, kn_ref, vp_ref, vc_ref, vn_ref, bias_ref = refs[pos:pos + 7]
        pos += 7
    ck_ref, cv_ref, o_ref, k_scr, v_scr, s_scr, p_scr = refs[pos:pos + 7]

    n_ctx = ck_ref.shape[1]
    n_win = 3 * bq if window else 0
    tr = ATTN_ROWS

    for h in range(N_KV_HEADS):
        hs = slice(h * HEAD_DIM, (h + 1) * HEAD_DIM)
        if window:
            for t, (k_ref, v_ref) in enumerate(((kp_ref, vp_ref), (kc_ref, vc_ref), (kn_ref, vn_ref))):
                k_scr[h, t * bq:(t + 1) * bq] = k_ref[:, hs]
                v_scr[h, t * bq:(t + 1) * bq] = v_ref[:, hs]
        k_scr[h, n_win:] = ck_ref[0, :, hs]
        v_scr[h, n_win:] = cv_ref[0, :, hs]
        qs = jnp.concatenate(
            [q_ref[:, (h * GROUP + g) * HEAD_DIM:(h * GROUP + g + 1) * HEAD_DIM] for g in range(GROUP)],
            axis=0)
        s_scr[h] = lax.dot_general(qs, k_scr[h], (((1,), (1,)), ((), ())),
                                   preferred_element_type=F32)
    for h in range(N_KV_HEADS):
        for r0 in range(0, GROUP * bq, tr):
            sk = sink_ref[h * GROUP + r0 // bq]
            s = s_scr[h, r0:r0 + tr, :]
            if window:
                q0 = r0 % bq
                s = s + bias_ref[0, q0:q0 + tr, :]
            m = jnp.maximum(jnp.max(s, axis=-1, keepdims=True), sk)
            p = jnp.exp(s - m)
            denom = jnp.sum(p, axis=-1, keepdims=True) + jnp.exp(sk - m)
            p_scr[h, r0:r0 + tr, :] = (p * (1.0 / denom)).astype(BF16)
    for h in range(N_KV_HEADS):
        o = jnp.dot(p_scr[h], v_scr[h], preferred_element_type=F32)
        for g in range(GROUP):
            c0 = (h * GROUP + g) * HEAD_DIM
            o_ref[:, c0:c0 + HEAD_DIM] = o[g * bq:(g + 1) * bq].astype(BF16)


def _window_bias(n_ctx):
    i = np.arange(WINDOW)[:, None]
    j = np.arange(WINDOW)[None, :]
    prev = np.where(j >= i, 0.0, NEG_INF)
    cur = np.zeros((WINDOW, WINDOW))
    nxt = np.where(j <= i, 0.0, NEG_INF)
    dead = np.full((WINDOW, WINDOW), NEG_INF)
    ctx = np.zeros((WINDOW, n_ctx))
    first = np.concatenate([dead, cur, nxt, ctx], axis=1)
    mid = np.concatenate([prev, cur, nxt, ctx], axis=1)
    last = np.concatenate([prev, cur, dead, ctx], axis=1)
    return np.stack([first, mid, last]).astype(np.float32)


def _attention(qkv, sink, ctx_k, ctx_v, n_batch, seq, window):
    kcol = Q_DIM // KV_DIM
    vcol = kcol + 1
    n_ctx = ctx_k.shape[1]
    if window:
        bq = WINDOW
        nq = seq // bq
        in_specs = [
            pl.BlockSpec(memory_space=pltpu.SMEM),
            pl.BlockSpec((bq, Q_DIM), lambda b, n: (b * nq + n, 0)),
            pl.BlockSpec((bq, KV_DIM), lambda b, n: (b * nq + jnp.maximum(n - 1, 0), kcol)),
            pl.BlockSpec((bq, KV_DIM), lambda b, n: (b * nq + n, kcol)),
            pl.BlockSpec((bq, KV_DIM), lambda b, n: (b * nq + jnp.minimum(n + 1, nq - 1), kcol)),
            pl.BlockSpec((bq, KV_DIM), lambda b, n: (b * nq + jnp.maximum(n - 1, 0), vcol)),
            pl.BlockSpec((bq, KV_DIM), lambda b, n: (b * nq + n, vcol)),
            pl.BlockSpec((bq, KV_DIM), lambda b, n: (b * nq + jnp.minimum(n + 1, nq - 1), vcol)),
            pl.BlockSpec((1, bq, 3 * bq + n_ctx),
                         lambda b, n: (jnp.where(n == 0, 0, jnp.where(n == nq - 1, 2, 1)), 0, 0)),
            pl.BlockSpec((1, n_ctx, KV_DIM), lambda b, n: (b, 0, 0)),
            pl.BlockSpec((1, n_ctx, KV_DIM), lambda b, n: (b, 0, 0)),
        ]
        args = [sink, qkv, qkv, qkv, qkv, qkv, qkv, qkv, jnp.asarray(_window_bias(n_ctx)), ctx_k, ctx_v]
    else:
        bq = seq
        nq = 1
        in_specs = [
            pl.BlockSpec(memory_space=pltpu.SMEM),
            pl.BlockSpec((bq, Q_DIM), lambda b, n: (b, 0)),
            pl.BlockSpec((1, n_ctx, KV_DIM), lambda b, n: (b, 0, 0)),
            pl.BlockSpec((1, n_ctx, KV_DIM), lambda b, n: (b, 0, 0)),
        ]
        args = [sink, qkv, ctx_k, ctx_v]
    n_keys = (3 * bq if window else 0) + n_ctx
    return pl.pallas_call(
        functools.partial(_attn_kernel, window=window, bq=bq),
        grid=(n_batch, nq),
        in_specs=in_specs,
        out_specs=pl.BlockSpec((bq, Q_DIM), lambda b, n: (b * nq + n, 0)),
        out_shape=jax.ShapeDtypeStruct((n_batch * seq, Q_DIM), BF16),
        scratch_shapes=[
            pltpu.VMEM((N_KV_HEADS, n_keys, HEAD_DIM), BF16),
            pltpu.VMEM((N_KV_HEADS, n_keys, HEAD_DIM), BF16),
            pltpu.VMEM((N_KV_HEADS, GROUP * bq, n_keys), F32),
            pltpu.VMEM((N_KV_HEADS, GROUP * bq, n_keys), BF16),
        ],
        compiler_params=_cparams(("parallel", "parallel")),
        name="attention",
    )(*args)


def _proj_res_kernel(a_ref, w_ref, x_ref, gate_ref, o_ref):
    half_w = o_ref.shape[-1] // 2
    for half in range(2):
        cols = slice(half * half_w, (half + 1) * half_w)
        acc = jnp.dot(a_ref[...], w_ref[:, cols], preferred_element_type=F32)
        o_ref[0, :, cols] = x_ref[0, :, cols] + gate_ref[0, :, cols] * acc


def _proj_residual(a, w, x, gate, bm):
    G, S, D = x.shape
    K = a.shape[1]
    nb = S // bm
    bn = 1024
    return pl.pallas_call(
        _proj_res_kernel,
        grid=(G, nb, D // bn),
        in_specs=[
            pl.BlockSpec((bm, K), lambda gi, i, j: (gi * nb + i, 0)),
            pl.BlockSpec((K, bn), lambda gi, i, j: (0, j)),
            pl.BlockSpec((1, bm, bn), lambda gi, i, j: (gi, i, j)),
            pl.BlockSpec((1, 1, bn), lambda gi, i, j: (gi, 0, j)),
        ],
        out_specs=pl.BlockSpec((1, bm, bn), lambda gi, i, j: (gi, i, j)),
        out_shape=jax.ShapeDtypeStruct((G, S, D), F32),
        compiler_params=_cparams(("parallel", "parallel", "parallel")),
        name="proj_residual",
    )(a, w, x, gate)


def _fill_h_ext(h_scr, xm_ref, xp_ref, xn_ref, g_ref, sh_ref, sc_ref, bm):
    _store_modulated_norm(h_scr, 0, xp_ref, HALO, g_ref, sh_ref, sc_ref)
    _store_modulated_norm(h_scr, HALO, xm_ref, bm, g_ref, sh_ref, sc_ref)
    _store_modulated_norm(h_scr, HALO + bm, xn_ref, HALO, g_ref, sh_ref, sc_ref)


def _dwconv3_rows(u_ext, cw, row0, bm, seq_len):
    assert seq_len & (seq_len - 1) == 0
    pos = (row0 + lax.broadcasted_iota(jnp.int32, (bm, 1), 0)) & (seq_len - 1)
    prev = jnp.where(pos == 0, 0.0, u_ext[HALO - 1:HALO - 1 + bm])
    nxt = jnp.where(pos == seq_len - 1, 0.0, u_ext[HALO + 1:HALO + 1 + bm])
    return prev * cw[0:1] + u_ext[HALO:HALO + bm] * cw[1:2] + nxt * cw[2:3]


def _halo_specs(bm, D, S):
    nh = bm // HALO
    last = S // HALO - 1
    return [
        pl.BlockSpec((1, bm, D), lambda gi, i, j: (gi, i, 0)),
        pl.BlockSpec((1, HALO, D), lambda gi, i, j: (gi, jnp.maximum(i * nh - 1, 0), 0)),
        pl.BlockSpec((1, HALO, D), lambda gi, i, j: (gi, jnp.minimum((i + 1) * nh, last), 0)),
    ]


def _ffn_kernel(*refs, bm, seq_len, final_norm):
    (xm_ref, xp_ref, xn_ref, g_ref, sh_ref, sc_ref, gate_ref,
     wa_ref, wb_ref, cw_ref, wd_ref) = refs[:11]
    pos = 11
    if final_norm:
        gf_ref = refs[pos]
        pos += 1
    o_ref, h_scr = refs[pos:pos + 2]

    i = pl.program_id(1)
    j = pl.program_id(2)
    D = o_ref.shape[-1]

    @pl.when(j == 0)
    def _():
        _fill_h_ext(h_scr, xm_ref, xp_ref, xn_ref, g_ref, sh_ref, sc_ref, bm)
        o_ref[0] = jnp.zeros((bm, D), F32)

    ua = jnp.dot(h_scr[...], wa_ref[...], preferred_element_type=F32)
    ub = jnp.dot(h_scr[HALO:HALO + bm], wb_ref[...], preferred_element_type=F32)
    a = _dwconv3_rows(ua, cw_ref[...], i * bm, bm, seq_len)
    act = (a * (1.0 / (1.0 + jnp.exp(-a))) * ub).astype(BF16)
    for c0 in range(0, D, FFN_DOWN_COLS):
        cols = slice(c0, c0 + FFN_DOWN_COLS)
        o_ref[0, :, cols] += jnp.dot(act, wd_ref[:, cols], preferred_element_type=F32)

    @pl.when(j == pl.num_programs(2) - 1)
    def _():
        y = xm_ref[0] + gate_ref[0] * o_ref[0]
        if final_norm:
            ms = jnp.mean(y * y, axis=-1, keepdims=True)
            y = y * lax.rsqrt(ms + EPS) * gf_ref[...]
        o_ref[0] = y


def _conv_ffn(x, g, shift, scale, gate, w_up, conv_w, w_down, seq_len, bm, bf, g_final=None):
    G, S, D = x.shape
    nb = S // bm
    nc = D_FF // bf
    final_norm = g_final is not None
    in_specs = _halo_specs(bm, D, S) + [
        pl.BlockSpec((1, D), lambda gi, i, j: (0, 0)),
        pl.BlockSpec((1, 1, D), lambda gi, i, j: (gi, 0, 0)),
        pl.BlockSpec((1, 1, D), lambda gi, i, j: (gi, 0, 0)),
        pl.BlockSpec((1, 1, D), lambda gi, i, j: (gi, 0, 0)),
        pl.BlockSpec((D, bf), lambda gi, i, j: (0, j)),
        pl.BlockSpec((D, bf), lambda gi, i, j: (0, nc + j)),
        pl.BlockSpec((3, bf), lambda gi, i, j: (0, j)),
        pl.BlockSpec((bf, D), lambda gi, i, j: (j, 0)),
    ]
    args = [x, x, x, g, shift, scale, gate, w_up, w_up, conv_w, w_down]
    if final_norm:
        in_specs.append(pl.BlockSpec((1, D), lambda gi, i, j: (0, 0)))
        args.append(g_final)
    return pl.pallas_call(
        functools.partial(_ffn_kernel, bm=bm, seq_len=seq_len, final_norm=final_norm),
        grid=(G, nb, nc),
        in_specs=in_specs,
        out_specs=pl.BlockSpec((1, bm, D), lambda gi, i, j: (gi, i, 0), pipeline_mode=pl.Buffered(1)),
        out_shape=jax.ShapeDtypeStruct((G, S, D), F32),
        scratch_shapes=[pltpu.VMEM((bm + 2 * HALO, D), BF16)],
        compiler_params=_cparams(("parallel", "parallel", "arbitrary")),
        name="conv_ffn",
    )(*args)


def _hy_in_kernel(xm_ref, xp_ref, xn_ref, g_ref, sh_ref, sc_ref, w_ref, cw_ref, o_ref, h_scr,
                  *, bm, seq_len):
    i = pl.program_id(1)
    j = pl.program_id(2)

    @pl.when(j == 0)
    def _():
        _fill_h_ext(h_scr, xm_ref, xp_ref, xn_ref, g_ref, sh_ref, sc_ref, bm)

    for c0 in range(0, o_ref.shape[-1], SUB_COLS):
        cols = slice(c0, c0 + SUB_COLS)
        u = jnp.dot(h_scr[...], w_ref[:, cols], preferred_element_type=F32)
        o_ref[0, :, cols] = _dwconv3_rows(u, cw_ref[:, cols], i * bm, bm, seq_len)


def _hyena_in(x, g, shift, scale, w_in, conv_w, seq_len, bm, bn):
    G, S, D = x.shape
    nb = S // bm
    N = w_in.shape[1]
    in_specs = _halo_specs(bm, D, S) + [
        pl.BlockSpec((1, D), lambda gi, i, j: (0, 0)),
        pl.BlockSpec((1, 1, D), lambda gi, i, j: (gi, 0, 0)),
        pl.BlockSpec((1, 1, D), lambda gi, i, j: (gi, 0, 0)),
        pl.BlockSpec((D, bn), lambda gi, i, j: (0, j)),
        pl.BlockSpec((3, bn), lambda gi, i, j: (0, j)),
    ]
    return pl.pallas_call(
        functools.partial(_hy_in_kernel, bm=bm, seq_len=seq_len),
        grid=(G, nb, N // bn),
        in_specs=in_specs,
        out_specs=pl.BlockSpec((1, bm, bn), lambda gi, i, j: (gi, i, j)),
        out_shape=jax.ShapeDtypeStruct((G, S, N), F32),
        scratch_shapes=[pltpu.VMEM((bm + 2 * HALO, D), BF16)],
        compiler_params=_cparams(("parallel", "parallel", "arbitrary")),
        name="hyena_in",
    )(x, x, x, g, shift, scale, w_in, conv_w)


class _FftPlan:
    def __init__(self, seq_len):
        self.L = seq_len
        self.n2 = FFT_N2 if 2 * seq_len > FFT_N2_SINGLE else 2 * seq_len
        self.n1 = 2 * seq_len // self.n2
        self.n = self.n1 * self.n2
        self.classes = self.n1 // 2 + 1
        self.slab = min(self.n2, seq_len)
        self.data_slabs = seq_len // self.slab

    def stage1_coef(self, m, k1):
        th = 2.0 * math.pi * ((m * k1) % self.n1) / self.n1
        return _snap(math.cos(th)), _snap(-math.sin(th))

    def class_groups(self):
        half = self.n1 // 2
        groups = [(k, half - k) for k in range((half + 1) // 2)]
        if half % 2 == 0:
            groups.append((half // 2,))
        return groups

    def class_weight(self, k1):
        if self.n1 == 1:
            return 1.0
        return 1.0 if k1 in (0, self.n1 // 2) else 2.0


def _snap(v):
    for t in (0.0, 1.0, -1.0):
        if abs(v - t) < 1e-12:
            return t
    return v


@functools.lru_cache(maxsize=None)
def _dft_consts(n1, n2):
    n = n1 * n2
    k = np.arange(n2)
    ang = 2.0 * np.pi * np.outer(k, k) / n2
    fr, fi = np.cos(ang), -np.sin(ang)
    fwd = np.block([[fr, -fi], [fi, fr]]).astype(np.float32)
    inv = np.block([[fr, fi], [-fi, fr]]).astype(np.float32)
    classes = n1 // 2 + 1
    tw = 2.0 * np.pi * np.outer(np.arange(classes), np.arange(n2)) / n
    twc = np.repeat(np.cos(tw)[:, :, None], TW_LANES, axis=2).astype(np.float32)
    tws = np.repeat((-np.sin(tw))[:, :, None], TW_LANES, axis=2).astype(np.float32)
    return fwd, inv, twc, tws


def _const_spec(shape):
    nd = len(shape)
    return pl.BlockSpec(shape, lambda *_: (0,) * nd, pipeline_mode=pl.Buffered(1))


def _lane_tile(t, lanes):
    reps = lanes // t.shape[1]
    return t if reps == 1 else jnp.concatenate([t] * reps, axis=1)


def _scale(c, x):
    return x if c == 1.0 else (-x if c == -1.0 else c * x)


def _lin2(c1, x1, c2, x2):
    if x1 is None or c1 == 0.0:
        x1 = None
    if x2 is None or c2 == 0.0:
        x2 = None
    if x1 is None and x2 is None:
        return None
    if x2 is None:
        return _scale(c1, x1)
    if x1 is None:
        return _scale(c2, x2)
    if abs(abs(c1) - abs(c2)) < 1e-12:
        s = (x1 + x2) if (c1 > 0) == (c2 > 0) else (x1 - x2)
        return _scale(c1, s)
    return c1 * x1 + c2 * x2


def _rdft_half(xs):
    n = len(xs)
    if n == 1:
        return [(xs[0], None)]
    if n == 2:
        return [(_lin2(1.0, xs[0], 1.0, xs[1]), None), (_lin2(1.0, xs[0], -1.0, xs[1]), None)]
    half, quarter = n // 2, n // 4
    ev, od = _rdft_half(xs[0::2]), _rdft_half(xs[1::2])
    out = [None] * (half + 1)
    for j in range(quarter + 1):
        th = 2.0 * math.pi * j / n
        wr, wi = _snap(math.cos(th)), _snap(-math.sin(th))
        (er, ei), (orr, oi) = ev[j], od[j]
        tr = _lin2(wr, orr, -wi, oi)
        ti = _lin2(wi, orr, wr, oi)
        out[j] = (_lin2(1.0, er, 1.0, tr), _lin2(1.0, ei, 1.0, ti))
        if half - j != j:
            out[half - j] = (_lin2(1.0, er, -1.0, tr), _lin2(1.0, ti, -1.0, ei))
    return out


def _fft_forward_all(plan, n_slabs, src_ref, src_lead, twc_ref, tws_ref, abuf):
    n2, R = plan.n2, FFT_FWD_ROWS
    lanes = abuf.shape[-1]

    def body(r, carry):
        off = pl.multiple_of(r * R, R)
        for l0 in range(0, lanes, TW_LANES):
            ls = slice(l0, l0 + TW_LANES)
            xs = [src_ref[src_lead + (pl.ds(m * n2 + off, R), ls)] if m < n_slabs else None
                  for m in range(plan.n1)]
            for k1, (ar, ai) in enumerate(_rdft_half(xs)):
                if k1 == 0:
                    abuf[0, pl.ds(off, R), ls] = ar.astype(BF16)
                    continue
                tc = twc_ref[k1, pl.ds(off, R), :]
                ts = tws_ref[k1, pl.ds(off, R), :]
                abuf[k1, pl.ds(off, R), ls] = _lin2(1.0, ar * tc, -1.0, None if ai is None else ai * ts).astype(BF16)
                abuf[k1, pl.ds(n2 + off, R), ls] = _lin2(1.0, ar * ts, 1.0, None if ai is None else ai * tc).astype(BF16)
        return carry

    lax.fori_loop(0, n2 // R, body, 0)


def _fftconv_kernel(z_ref, gate_ref, bias_ref, h_ref, fwd_ref, inv_ref, twc_ref, tws_ref,
                    o_ref, abuf, ybuf, b_scr, acc_scr, *, plan):
    n2, L, R = plan.n2, plan.L, FFT_ROWS
    lanes = o_ref.shape[-1]
    inv_n = 1.0 / plan.n

    def dots(k1, slot):
        if k1 == 0:
            xf = jnp.dot(fwd_ref[:, 0:n2], abuf[0, 0:n2], preferred_element_type=F32)
        else:
            xf = jnp.dot(fwd_ref[...], abuf[k1], preferred_element_type=F32)
        xr, xi = xf[:n2], xf[n2:]
        hr, hi = h_ref[k1, 0:n2], h_ref[k1, n2:]
        ys = slot % ybuf.shape[0]
        ybuf[ys, 0:n2] = (xr * hr - xi * hi).astype(BF16)
        ybuf[ys, n2:] = (xr * hi + xi * hr).astype(BF16)
        if k1 == 0:
            b_scr[slot, 0:n2] = jnp.dot(inv_ref[0:n2, :], ybuf[ys], preferred_element_type=F32)
        else:
            b_scr[slot] = jnp.dot(inv_ref[...], ybuf[ys], preferred_element_type=F32)

    if plan.n1 == 1:
        hr, hi = h_ref[0, 0:n2], h_ref[0, n2:]
        for s in range(z_ref.shape[0]):
            z = z_ref[s]
            xf = jnp.dot(fwd_ref[:, 0:L], z.astype(BF16), preferred_element_type=F32)
            xr, xi = xf[:n2], xf[n2:]
            ybuf[s % 2, 0:n2] = (xr * hr - xi * hi).astype(BF16)
            ybuf[s % 2, n2:] = (xr * hi + xi * hr).astype(BF16)
            b = jnp.dot(inv_ref[0:L, :], ybuf[s % 2], preferred_element_type=F32)
            y = b * inv_n + bias_ref[...] * z
            o_ref[s] = (gate_ref[s] * y).astype(o_ref.dtype)
        return

    n_slots = b_scr.shape[0]

    def inverse(group, slots, first, last):
        wk = plan.class_weight(group[-1])
        for off in range(0, n2, R):
            vals = []
            for k1, slot in zip(group, slots):
                br = b_scr[slot, off:off + R, :]
                if k1 == 0:
                    vals.append((br, None))
                    continue
                bi = b_scr[slot, n2 + off:n2 + off + R, :]
                tc = _lane_tile(twc_ref[k1, off:off + R, :], lanes)
                ts = _lane_tile(tws_ref[k1, off:off + R, :], lanes)
                vals.append((br * tc + bi * ts, bi * tc - br * ts))
            if len(group) == 2:
                (ar, ai), (br_, bi_) = vals
                by_parity = [(_lin2(1.0, ar, 1.0, br_), _lin2(1.0, ai, -1.0, bi_)),
                             (_lin2(1.0, ar, -1.0, br_), _lin2(1.0, ai, 1.0, bi_))]
            else:
                by_parity = [vals[0], vals[0]]
            for m in range(plan.data_slabs):
                c, s = plan.stage1_coef(m, group[0])
                pr, pi = by_parity[m % 2]
                t = _lin2(wk * c, pr, wk * s, pi)
                rows = slice(m * n2 + off, m * n2 + off + R)
                if last:
                    z = z_ref[0, rows, :]
                    tot = t if first else acc_scr[rows, :] + t
                    y = tot * inv_n + bias_ref[...] * z
                    o_ref[0, rows, :] = (gate_ref[0, rows, :] * y).astype(o_ref.dtype)
                elif first:
                    acc_scr[rows, :] = t
                else:
                    acc_scr[rows, :] += t

    _fft_forward_all(plan, plan.data_slabs, z_ref, (0,), twc_ref, tws_ref, abuf)
    groups = plan.class_groups()
    issued = 0
    pending = None
    for gi, group in enumerate(groups):
        slots = []
        for k1 in group:
            slots.append(issued % n_slots)
            dots(k1, issued % n_slots)
            issued += 1
        if pending is not None:
            inverse(*pending, first=pending_first, last=False)
        pending, pending_first = (group, slots), gi == 0
    inverse(*pending, first=len(groups) == 1, last=True)


def _fftconv(plan, z, z_col, gate, gate_col, bias, hspec, n_seq, lanes, out_dtype):
    L, n2 = plan.L, plan.n2
    nc = D_MODEL // lanes
    fwd, inv, twc, tws = _dft_consts(plan.n1, n2)
    zc, gc = z_col // lanes, gate_col // lanes
    acc_rows = L if plan.n1 > 1 else 8
    sb = 1 if plan.n1 > 1 else min(n_seq, 4)
    return pl.pallas_call(
        functools.partial(_fftconv_kernel, plan=plan),
        grid=(nc, n_seq // sb),
        in_specs=[
            pl.BlockSpec((sb, L, lanes), lambda c, b: (b, 0, zc + c)),
            pl.BlockSpec((sb, L, lanes), lambda c, b: (b, 0, gc + c)),
            pl.BlockSpec((1, lanes), lambda c, b: (0, c)),
            pl.BlockSpec((plan.classes, 2 * n2, lanes), lambda c, b: (0, 0, c),
                         pipeline_mode=pl.Buffered(1)),
            _const_spec((2 * n2, 2 * n2)),
            _const_spec((2 * n2, 2 * n2)),
            _const_spec((plan.classes, n2, TW_LANES)),
            _const_spec((plan.classes, n2, TW_LANES)),
        ],
        out_specs=pl.BlockSpec((sb, L, lanes), lambda c, b: (b, 0, c)),
        out_shape=jax.ShapeDtypeStruct((n_seq, L, D_MODEL), out_dtype),
        scratch_shapes=[
            pltpu.VMEM((plan.classes, 2 * n2, lanes) if plan.n1 > 1 else (1, 16, lanes), BF16),
            pltpu.VMEM((2, 2 * n2, lanes), BF16),
            pltpu.VMEM((4, 2 * n2, lanes), F32),
            pltpu.VMEM((acc_rows, lanes), F32),
        ],
        compiler_params=_cparams(("parallel", "arbitrary")),
        name="hyena_fftconv",
    )(z, gate, bias, hspec, jnp.asarray(fwd).astype(BF16), jnp.asarray(inv).astype(BF16),
      jnp.asarray(twc), jnp.asarray(tws))


@functools.lru_cache(maxsize=None)
def _filter_tables(L):
    r = np.arange(2 * L)
    p = np.where(r < L, r, 2 * L - r)
    p[L] = 0
    t = p / (L - 1.0)
    w = 2.0 * np.pi * p / L
    f = np.linspace(1e-4, HY_BANDS - 1, HY_BANDS)
    feat = np.concatenate([t[:, None], np.cos(np.outer(w, f)), -np.sin(np.outer(w, f))], axis=1)
    feat = np.pad(feat, ((0, 0), (0, HY_EMB_PAD - HY_EMB))).astype(np.float32)
    t_tab = np.repeat(t[:, None], TW_LANES, axis=1).astype(np.float32)
    return feat, t_tab


def _filter_hidden_kernel(feat_ref, w1_ref, b1_ref, w2_ref, b2_ref, fq_ref, o_ref):
    hp = lax.Precision.HIGHEST
    h = jnp.sin(fq_ref[...] * (jnp.dot(feat_ref[...], w1_ref[...], preferred_element_type=F32,
                                        precision=hp) + b1_ref[...]))
    h = jnp.sin(fq_ref[...] * (jnp.dot(h, w2_ref[...], preferred_element_type=F32,
                                        precision=hp) + b2_ref[...]))
    o_ref[...] = h


def _filter_hidden(feat, w1, b1, w2, b2, freq):
    rows = feat.shape[0]
    br = 1024 if rows % 1024 == 0 else rows
    vec = pl.BlockSpec((1, HY_FH), lambda i: (0, 0))
    return pl.pallas_call(
        _filter_hidden_kernel,
        grid=(rows // br,),
        in_specs=[
            pl.BlockSpec((br, HY_EMB_PAD), lambda i: (i, 0)),
            pl.BlockSpec((HY_EMB_PAD, HY_FH), lambda i: (0, 0)),
            vec,
            pl.BlockSpec((HY_FH, HY_FH), lambda i: (0, 0)),
            vec, vec,
        ],
        out_specs=pl.BlockSpec((br, HY_FH), lambda i: (i, 0)),
        out_shape=jax.ShapeDtypeStruct((rows, HY_FH), F32),
        compiler_params=_cparams(("parallel",)),
        name="hyena_filter_hidden",
    )(feat, w1, b1, w2, b2, freq)


def _filter_spec_kernel(hid_ref, w3a_ref, w3b_ref, t_ref, delta_ref, fwd_ref, twc_ref, tws_ref,
                        o_ref, f_scr, abuf, *, plan):
    L, n2 = plan.L, plan.n2
    lanes = o_ref.shape[-1]
    hp = lax.Precision.HIGHEST
    absd = jnp.abs(delta_ref[...])
    rows_per = min(L, 512)

    def gen(r, carry):
        for base, w_ref in ((0, w3a_ref), (L, w3b_ref)):
            rows = pl.ds(pl.multiple_of(base + r * rows_per, rows_per), rows_per)
            f = jnp.dot(hid_ref[rows, :], w_ref[0], preferred_element_type=F32, precision=hp)
            f_scr[rows, :] = f * jnp.exp(-_lane_tile(t_ref[rows, :], lanes) * absd)
        return carry

    lax.fori_loop(0, L // rows_per, gen, 0)
    f_scr[L:L + 8] = jnp.where(lax.broadcasted_iota(jnp.int32, (8, 1), 0) > 0, f_scr[L:L + 8], 0.0)

    if plan.n1 == 1:
        abuf[0, 0:n2] = f_scr[...].astype(BF16)
    else:
        _fft_forward_all(plan, plan.n1, f_scr, (), twc_ref, tws_ref, abuf)
    o_ref[0, 0] = jnp.dot(fwd_ref[:, 0:n2], abuf[0, 0:n2], preferred_element_type=F32)
    for k1 in range(1, plan.classes):
        o_ref[0, k1] = jnp.dot(fwd_ref[...], abuf[k1], preferred_element_type=F32)


def _filter_spectrum(plan, hidden, w_f3, lanes):
    L, n2 = plan.L, plan.n2
    nc = D_MODEL // lanes
    fwd, _, twc, tws = _dft_consts(plan.n1, n2)
    _, t_tab = _filter_tables(L)
    delta = np.linspace(HY_MIN_DECAY, HY_MAX_DECAY, D_MODEL).astype(np.float32)[None, :]
    w3 = w_f3.reshape(HY_FH, 2, 2, D_MODEL).transpose(1, 2, 0, 3)
    return pl.pallas_call(
        functools.partial(_filter_spec_kernel, plan=plan),
        grid=(2, nc),
        in_specs=[
            _const_spec((2 * L, HY_FH)),
            pl.BlockSpec((None, 1, HY_FH, lanes), lambda o, c: (0, o, 0, c)),
            pl.BlockSpec((None, 1, HY_FH, lanes), lambda o, c: (1, o, 0, c)),
            _const_spec((2 * L, TW_LANES)),
            pl.BlockSpec((1, lanes), lambda o, c: (0, c)),
            _const_spec((2 * n2, 2 * n2)),
            _const_spec((plan.classes, n2, TW_LANES)),
            _const_spec((plan.classes, n2, TW_LANES)),
        ],
        out_specs=pl.BlockSpec((1, plan.classes, 2 * n2, lanes), lambda o, c: (o, 0, 0, c)),
        out_shape=jax.ShapeDtypeStruct((2, plan.classes, 2 * n2, D_MODEL), F32),
        scratch_shapes=[pltpu.VMEM((2 * L, lanes), F32),
                        pltpu.VMEM((plan.classes, 2 * n2, lanes), BF16)],
        compiler_params=_cparams(("parallel", "parallel")),
        name="hyena_filter_spectrum",
    )(hidden, w3, w3, jnp.asarray(t_tab), jnp.asarray(delta), jnp.asarray(fwd).astype(BF16),
      jnp.asarray(twc), jnp.asarray(tws))


@functools.lru_cache(maxsize=None)
def _rope_tables(L):
    t = np.arange(L)
    inv = ROPE_THETA ** (-np.arange(ROPE_PAIRS, dtype=np.float32) / ROPE_PAIRS)
    ang_row = (t // GRID_W).astype(np.float32)[:, None] * inv
    ang_col = (t % GRID_W).astype(np.float32)[:, None] * inv
    zero = np.zeros_like(ang_row)
    cos = np.concatenate([np.cos(ang_row)] * 2 + [np.cos(ang_col)] * 2, axis=1)
    sa = np.concatenate([-np.sin(ang_row), zero, -np.sin(ang_col), zero], axis=1)
    sb = np.concatenate([zero, np.sin(ang_row), zero, np.sin(ang_col)], axis=1)
    q_scale = HEAD_DIM ** -0.5
    variants = lambda t, ident: np.stack([t * q_scale, t, np.full_like(t, ident)]).astype(np.float32)
    return variants(cos, 1.0), variants(sa, 0.0), variants(sb, 0.0)


def _hyena_mixer(x, g, shift, scale, gate, w_in, conv_w, hspec, bias, w_out, seq_len, lanes, bm):
    G, S, D = x.shape
    plan = _FftPlan(seq_len)
    n_seq = G * S // seq_len
    u = _hyena_in(x, g, shift, scale, w_in, conv_w, seq_len, bm, 1024)
    u = u.reshape(n_seq, seq_len, 3 * D)
    z1 = _fftconv(plan, u, 2 * D, u, 0, bias[0:1], hspec[0], n_seq, lanes, F32)
    z2 = _fftconv(plan, z1, 0, u, D, bias[1:2], hspec[1], n_seq, lanes, BF16)
    return _proj_residual(z2.reshape(G * S, D), w_out, x, gate, bm)


def kernel(x_prompt, x_sample, cache_k, cache_v, c, c_ctx, w_mod, b_mod, norm_mix, norm_ffn, norm_final, w_qkv, w_o, attn_sink, hy_w_in, hy_conv, hy_w_f1, hy_b_f1, hy_w_f2, hy_b_f2, hy_w_f3, hy_freq, hy_bias, hy_w_out, ffn_w_up, ffn_conv, ffn_w_down):
    B, SEQ, D = x_prompt.shape
    DB, DSEQ, _ = x_sample.shape
    depth = w_mod.shape[0]

    n_cond = DB + 1
    cvec = jnp.concatenate([c, c_ctx[None, :], jnp.zeros((16 - n_cond, D), F32)], axis=0)
    mod = _modulation(cvec, w_mod, b_mod).reshape(depth, 16, 6, 1, D)

    xs = x_sample
    xp = x_prompt.reshape(1, B * SEQ, D)
    streams = [
        dict(x=xs, rows=slice(0, DB), seq=DSEQ, n_seq=DB, lanes=256),
        dict(x=xp, rows=slice(DB, DB + 1), seq=SEQ, n_seq=B, lanes=256),
    ]
    bm = 1024
    rope_tabs = tuple(jnp.asarray(t) for t in _rope_tables(DSEQ))
    new_k, new_v = [], []

    for i in range(depth):
        g_mix = norm_mix[i][None, :]
        g_ffn = norm_ffn[i][None, :]
        if i % 2 == 0:
            a = i // 2
            wq = w_qkv[a].astype(BF16)
            wo = w_o[a].astype(BF16)
        else:
            hl = i // 2
            w_in = hy_w_in[hl].astype(BF16)
            w_out = hy_w_out[hl].astype(BF16)
            w1 = jnp.pad(hy_w_f1[hl], ((0, HY_EMB_PAD - HY_EMB), (0, 0)))
        w_up = ffn_w_up[i].astype(BF16)
        w_down = ffn_w_down[i].astype(BF16)
        g_final = norm_final[None, :] if i == depth - 1 else None

        for si, st in enumerate(streams):
            x = st["x"]
            m = mod[i, st["rows"]]
            sh1, sc1, g1, sh2, sc2, g2 = (m[:, t] for t in range(6))
            latent = si == 0
            if i % 2 == 0:
                res = _qkv_proj(x, g_mix, sh1, sc1, wq, rope_tabs if latent else None,
                                emit_kv=not latent, bm=bm)
                qkv = res[0]
                if latent:
                    ck = cache_k[:, a].reshape(DB, -1, KV_DIM).astype(BF16)
                    cv = cache_v[:, a].reshape(DB, -1, KV_DIM).astype(BF16)
                    att = _attention(qkv, attn_sink[a], ck, cv, DB, DSEQ, window=True)
                else:
                    kv = res[1]
                    new_k.append(kv[:, :KV_DIM].reshape(B, 1, SEQ, N_KV_HEADS, HEAD_DIM))
                    new_v.append(kv[:, KV_DIM:].reshape(B, 1, SEQ, N_KV_HEADS, HEAD_DIM))
                    ck = qkv[:, Q_DIM:Q_DIM + KV_DIM].reshape(B, SEQ, KV_DIM)
                    cv = qkv[:, Q_DIM + KV_DIM:].reshape(B, SEQ, KV_DIM)
                    att = _attention(qkv, attn_sink[a], ck, cv, B, SEQ, window=False)
                x = _proj_residual(att, wo, x, g1, bm)
            else:
                plan = _FftPlan(st["seq"])
                feat, _ = _filter_tables(st["seq"])
                hidden = _filter_hidden(jnp.asarray(feat), w1, hy_b_f1[hl][None, :], hy_w_f2[hl],
                                        hy_b_f2[hl][None, :], hy_freq[hl][None, :])
                hspec = _filter_spectrum(plan, hidden, hy_w_f3[hl], st["lanes"])
                x = _hyena_mixer(x, g_mix, sh1, sc1, g1, w_in, hy_conv[hl], hspec, hy_bias[hl],
                                 w_out, st["seq"], st["lanes"], bm)
            x = _conv_ffn(x, g_ffn, sh2, sc2, g2, w_up, ffn_conv[i], w_down, st["seq"], bm, 512,
                          g_final=g_final)
            st["x"] = x

    y_sample = streams[0]["x"]
    y_prompt = streams[1]["x"].reshape(B, SEQ, D)
    new_cache_k = new_k[0] if len(new_k) == 1 else jnp.concatenate(new_k, axis=1)
    new_cache_v = new_v[0] if len(new_v) == 1 else jnp.concatenate(new_v, axis=1)
    return (y_prompt, y_sample, new_cache_k, new_cache_v)
```

```python
import functools
import math

import jax
import jax.numpy as jnp
import numpy as np
from jax import lax
from jax.experimental import pallas as pl
from jax.experimental.pallas import tpu as pltpu

F32 = jnp.float32
BF16 = jnp.bfloat16

D_MODEL = 2048
HEAD_DIM = 128
N_HEADS = 16
N_KV_HEADS = 4
GROUP = N_HEADS // N_KV_HEADS
Q_DIM = N_HEADS * HEAD_DIM
KV_DIM = N_KV_HEADS * HEAD_DIM
QKV_DIM = Q_DIM + 2 * KV_DIM
GRID_W = 64
WINDOW = 128
ROPE_THETA = 10000.0
ROPE_PAIRS = HEAD_DIM // 4
HY_BANDS = 16
HY_EMB = 1 + 2 * HY_BANDS
HY_EMB_PAD = 40
HY_FH = 64
HY_MIN_DECAY = math.log(1e-2) / 1.5
HY_MAX_DECAY = math.log(1e-2) / 0.3
D_FF = 5632
EPS = 1e-6
NEG_INF = -1e30

VMEM_LIMIT = 56 * 1024 * 1024
CAST_ROWS, CAST_COLS = 2048, 1024
NORM_ROWS = 16
HALO = 16
FFT_N2 = 256
FFT_N2_SINGLE = 512
FFT_ROWS = 32
FFT_FWD_ROWS = 16
SUB_COLS = 256
QKV_BN = 1024
ATTN_ROWS = 32
FFN_DOWN_COLS = 1024
TW_LANES = 128


def _cparams(sem):
    return pltpu.CompilerParams(dimension_semantics=sem, vmem_limit_bytes=VMEM_LIMIT)


def _store_modulated_norm(dst_ref, dst0, x_ref, n_rows, g_ref, sh_ref, sc_ref):
    gain = g_ref[...] * (1.0 + sc_ref[0])
    shift = sh_ref[0]
    for r in range(0, n_rows, NORM_ROWS):
        x = x_ref[0, r:r + NORM_ROWS, :]
        ms = jnp.mean(x * x, axis=-1, keepdims=True)
        y = x * lax.rsqrt(ms + EPS) * gain + shift
        dst_ref[dst0 + r:dst0 + r + NORM_ROWS, :] = y.astype(BF16)


def _cast_kernel(w_ref, o_ref):
    o_ref[...] = w_ref[...].astype(BF16)


def _weight_bf16(w, layer):
    _, rows, cols = w.shape
    br = CAST_ROWS if rows % CAST_ROWS == 0 else CAST_ROWS // 4
    bc = CAST_COLS
    return pl.pallas_call(
        _cast_kernel,
        grid=(rows // br, cols // bc),
        in_specs=[pl.BlockSpec((None, br, bc), lambda r, c: (layer, r, c))],
        out_specs=pl.BlockSpec((br, bc), lambda r, c: (r, c)),
        out_shape=jax.ShapeDtypeStruct((rows, cols), BF16),
        compiler_params=_cparams(("parallel", "parallel")),
        name="weight_cast",
    )(w)


def _mod_kernel(c_ref, w_ref, b_ref, o_ref):
    c = c_ref[...]
    s = c * (1.0 / (1.0 + jnp.exp(-c)))
    o_ref[0] = jnp.dot(s, w_ref[0], preferred_element_type=F32,
                       precision=lax.Precision.HIGHEST) + b_ref[0]


def _modulation(cvec, w_mod, b_mod):
    depth, d, n = w_mod.shape
    rows = cvec.shape[0]
    bn = 1024
    return pl.pallas_call(
        _mod_kernel,
        grid=(depth, n // bn),
        in_specs=[
            pl.BlockSpec((rows, d), lambda l, j: (0, 0)),
            pl.BlockSpec((1, d, bn), lambda l, j: (l, 0, j)),
            pl.BlockSpec((1, 1, bn), lambda l, j: (l, 0, j)),
        ],
        out_specs=pl.BlockSpec((1, rows, bn), lambda l, j: (l, 0, j)),
        out_shape=jax.ShapeDtypeStruct((depth, rows, n), F32),
        compiler_params=_cparams(("parallel", "parallel")),
        name="modulation",
    )(cvec, w_mod, b_mod.reshape(depth, 1, n))


def _qkv_kernel(*refs, rope, emit_kv):
    x_ref, g_ref, sh_ref, sc_ref, w_ref = refs[:5]
    pos = 5
    if rope:
        tabs = (refs[pos:pos + 3], refs[pos + 3:pos + 6])
        pos += 6
    o_ref = refs[pos]
    pos += 1
    if emit_kv:
        kv_ref = refs[pos]
        pos += 1
    h_scr = refs[pos]

    j = pl.program_id(2)
    n_q = Q_DIM // QKV_BN
    heads = QKV_BN // HEAD_DIM

    @pl.when(j == 0)
    def _():
        _store_modulated_norm(h_scr, 0, x_ref, h_scr.shape[0], g_ref, sh_ref, sc_ref)

    q_scale = jnp.where(j < n_q, HEAD_DIM ** -0.5, 1.0)
    for c0 in range(0, QKV_BN, SUB_COLS):
        cols = slice(c0, c0 + SUB_COLS)
        acc = jnp.dot(h_scr[...], w_ref[:, cols], preferred_element_type=F32)
        if emit_kv:
            kv_ref[:, cols] = acc
        for h in range(SUB_COLS // HEAD_DIM):
            xh = acc[:, h * HEAD_DIM:(h + 1) * HEAD_DIM]
            if rope:
                cos_ref, sa_ref, sb_ref = tabs[c0 // (QKV_BN // 2)]
                xh = (xh * cos_ref[...] + pltpu.roll(xh, HEAD_DIM - 32, axis=1) * sa_ref[...]
                      + pltpu.roll(xh, 32, axis=1) * sb_ref[...])
            else:
                xh = xh * q_scale
            o_ref[:, c0 + h * HEAD_DIM:c0 + (h + 1) * HEAD_DIM] = xh.astype(BF16)


def _qkv_proj(x, g, shift, scale, w, rope_tabs, emit_kv, bm):
    G, S, D = x.shape
    nb = S // bm
    bn = QKV_BN
    assert 2 * KV_DIM == bn and Q_DIM % bn == 0 and bn // HEAD_DIM == 2 * N_KV_HEADS
    nj = QKV_DIM // bn
    n_q = Q_DIM // bn
    rope = rope_tabs is not None
    in_specs = [
        pl.BlockSpec((1, bm, D), lambda gi, i, j: (gi, i, 0)),
        pl.BlockSpec((1, D), lambda gi, i, j: (0, 0)),
        pl.BlockSpec((1, 1, D), lambda gi, i, j: (gi, 0, 0)),
        pl.BlockSpec((1, 1, D), lambda gi, i, j: (gi, 0, 0)),
        pl.BlockSpec((D, bn), lambda gi, i, j: (0, j)),
    ]
    args = [x, g, shift, scale, w]
    if rope:
        for other in (1, 2):
            for t in rope_tabs:
                in_specs.append(pl.BlockSpec(
                    (None, bm, HEAD_DIM),
                    lambda gi, i, j, other=other: (jnp.where(j < n_q, 0, other), i, 0)))
                args.append(t)
    out_specs = [pl.BlockSpec((bm, bn), lambda gi, i, j: (gi * nb + i, j))]
    out_shape = [jax.ShapeDtypeStruct((G * S, QKV_DIM), BF16)]
    if emit_kv:
        out_specs.append(pl.BlockSpec((bm, bn), lambda gi, i, j: (gi * nb + i, 0)))
        out_shape.append(jax.ShapeDtypeStruct((G * S, 2 * KV_DIM), F32))
    res = pl.pallas_call(
        functools.partial(_qkv_kernel, rope=rope, emit_kv=emit_kv),
        grid=(G, nb, nj),
        in_specs=in_specs,
        out_specs=out_specs,
        out_shape=out_shape,
        scratch_shapes=[pltpu.VMEM((bm, D), BF16)],
        compiler_params=_cparams(("parallel", "parallel", "arbitrary")),
        name="qkv_proj",
    )(*args)
    return res


def _attn_kernel(*refs, window, bq):
    sink_ref, q_ref = refs[:2]
    pos = 2
    if window:
        kp_ref, kc_ref, kn_ref, vp_ref, vc_ref, vn_ref, bias_ref = refs[pos:pos + 7]
        pos += 7
    ck_ref, cv_ref, o_ref, k_scr, v_scr, s_scr, p_scr = refs[pos:pos + 7]

    n_ctx = ck_ref.shape[1]
    n_win = 3 * bq if window else 0
    tr = ATTN_ROWS

    for h in range(N_KV_HEADS):
        hs = slice(h * HEAD_DIM, (h + 1) * HEAD_DIM)
        if window:
            for t, (k_ref, v_ref) in enumerate(((kp_ref, vp_ref), (kc_ref, vc_ref), (kn_ref, vn_ref))):
                k_scr[h, t * bq:(t + 1) * bq] = k_ref[:, hs]
                v_scr[h, t * bq:(t + 1) * bq] = v_ref[:, hs]
        k_scr[h, n_win:] = ck_ref[0, :, hs]
        v_scr[h, n_win:] = cv_ref[0, :, hs]
        qs = jnp.concatenate(
            [q_ref[:, (h * GROUP + g) * HEAD_DIM:(h * GROUP + g + 1) * HEAD_DIM] for g in range(GROUP)],
            axis=0)
        s_scr[h] = lax.dot_general(qs, k_scr[h], (((1,), (1,)), ((), ())),
                                   preferred_element_type=F32)
    for h in range(N_KV_HEADS):
        for r0 in range(0, GROUP * bq, tr):
            sk = sink_ref[h * GROUP + r0 // bq]
            s = s_scr[h, r0:r0 + tr, :]
            if window:
                q0 = r0 % bq
                s = s + bias_ref[0, q0:q0 + tr, :]
            m = jnp.maximum(jnp.max(s, axis=-1, keepdims=True), sk)
            p = jnp.exp(s - m)
            denom = jnp.sum(p, axis=-1, keepdims=True) + jnp.exp(sk - m)
            p_scr[h, r0:r0 + tr, :] = (p * (1.0 / denom)).astype(BF16)
    for h in range(N_KV_HEADS):
        o = jnp.dot(p_scr[h], v_scr[h], preferred_element_type=F32)
        for g in range(GROUP):
            c0 = (h * GROUP + g) * HEAD_DIM
            o_ref[:, c0:c0 + HEAD_DIM] = o[g * bq:(g + 1) * bq].astype(BF16)


def _window_bias(n_ctx):
    i = np.arange(WINDOW)[:, None]
    j = np.arange(WINDOW)[None, :]
    prev = np.where(j >= i, 0.0, NEG_INF)
    cur = np.zeros((WINDOW, WINDOW))
    nxt = np.where(j <= i, 0.0, NEG_INF)
    dead = np.full((WINDOW, WINDOW), NEG_INF)
    ctx = np.zeros((WINDOW, n_ctx))
    first = np.concatenate([dead, cur, nxt, ctx], axis=1)
    mid = np.concatenate([prev, cur, nxt, ctx], axis=1)
    last = np.concatenate([prev, cur, dead, ctx], axis=1)
    return np.stack([first, mid, last]).astype(np.float32)


def _attention(qkv, sink, ctx_k, ctx_v, n_batch, seq, window):
    kcol = Q_DIM // KV_DIM
    vcol = kcol + 1
    n_ctx = ctx_k.shape[1]
    if window:
        bq = WINDOW
        nq = seq // bq
        in_specs = [
            pl.BlockSpec(memory_space=pltpu.SMEM),
            pl.BlockSpec((bq, Q_DIM), lambda b, n: (b * nq + n, 0)),
            pl.BlockSpec((bq, KV_DIM), lambda b, n: (b * nq + jnp.maximum(n - 1, 0), kcol)),
            pl.BlockSpec((bq, KV_DIM), lambda b, n: (b * nq + n, kcol)),
            pl.BlockSpec((bq, KV_DIM), lambda b, n: (b * nq + jnp.minimum(n + 1, nq - 1), kcol)),
            pl.BlockSpec((bq, KV_DIM), lambda b, n: (b * nq + jnp.maximum(n - 1, 0), vcol)),
            pl.BlockSpec((bq, KV_DIM), lambda b, n: (b * nq + n, vcol)),
            pl.BlockSpec((bq, KV_DIM), lambda b, n: (b * nq + jnp.minimum(n + 1, nq - 1), vcol)),
            pl.BlockSpec((1, bq, 3 * bq + n_ctx),
                         lambda b, n: (jnp.where(n == 0, 0, jnp.where(n == nq - 1, 2, 1)), 0, 0)),
            pl.BlockSpec((1, n_ctx, KV_DIM), lambda b, n: (b, 0, 0)),
            pl.BlockSpec((1, n_ctx, KV_DIM), lambda b, n: (b, 0, 0)),
        ]
        args = [sink, qkv, qkv, qkv, qkv, qkv, qkv, qkv, jnp.asarray(_window_bias(n_ctx)), ctx_k, ctx_v]
    else:
        bq = seq
        nq = 1
        in_specs = [
            pl.BlockSpec(memory_space=pltpu.SMEM),
            pl.BlockSpec((bq, Q_DIM), lambda b, n: (b, 0)),
            pl.BlockSpec((1, n_ctx, KV_DIM), lambda b, n: (b, 0, 0)),
            pl.BlockSpec((1, n_ctx, KV_DIM), lambda b, n: (b, 0, 0)),
        ]
        args = [sink, qkv, ctx_k, ctx_v]
    n_keys = (3 * bq if window else 0) + n_ctx
    return pl.pallas_call(
        functools.partial(_attn_kernel, window=window, bq=bq),
        grid=(n_batch, nq),
        in_specs=in_specs,
        out_specs=pl.BlockSpec((bq, Q_DIM), lambda b, n: (b * nq + n, 0)),
        out_shape=jax.ShapeDtypeStruct((n_batch * seq, Q_DIM), BF16),
        scratch_shapes=[
            pltpu.VMEM((N_KV_HEADS, n_keys, HEAD_DIM), BF16),
            pltpu.VMEM((N_KV_HEADS, n_keys, HEAD_DIM), BF16),
            pltpu.VMEM((N_KV_HEADS, GROUP * bq, n_keys), F32),
            pltpu.VMEM((N_KV_HEADS, GROUP * bq, n_keys), BF16),
        ],
        compiler_params=_cparams(("parallel", "parallel")),
        name="attention",
    )(*args)


def _proj_res_kernel(a_ref, w_ref, x_ref, gate_ref, o_ref):
    half_w = o_ref.shape[-1] // 2
    for half in range(2):
        cols = slice(half * half_w, (half + 1) * half_w)
        acc = jnp.dot(a_ref[...], w_ref[:, cols], preferred_element_type=F32)
        o_ref[0, :, cols] = x_ref[0, :, cols] + gate_ref[0, :, cols] * acc


def _proj_residual(a, w, x, gate, bm):
    G, S, D = x.shape
    K = a.shape[1]
    nb = S // bm
    bn = 1024
    return pl.pallas_call(
        _proj_res_kernel,
        grid=(G, nb, D // bn),
        in_specs=[
            pl.BlockSpec((bm, K), lambda gi, i, j: (gi * nb + i, 0)),
            pl.BlockSpec((K, bn), lambda gi, i, j: (0, j)),
            pl.BlockSpec((1, bm, bn), lambda gi, i, j: (gi, i, j)),
            pl.BlockSpec((1, 1, bn), lambda gi, i, j: (gi, 0, j)),
        ],
        out_specs=pl.BlockSpec((1, bm, bn), lambda gi, i, j: (gi, i, j)),
        out_shape=jax.ShapeDtypeStruct((G, S, D), F32),
        compiler_params=_cparams(("parallel", "parallel", "parallel")),
        name="proj_residual",
    )(a, w, x, gate)


def _fill_h_ext(h_scr, xm_ref, xp_ref, xn_ref, g_ref, sh_ref, sc_ref, bm):
    _store_modulated_norm(h_scr, 0, xp_ref, HALO, g_ref, sh_ref, sc_ref)
    _store_modulated_norm(h_scr, HALO, xm_ref, bm, g_ref, sh_ref, sc_ref)
    _store_modulated_norm(h_scr, HALO + bm, xn_ref, HALO, g_ref, sh_ref, sc_ref)


def _dwconv3_rows(u_ext, cw, row0, bm, seq_len):
    assert seq_len & (seq_len - 1) == 0
    pos = (row0 + lax.broadcasted_iota(jnp.int32, (bm, 1), 0)) & (seq_len - 1)
    prev = jnp.where(pos == 0, 0.0, u_ext[HALO - 1:HALO - 1 + bm])
    nxt = jnp.where(pos == seq_len - 1, 0.0, u_ext[HALO + 1:HALO + 1 + bm])
    return prev * cw[0:1] + u_ext[HALO:HALO + bm] * cw[1:2] + nxt * cw[2:3]


def _halo_specs(bm, D, S):
    nh = bm // HALO
    last = S // HALO - 1
    return [
        pl.BlockSpec((1, bm, D), lambda gi, i, j: (gi, i, 0)),
        pl.BlockSpec((1, HALO, D), lambda gi, i, j: (gi, jnp.maximum(i * nh - 1, 0), 0)),
        pl.BlockSpec((1, HALO, D), lambda gi, i, j: (gi, jnp.minimum((i + 1) * nh, last), 0)),
    ]


def _ffn_kernel(*refs, bm, seq_len, final_norm):
    (xm_ref, xp_ref, xn_ref, g_ref, sh_ref, sc_ref, gate_ref,
     wa_ref, wb_ref, cw_ref, wd_ref) = refs[:11]
    pos = 11
    if final_norm:
        gf_ref = refs[pos]
        pos += 1
    o_ref, h_scr = refs[pos:pos + 2]

    i = pl.program_id(1)
    j = pl.program_id(2)
    D = o_ref.shape[-1]

    @pl.when(j == 0)
    def _():
        _fill_h_ext(h_scr, xm_ref, xp_ref, xn_ref, g_ref, sh_ref, sc_ref, bm)
        o_ref[0] = jnp.zeros((bm, D), F32)

    ua = jnp.dot(h_scr[...], wa_ref[...], preferred_element_type=F32)
    ub = jnp.dot(h_scr[HALO:HALO + bm], wb_ref[...], preferred_element_type=F32)
    a = _dwconv3_rows(ua, cw_ref[...], i * bm, bm, seq_len)
    act = (a * (1.0 / (1.0 + jnp.exp(-a))) * ub).astype(BF16)
    for c0 in range(0, D, FFN_DOWN_COLS):
        cols = slice(c0, c0 + FFN_DOWN_COLS)
        o_ref[0, :, cols] += jnp.dot(act, wd_ref[:, cols], preferred_element_type=F32)

    @pl.when(j == pl.num_programs(2) - 1)
    def _():
        for r in range(0, bm, NORM_ROWS):
            rows = slice(r, r + NORM_ROWS)
            y = xm_ref[0, rows, :] + gate_ref[0] * o_ref[0, rows, :]
            if final_norm:
                ms = jnp.mean(y * y, axis=-1, keepdims=True)
                y = y * lax.rsqrt(ms + EPS) * gf_ref[...]
            o_ref[0, rows, :] = y


def _conv_ffn(x, g, shift, scale, gate, w_up, conv_w, w_down, seq_len, bm, bf, g_final=None):
    G, S, D = x.shape
    nb = S // bm
    nc = D_FF // bf
    final_norm = g_final is not None
    in_specs = _halo_specs(bm, D, S) + [
        pl.BlockSpec((1, D), lambda gi, i, j: (0, 0)),
        pl.BlockSpec((1, 1, D), lambda gi, i, j: (gi, 0, 0)),
        pl.BlockSpec((1, 1, D), lambda gi, i, j: (gi, 0, 0)),
        pl.BlockSpec((1, 1, D), lambda gi, i, j: (gi, 0, 0)),
        pl.BlockSpec((D, bf), lambda gi, i, j: (0, j)),
        pl.BlockSpec((D, bf), lambda gi, i, j: (0, nc + j)),
        pl.BlockSpec((3, bf), lambda gi, i, j: (0, j)),
        pl.BlockSpec((bf, D), lambda gi, i, j: (j, 0)),
    ]
    args = [x, x, x, g, shift, scale, gate, w_up, w_up, conv_w, w_down]
    if final_norm:
        in_specs.append(pl.BlockSpec((1, D), lambda gi, i, j: (0, 0)))
        args.append(g_final)
    return pl.pallas_call(
        functools.partial(_ffn_kernel, bm=bm, seq_len=seq_len, final_norm=final_norm),
        grid=(G, nb, nc),
        in_specs=in_specs,
        out_specs=pl.BlockSpec((1, bm, D), lambda gi, i, j: (gi, i, 0),
                               pipeline_mode=pl.Buffered(1 if final_norm else 2)),
        out_shape=jax.ShapeDtypeStruct((G, S, D), F32),
        scratch_shapes=[pltpu.VMEM((bm + 2 * HALO, D), BF16)],
        compiler_params=_cparams(("parallel", "parallel", "arbitrary")),
        name="conv_ffn",
    )(*args)


def _hy_in_kernel(xm_ref, xp_ref, xn_ref, g_ref, sh_ref, sc_ref, w_ref, cw_ref, o_ref, h_scr,
                  *, bm, seq_len):
    i = pl.program_id(1)
    j = pl.program_id(2)

    @pl.when(j == 0)
    def _():
        _fill_h_ext(h_scr, xm_ref, xp_ref, xn_ref, g_ref, sh_ref, sc_ref, bm)

    for c0 in range(0, o_ref.shape[-1], SUB_COLS):
        cols = slice(c0, c0 + SUB_COLS)
        u = jnp.dot(h_scr[...], w_ref[:, cols], preferred_element_type=F32)
        o_ref[0, :, cols] = _dwconv3_rows(u, cw_ref[:, cols], i * bm, bm, seq_len)


def _hyena_in(x, g, shift, scale, w_in, conv_w, seq_len, bm, bn):
    G, S, D = x.shape
    nb = S // bm
    N = w_in.shape[1]
    in_specs = _halo_specs(bm, D, S) + [
        pl.BlockSpec((1, D), lambda gi, i, j: (0, 0)),
        pl.BlockSpec((1, 1, D), lambda gi, i, j: (gi, 0, 0)),
        pl.BlockSpec((1, 1, D), lambda gi, i, j: (gi, 0, 0)),
        pl.BlockSpec((D, bn), lambda gi, i, j: (0, j)),
        pl.BlockSpec((3, bn), lambda gi, i, j: (0, j)),
    ]
    return pl.pallas_call(
        functools.partial(_hy_in_kernel, bm=bm, seq_len=seq_len),
        grid=(G, nb, N // bn),
        in_specs=in_specs,
        out_specs=pl.BlockSpec((1, bm, bn), lambda gi, i, j: (gi, i, j)),
        out_shape=jax.ShapeDtypeStruct((G, S, N), F32),
        scratch_shapes=[pltpu.VMEM((bm + 2 * HALO, D), BF16)],
        compiler_params=_cparams(("parallel", "parallel", "arbitrary")),
        name="hyena_in",
    )(x, x, x, g, shift, scale, w_in, conv_w)


class _FftPlan:
    def __init__(self, seq_len):
        self.L = seq_len
        self.n2 = FFT_N2 if 2 * seq_len > FFT_N2_SINGLE else 2 * seq_len
        self.n1 = 2 * seq_len // self.n2
        self.n = self.n1 * self.n2
        self.classes = self.n1 // 2 + 1
        self.slab = min(self.n2, seq_len)
        self.data_slabs = seq_len // self.slab

    def stage1_coef(self, m, k1):
        th = 2.0 * math.pi * ((m * k1) % self.n1) / self.n1
        return _snap(math.cos(th)), _snap(-math.sin(th))

    def class_groups(self):
        half = self.n1 // 2
        groups = [(k, half - k) for k in range((half + 1) // 2)]
        if half % 2 == 0:
            groups.append((half // 2,))
        return groups

    def class_weight(self, k1):
        if self.n1 == 1:
            return 1.0
        return 1.0 if k1 in (0, self.n1 // 2) else 2.0


def _snap(v):
    for t in (0.0, 1.0, -1.0):
        if abs(v - t) < 1e-12:
            return t
    return v


@functools.lru_cache(maxsize=None)
def _dft_consts(n1, n2):
    n = n1 * n2
    k = np.arange(n2)
    ang = 2.0 * np.pi * np.outer(k, k) / n2
    fr, fi = np.cos(ang), -np.sin(ang)
    fwd = np.block([[fr, -fi], [fi, fr]]).astype(np.float32)
    inv = np.block([[fr, fi], [-fi, fr]]).astype(np.float32)
    classes = n1 // 2 + 1
    tw = 2.0 * np.pi * np.outer(np.arange(classes), np.arange(n2)) / n
    twc = np.repeat(np.cos(tw)[:, :, None], TW_LANES, axis=2).astype(np.float32)
    tws = np.repeat((-np.sin(tw))[:, :, None], TW_LANES, axis=2).astype(np.float32)
    return fwd, inv, twc, tws


def _const_spec(shape):
    nd = len(shape)
    return pl.BlockSpec(shape, lambda *_: (0,) * nd, pipeline_mode=pl.Buffered(1))


def _lane_tile(t, lanes):
    reps = lanes // t.shape[1]
    return t if reps == 1 else jnp.concatenate([t] * reps, axis=1)


def _scale(c, x):
    return x if c == 1.0 else (-x if c == -1.0 else c * x)


def _lin2(c1, x1, c2, x2):
    if x1 is None or c1 == 0.0:
        x1 = None
    if x2 is None or c2 == 0.0:
        x2 = None
    if x1 is None and x2 is None:
        return None
    if x2 is None:
        return _scale(c1, x1)
    if x1 is None:
        return _scale(c2, x2)
    if abs(abs(c1) - abs(c2)) < 1e-12:
        s = (x1 + x2) if (c1 > 0) == (c2 > 0) else (x1 - x2)
        return _scale(c1, s)
    return c1 * x1 + c2 * x2


def _rdft_half(xs):
    n = len(xs)
    if n == 1:
        return [(xs[0], None)]
    if n == 2:
        return [(_lin2(1.0, xs[0], 1.0, xs[1]), None), (_lin2(1.0, xs[0], -1.0, xs[1]), None)]
    half, quarter = n // 2, n // 4
    ev, od = _rdft_half(xs[0::2]), _rdft_half(xs[1::2])
    out = [None] * (half + 1)
    for j in range(quarter + 1):
        th = 2.0 * math.pi * j / n
        wr, wi = _snap(math.cos(th)), _snap(-math.sin(th))
        (er, ei), (orr, oi) = ev[j], od[j]
        tr = _lin2(wr, orr, -wi, oi)
        ti = _lin2(wi, orr, wr, oi)
        out[j] = (_lin2(1.0, er, 1.0, tr), _lin2(1.0, ei, 1.0, ti))
        if half - j != j:
            out[half - j] = (_lin2(1.0, er, -1.0, tr), _lin2(1.0, ti, -1.0, ei))
    return out


def _fft_forward_all(plan, n_slabs, src_ref, src_lead, twc_ref, tws_ref, abuf):
    n2, R = plan.n2, FFT_FWD_ROWS
    lanes = abuf.shape[-1]

    def body(r, carry):
        off = pl.multiple_of(r * R, R)
        for l0 in range(0, lanes, TW_LANES):
            ls = slice(l0, l0 + TW_LANES)
            xs = [src_ref[src_lead + (pl.ds(m * n2 + off, R), ls)] if m < n_slabs else None
                  for m in range(plan.n1)]
            for k1, (ar, ai) in enumerate(_rdft_half(xs)):
                if k1 == 0:
                    abuf[0, pl.ds(off, R), ls] = ar.astype(BF16)
                    continue
                tc = twc_ref[k1, pl.ds(off, R), :]
                ts = tws_ref[k1, pl.ds(off, R), :]
                abuf[k1, pl.ds(off, R), ls] = _lin2(1.0, ar * tc, -1.0, None if ai is None else ai * ts).astype(BF16)
                abuf[k1, pl.ds(n2 + off, R), ls] = _lin2(1.0, ar * ts, 1.0, None if ai is None else ai * tc).astype(BF16)
        return carry

    lax.fori_loop(0, n2 // R, body, 0)


def _fftconv_kernel(z_ref, gate_ref, bias_ref, h_ref, fwd_ref, inv_ref, twc_ref, tws_ref,
                    o_ref, abuf, ybuf, b_scr, acc_scr, *, plan):
    n2, L, R = plan.n2, plan.L, FFT_ROWS
    lanes = o_ref.shape[-1]
    inv_n = 1.0 / plan.n

    def dots(k1, slot):
        if k1 == 0:
            xf = jnp.dot(fwd_ref[:, 0:n2], abuf[0, 0:n2], preferred_element_type=F32)
        else:
            xf = jnp.dot(fwd_ref[...], abuf[k1], preferred_element_type=F32)
        xr, xi = xf[:n2], xf[n2:]
        hr, hi = h_ref[k1, 0:n2], h_ref[k1, n2:]
        ys = slot % ybuf.shape[0]
        ybuf[ys, 0:n2] = (xr * hr - xi * hi).astype(BF16)
        ybuf[ys, n2:] = (xr * hi + xi * hr).astype(BF16)
        if k1 == 0:
            b_scr[slot, 0:n2] = jnp.dot(inv_ref[0:n2, :], ybuf[ys], preferred_element_type=F32)
        else:
            b_scr[slot] = jnp.dot(inv_ref[...], ybuf[ys], preferred_element_type=F32)

    if plan.n1 == 1:
        hr, hi = h_ref[0, 0:n2], h_ref[0, n2:]
        for s in range(z_ref.shape[0]):
            z = z_ref[s]
            xf = jnp.dot(fwd_ref[:, 0:L], z.astype(BF16), preferred_element_type=F32)
            xr, xi = xf[:n2], xf[n2:]
            ybuf[s % 2, 0:n2] = (xr * hr - xi * hi).astype(BF16)
            ybuf[s % 2, n2:] = (xr * hi + xi * hr).astype(BF16)
            b = jnp.dot(inv_ref[0:L, :], ybuf[s % 2], preferred_element_type=F32)
            y = b * inv_n + bias_ref[...] * z
            o_ref[s] = (gate_ref[s] * y).astype(o_ref.dtype)
        return

    n_slots = b_scr.shape[0]

    def inverse(group, slots, first, last):
        wk = plan.class_weight(group[-1])
        for off in range(0, n2, R):
            vals = []
            for k1, slot in zip(group, slots):
                br = b_scr[slot, off:off + R, :]
                if k1 == 0:
                    vals.append((br, None))
                    continue
                bi = b_scr[slot, n2 + off:n2 + off + R, :]
                tc = _lane_tile(twc_ref[k1, off:off + R, :], lanes)
                ts = _lane_tile(tws_ref[k1, off:off + R, :], lanes)
                vals.append((br * tc + bi * ts, bi * tc - br * ts))
            if len(group) == 2:
                (ar, ai), (br_, bi_) = vals
                by_parity = [(_lin2(1.0, ar, 1.0, br_), _lin2(1.0, ai, -1.0, bi_)),
                             (_lin2(1.0, ar, -1.0, br_), _lin2(1.0, ai, 1.0, bi_))]
            else:
                by_parity = [vals[0], vals[0]]
            for m in range(plan.data_slabs):
                c, s = plan.stage1_coef(m, group[0])
                pr, pi = by_parity[m % 2]
                t = _lin2(wk * c, pr, wk * s, pi)
                rows = slice(m * n2 + off, m * n2 + off + R)
                if last:
                    z = z_ref[0, rows, :]
                    tot = t if first else acc_scr[rows, :] + t
                    y = tot * inv_n + bias_ref[...] * z
                    o_ref[0, rows, :] = (gate_ref[0, rows, :] * y).astype(o_ref.dtype)
                elif first:
                    acc_scr[rows, :] = t
                else:
                    acc_scr[rows, :] += t

    _fft_forward_all(plan, plan.data_slabs, z_ref, (0,), twc_ref, tws_ref, abuf)
    groups = plan.class_groups()
    issued = 0
    pending = None
    for gi, group in enumerate(groups):
        slots = []
        for k1 in group:
            slots.append(issued % n_slots)
            dots(k1, issued % n_slots)
            issued += 1
        if pending is not None:
            inverse(*pending, first=pending_first, last=False)
        pending, pending_first = (group, slots), gi == 0
    inverse(*pending, first=len(groups) == 1, last=True)


def _fftconv(plan, z, z_col, gate, gate_col, bias, hspec, n_seq, lanes, out_dtype):
    L, n2 = plan.L, plan.n2
    nc = D_MODEL // lanes
    fwd, inv, twc, tws = _dft_consts(plan.n1, n2)
    zc, gc = z_col // lanes, gate_col // lanes
    acc_rows = L if plan.n1 > 1 else 8
    sb = 1 if plan.n1 > 1 else min(n_seq, 4)
    return pl.pallas_call(
        functools.partial(_fftconv_kernel, plan=plan),
        grid=(nc, n_seq // sb),
        in_specs=[
            pl.BlockSpec((sb, L, lanes), lambda c, b: (b, 0, zc + c)),
            pl.BlockSpec((sb, L, lanes), lambda c, b: (b, 0, gc + c)),
            pl.BlockSpec((1, lanes), lambda c, b: (0, c)),
            pl.BlockSpec((plan.classes, 2 * n2, lanes), lambda c, b: (0, 0, c),
                         pipeline_mode=pl.Buffered(1)),
            _const_spec((2 * n2, 2 * n2)),
            _const_spec((2 * n2, 2 * n2)),
            _const_spec((plan.classes, n2, TW_LANES)),
            _const_spec((plan.classes, n2, TW_LANES)),
        ],
        out_specs=pl.BlockSpec((sb, L, lanes), lambda c, b: (b, 0, c)),
        out_shape=jax.ShapeDtypeStruct((n_seq, L, D_MODEL), out_dtype),
        scratch_shapes=[
            pltpu.VMEM((plan.classes, 2 * n2, lanes) if plan.n1 > 1 else (1, 16, lanes), BF16),
            pltpu.VMEM((2, 2 * n2, lanes), BF16),
            pltpu.VMEM((4, 2 * n2, lanes), F32),
            pltpu.VMEM((acc_rows, lanes), F32),
        ],
        compiler_params=_cparams(("parallel", "arbitrary")),
        name="hyena_fftconv",
    )(z, gate, bias, hspec, jnp.asarray(fwd).astype(BF16), jnp.asarray(inv).astype(BF16),
      jnp.asarray(twc), jnp.asarray(tws))


@functools.lru_cache(maxsize=None)
def _filter_tables(L):
    r = np.arange(2 * L)
    p = np.where(r < L, r, 2 * L - r)
    p[L] = 0
    t = p / (L - 1.0)
    w = 2.0 * np.pi * p / L
    f = np.linspace(1e-4, HY_BANDS - 1, HY_BANDS)
    feat = np.concatenate([t[:, None], np.cos(np.outer(w, f)), -np.sin(np.outer(w, f))], axis=1)
    feat = np.pad(feat, ((0, 0), (0, HY_EMB_PAD - HY_EMB))).astype(np.float32)
    t_tab = np.repeat(t[:, None], TW_LANES, axis=1).astype(np.float32)
    return feat, t_tab


def _filter_hidden_kernel(feat_ref, w1_ref, b1_ref, w2_ref, b2_ref, fq_ref, o_ref):
    hp = lax.Precision.HIGHEST
    h = jnp.sin(fq_ref[...] * (jnp.dot(feat_ref[...], w1_ref[...], preferred_element_type=F32,
                                        precision=hp) + b1_ref[...]))
    h = jnp.sin(fq_ref[...] * (jnp.dot(h, w2_ref[...], preferred_element_type=F32,
                                        precision=hp) + b2_ref[...]))
    o_ref[...] = h


def _filter_hidden(feat, w1, b1, w2, b2, freq):
    rows = feat.shape[0]
    br = 1024 if rows % 1024 == 0 else rows
    vec = pl.BlockSpec((1, HY_FH), lambda i: (0, 0))
    return pl.pallas_call(
        _filter_hidden_kernel,
        grid=(rows // br,),
        in_specs=[
            pl.BlockSpec((br, HY_EMB_PAD), lambda i: (i, 0)),
            pl.BlockSpec((HY_EMB_PAD, HY_FH), lambda i: (0, 0)),
            vec,
            pl.BlockSpec((HY_FH, HY_FH), lambda i: (0, 0)),
            vec, vec,
        ],
        out_specs=pl.BlockSpec((br, HY_FH), lambda i: (i, 0)),
        out_shape=jax.ShapeDtypeStruct((rows, HY_FH), F32),
        compiler_params=_cparams(("parallel",)),
        name="hyena_filter_hidden",
    )(feat, w1, b1, w2, b2, freq)


def _filter_spec_kernel(hid_ref, w3a_ref, w3b_ref, t_ref, delta_ref, fwd_ref, twc_ref, tws_ref,
                        o_ref, f_scr, abuf, *, plan):
    L, n2 = plan.L, plan.n2
    lanes = o_ref.shape[-1]
    hp = lax.Precision.HIGHEST
    absd = jnp.abs(delta_ref[...])
    rows_per = min(L, 512)

    def gen(r, carry):
        for base, w_ref in ((0, w3a_ref), (L, w3b_ref)):
            rows = pl.ds(pl.multiple_of(base + r * rows_per, rows_per), rows_per)
            f = jnp.dot(hid_ref[rows, :], w_ref[0], preferred_element_type=F32, precision=hp)
            f_scr[rows, :] = f * jnp.exp(-_lane_tile(t_ref[rows, :], lanes) * absd)
        return carry

    lax.fori_loop(0, L // rows_per, gen, 0)
    f_scr[L:L + 8] = jnp.where(lax.broadcasted_iota(jnp.int32, (8, 1), 0) > 0, f_scr[L:L + 8], 0.0)

    if plan.n1 == 1:
        abuf[0, 0:n2] = f_scr[...].astype(BF16)
    else:
        _fft_forward_all(plan, plan.n1, f_scr, (), twc_ref, tws_ref, abuf)
    o_ref[0, 0] = jnp.dot(fwd_ref[:, 0:n2], abuf[0, 0:n2], preferred_element_type=F32)
    for k1 in range(1, plan.classes):
        o_ref[0, k1] = jnp.dot(fwd_ref[...], abuf[k1], preferred_element_type=F32)


def _filter_spectrum(plan, hidden, w_f3, lanes):
    L, n2 = plan.L, plan.n2
    nc = D_MODEL // lanes
    fwd, _, twc, tws = _dft_consts(plan.n1, n2)
    _, t_tab = _filter_tables(L)
    delta = np.linspace(HY_MIN_DECAY, HY_MAX_DECAY, D_MODEL).astype(np.float32)[None, :]
    w3 = w_f3.reshape(HY_FH, 2, 2, D_MODEL).transpose(1, 2, 0, 3)
    return pl.pallas_call(
        functools.partial(_filter_spec_kernel, plan=plan),
        grid=(2, nc),
        in_specs=[
            _const_spec((2 * L, HY_FH)),
            pl.BlockSpec((None, 1, HY_FH, lanes), lambda o, c: (0, o, 0, c)),
            pl.BlockSpec((None, 1, HY_FH, lanes), lambda o, c: (1, o, 0, c)),
            _const_spec((2 * L, TW_LANES)),
            pl.BlockSpec((1, lanes), lambda o, c: (0, c)),
            _const_spec((2 * n2, 2 * n2)),
            _const_spec((plan.classes, n2, TW_LANES)),
            _const_spec((plan.classes, n2, TW_LANES)),
        ],
        out_specs=pl.BlockSpec((1, plan.classes, 2 * n2, lanes), lambda o, c: (o, 0, 0, c)),
        out_shape=jax.ShapeDtypeStruct((2, plan.classes, 2 * n2, D_MODEL), F32),
        scratch_shapes=[pltpu.VMEM((2 * L, lanes), F32),
                        pltpu.VMEM((plan.classes, 2 * n2, lanes), BF16)],
        compiler_params=_cparams(("parallel", "parallel")),
        name="hyena_filter_spectrum",
    )(hidden, w3, w3, jnp.asarray(t_tab), jnp.asarray(delta), jnp.asarray(fwd).astype(BF16),
      jnp.asarray(twc), jnp.asarray(tws))


@functools.lru_cache(maxsize=None)
def _rope_tables(L):
    t = np.arange(L)
    inv = ROPE_THETA ** (-np.arange(ROPE_PAIRS, dtype=np.float32) / ROPE_PAIRS)
    ang_row = (t // GRID_W).astype(np.float32)[:, None] * inv
    ang_col = (t % GRID_W).astype(np.float32)[:, None] * inv
    zero = np.zeros_like(ang_row)
    cos = np.concatenate([np.cos(ang_row)] * 2 + [np.cos(ang_col)] * 2, axis=1)
    sa = np.concatenate([-np.sin(ang_row), zero, -np.sin(ang_col), zero], axis=1)
    sb = np.concatenate([zero, np.sin(ang_row), zero, np.sin(ang_col)], axis=1)
    q_scale = HEAD_DIM ** -0.5
    variants = lambda t, ident: np.stack([t * q_scale, t, np.full_like(t, ident)]).astype(np.float32)
    return variants(cos, 1.0), variants(sa, 0.0), variants(sb, 0.0)


def _hyena_mixer(x, g, shift, scale, gate, w_in, conv_w, hspec, bias, w_out, seq_len, lanes, bm):
    G, S, D = x.shape
    plan = _FftPlan(seq_len)
    n_seq = G * S // seq_len
    u = _hyena_in(x, g, shift, scale, w_in, conv_w, seq_len, bm, 1024)
    u = u.reshape(n_seq, seq_len, 3 * D)
    z1 = _fftconv(plan, u, 2 * D, u, 0, bias[0:1], hspec[0], n_seq, lanes, F32)
    z2 = _fftconv(plan, z1, 0, u, D, bias[1:2], hspec[1], n_seq, lanes, BF16)
    return _proj_residual(z2.reshape(G * S, D), w_out, x, gate, bm)


def kernel(x_prompt, x_sample, cache_k, cache_v, c, c_ctx, w_mod, b_mod, norm_mix, norm_ffn, norm_final, w_qkv, w_o, attn_sink, hy_w_in, hy_conv, hy_w_f1, hy_b_f1, hy_w_f2, hy_b_f2, hy_w_f3, hy_freq, hy_bias, hy_w_out, ffn_w_up, ffn_conv, ffn_w_down):
    B, SEQ, D = x_prompt.shape
    DB, DSEQ, _ = x_sample.shape
    depth = w_mod.shape[0]

    n_cond = DB + 1
    cvec = jnp.concatenate([c, c_ctx[None, :], jnp.zeros((16 - n_cond, D), F32)], axis=0)
    mod = _modulation(cvec, w_mod, b_mod).reshape(depth, 16, 6, 1, D)

    xs = x_sample
    xp = x_prompt.reshape(1, B * SEQ, D)
    streams = [
        dict(x=xs, rows=slice(0, DB), seq=DSEQ, n_seq=DB, lanes=256),
        dict(x=xp, rows=slice(DB, DB + 1), seq=SEQ, n_seq=B, lanes=256),
    ]
    bm = 1024
    rope_tabs = tuple(jnp.asarray(t) for t in _rope_tables(DSEQ))
    new_k, new_v = [], []

    for i in range(depth):
        g_mix = norm_mix[i][None, :]
        g_ffn = norm_ffn[i][None, :]
        if i % 2 == 0:
            a = i // 2
            wq = _weight_bf16(w_qkv, a)
            wo = _weight_bf16(w_o, a)
        else:
            hl = i // 2
            w_in = _weight_bf16(hy_w_in, hl)
            w_out = _weight_bf16(hy_w_out, hl)
            w1 = jnp.pad(hy_w_f1[hl], ((0, HY_EMB_PAD - HY_EMB), (0, 0)))
        w_up = _weight_bf16(ffn_w_up, i)
        w_down = _weight_bf16(ffn_w_down, i)
        g_final = norm_final[None, :] if i == depth - 1 else None

        for si, st in enumerate(streams):
            x = st["x"]
            m = mod[i, st["rows"]]
            sh1, sc1, g1, sh2, sc2, g2 = (m[:, t] for t in range(6))
            latent = si == 0
            if i % 2 == 0:
                res = _qkv_proj(x, g_mix, sh1, sc1, wq, rope_tabs if latent else None,
                                emit_kv=not latent, bm=bm)
                qkv = res[0]
                if latent:
                    ck = cache_k[:, a].reshape(DB, -1, KV_DIM).astype(BF16)
                    cv = cache_v[:, a].reshape(DB, -1, KV_DIM).astype(BF16)
                    att = _attention(qkv, attn_sink[a], ck, cv, DB, DSEQ, window=True)
                else:
                    kv = res[1]
                    new_k.append(kv[:, :KV_DIM].reshape(B, 1, SEQ, N_KV_HEADS, HEAD_DIM))
                    new_v.append(kv[:, KV_DIM:].reshape(B, 1, SEQ, N_KV_HEADS, HEAD_DIM))
                    ck = qkv[:, Q_DIM:Q_DIM + KV_DIM].reshape(B, SEQ, KV_DIM)
                    cv = qkv[:, Q_DIM + KV_DIM:].reshape(B, SEQ, KV_DIM)
                    att = _attention(qkv, attn_sink[a], ck, cv, B, SEQ, window=False)
                x = _proj_residual(att, wo, x, g1, bm)
            else:
                plan = _FftPlan(st["seq"])
                feat, _ = _filter_tables(st["seq"])
                hidden = _filter_hidden(jnp.asarray(feat), w1, hy_b_f1[hl][None, :], hy_w_f2[hl],
                                        hy_b_f2[hl][None, :], hy_freq[hl][None, :])
                hspec = _filter_spectrum(plan, hidden, hy_w_f3[hl], st["lanes"])
                x = _hyena_mixer(x, g_mix, sh1, sc1, g1, w_in, hy_conv[hl], hspec, hy_bias[hl],
                                 w_out, st["seq"], st["lanes"], bm)
            x = _conv_ffn(x, g_ffn, sh2, sc2, g2, w_up, ffn_conv[i], w_down, st["seq"], bm, 512,
                          g_final=g_final)
            st["x"] = x

    y_sample = streams[0]["x"]
    y_prompt = streams[1]["x"].reshape(B, SEQ, D)
    new_cache_k = new_k[0] if len(new_k) == 1 else jnp.concatenate(new_k, axis=1)
    new_cache_v = new_v[0] if len(new_v) == 1 else jnp.concatenate(new_v, axis=1)
    return (y_prompt, y_sample, new_cache_k, new_cache_v)
```

```python
import functools
import math

import jax
import jax.numpy as jnp
import numpy as np
from jax import lax
from jax.experimental import pallas as pl
from jax.experimental.pallas import tpu as pltpu

F32 = jnp.float32
BF16 = jnp.bfloat16

D_MODEL = 2048
HEAD_DIM = 128
N_HEADS = 16
N_KV_HEADS = 4
GROUP = N_HEADS // N_KV_HEADS
Q_DIM = N_HEADS * HEAD_DIM
KV_DIM = N_KV_HEADS * HEAD_DIM
QKV_DIM = Q_DIM + 2 * KV_DIM
GRID_W = 64
WINDOW = 128
ROPE_THETA = 10000.0
ROPE_PAIRS = HEAD_DIM // 4
HY_BANDS = 16
HY_EMB = 1 + 2 * HY_BANDS
HY_EMB_PAD = 40
HY_FH = 64
HY_MIN_DECAY = math.log(1e-2) / 1.5
HY_MAX_DECAY = math.log(1e-2) / 0.3
D_FF = 5632
EPS = 1e-6
NEG_INF = -1e30
LOG2E = math.log2(math.e)
Q_SCALE = HEAD_DIM ** -0.5 * LOG2E

VMEM_LIMIT = 56 * 1024 * 1024
FFN_NORM_VMEM_LIMIT = 61 * 1024 * 1024
CAST_ROWS, CAST_COLS = 2048, 1024
NORM_ROWS = 16
HALO = 16
FFT_N2 = 256
FFT_N2_SINGLE = 512
FFT_ROWS = 32
FFT_FWD_ROWS = 16
SUB_COLS = 256
QKV_BN = 1024
ATTN_ROWS = 32
FFN_DOWN_COLS = 1024
TW_LANES = 128


def _cparams(sem):
    return pltpu.CompilerParams(dimension_semantics=sem, vmem_limit_bytes=VMEM_LIMIT)


def _store_modulated_norm(dst_ref, dst0, x_ref, n_rows, g_ref, sh_ref, sc_ref):
    gain = g_ref[...] * (1.0 + sc_ref[0])
    shift = sh_ref[0]
    for r in range(0, n_rows, NORM_ROWS):
        x = x_ref[0, r:r + NORM_ROWS, :]
        ms = jnp.mean(x * x, axis=-1, keepdims=True)
        y = x * lax.rsqrt(ms + EPS) * gain + shift
        dst_ref[dst0 + r:dst0 + r + NORM_ROWS, :] = y.astype(BF16)


def _cast_kernel(w_ref, o_ref):
    o_ref[...] = w_ref[...].astype(BF16)


def _weight_bf16(w, layer):
    _, rows, cols = w.shape
    br = CAST_ROWS if rows % CAST_ROWS == 0 else CAST_ROWS // 4
    bc = CAST_COLS
    return pl.pallas_call(
        _cast_kernel,
        grid=(rows // br, cols // bc),
        in_specs=[pl.BlockSpec((None, br, bc), lambda r, c: (layer, r, c))],
        out_specs=pl.BlockSpec((br, bc), lambda r, c: (r, c)),
        out_shape=jax.ShapeDtypeStruct((rows, cols), BF16),
        compiler_params=_cparams(("parallel", "parallel")),
        name="weight_cast",
    )(w)


def _mod_kernel(c_ref, w_ref, b_ref, o_ref):
    c = c_ref[...]
    s = c * (1.0 / (1.0 + jnp.exp(-c)))
    o_ref[0] = jnp.dot(s, w_ref[0], preferred_element_type=F32,
                       precision=lax.Precision.HIGHEST) + b_ref[0]


def _modulation(cvec, w_mod, b_mod):
    depth, d, n = w_mod.shape
    rows = cvec.shape[0]
    bn = 1024
    return pl.pallas_call(
        _mod_kernel,
        grid=(depth, n // bn),
        in_specs=[
            pl.BlockSpec((rows, d), lambda l, j: (0, 0)),
            pl.BlockSpec((1, d, bn), lambda l, j: (l, 0, j)),
            pl.BlockSpec((1, 1, bn), lambda l, j: (l, 0, j)),
        ],
        out_specs=pl.BlockSpec((1, rows, bn), lambda l, j: (l, 0, j)),
        out_shape=jax.ShapeDtypeStruct((depth, rows, n), F32),
        compiler_params=_cparams(("parallel", "parallel")),
        name="modulation",
    )(cvec, w_mod, b_mod.reshape(depth, 1, n))


def _qkv_kernel(*refs, rope, emit_kv):
    x_ref, g_ref, sh_ref, sc_ref, w_ref = refs[:5]
    pos = 5
    if rope:
        tabs = (refs[pos:pos + 3], refs[pos + 3:pos + 6])
        pos += 6
    o_ref = refs[pos]
    pos += 1
    if emit_kv:
        kv_ref = refs[pos]
        pos += 1
    h_scr = refs[pos]

    j = pl.program_id(2)
    n_q = Q_DIM // QKV_BN
    heads = QKV_BN // HEAD_DIM

    @pl.when(j == 0)
    def _():
        _store_modulated_norm(h_scr, 0, x_ref, h_scr.shape[0], g_ref, sh_ref, sc_ref)

    q_scale = jnp.where(j < n_q, Q_SCALE, 1.0)
    for c0 in range(0, QKV_BN, SUB_COLS):
        cols = slice(c0, c0 + SUB_COLS)
        acc = jnp.dot(h_scr[...], w_ref[:, cols], preferred_element_type=F32)
        if emit_kv:
            kv_ref[:, cols] = acc
        for h in range(SUB_COLS // HEAD_DIM):
            xh = acc[:, h * HEAD_DIM:(h + 1) * HEAD_DIM]
            if rope:
                cos_ref, sa_ref, sb_ref = tabs[c0 // (QKV_BN // 2)]
                xh = (xh * cos_ref[...] + pltpu.roll(xh, HEAD_DIM - 32, axis=1) * sa_ref[...]
                      + pltpu.roll(xh, 32, axis=1) * sb_ref[...])
            else:
                xh = xh * q_scale
            o_ref[:, c0 + h * HEAD_DIM:c0 + (h + 1) * HEAD_DIM] = xh.astype(BF16)


def _qkv_proj(x, g, shift, scale, w, rope_tabs, emit_kv, bm):
    G, S, D = x.shape
    nb = S // bm
    bn = QKV_BN
    assert 2 * KV_DIM == bn and Q_DIM % bn == 0 and bn // HEAD_DIM == 2 * N_KV_HEADS
    nj = QKV_DIM // bn
    n_q = Q_DIM // bn
    rope = rope_tabs is not None
    in_specs = [
        pl.BlockSpec((1, bm, D), lambda gi, i, j: (gi, i, 0)),
        pl.BlockSpec((1, D), lambda gi, i, j: (0, 0)),
        pl.BlockSpec((1, 1, D), lambda gi, i, j: (gi, 0, 0)),
        pl.BlockSpec((1, 1, D), lambda gi, i, j: (gi, 0, 0)),
        pl.BlockSpec((D, bn), lambda gi, i, j: (0, j)),
    ]
    args = [x, g, shift, scale, w]
    if rope:
        for other in (1, 2):
            for t in rope_tabs:
                in_specs.append(pl.BlockSpec(
                    (None, bm, HEAD_DIM),
                    lambda gi, i, j, other=other: (jnp.where(j < n_q, 0, other), i, 0)))
                args.append(t)
    out_specs = [pl.BlockSpec((bm, bn), lambda gi, i, j: (gi * nb + i, j))]
    out_shape = [jax.ShapeDtypeStruct((G * S, QKV_DIM), BF16)]
    if emit_kv:
        out_specs.append(pl.BlockSpec((bm, bn), lambda gi, i, j: (gi * nb + i, 0)))
        out_shape.append(jax.ShapeDtypeStruct((G * S, 2 * KV_DIM), F32))
    res = pl.pallas_call(
        functools.partial(_qkv_kernel, rope=rope, emit_kv=emit_kv),
        grid=(G, nb, nj),
        in_specs=in_specs,
        out_specs=out_specs,
        out_shape=out_shape,
        scratch_shapes=[pltpu.VMEM((bm, D), BF16)],
        compiler_params=_cparams(("parallel", "parallel", "arbitrary")),
        name="qkv_proj",
    )(*args)
    return res


def _attn_kernel(*refs, window, bq):
    sink_ref, q_ref = refs[:2]
    pos = 2
    if window:
        kp_ref, kc_ref, kn_ref, vp_ref, vc_ref, vn_ref, bias_ref = refs[pos:pos + 7]
        pos += 7
    ck_ref, cv_ref, o_ref, k_scr, v_scr, s_scr, p_scr = refs[pos:pos + 7]

    n_ctx = ck_ref.shape[1]
    n_win = 3 * bq if window else 0
    tr = ATTN_ROWS

    for h in range(N_KV_HEADS):
        hs = slice(h * HEAD_DIM, (h + 1) * HEAD_DIM)
        if window:
            for t, (k_ref, v_ref) in enumerate(((kp_ref, vp_ref), (kc_ref, vc_ref), (kn_ref, vn_ref))):
                k_scr[h, t * bq:(t + 1) * bq] = k_ref[:, hs]
                v_scr[h, t * bq:(t + 1) * bq] = v_ref[:, hs]
        k_scr[h, n_win:] = ck_ref[0, :, hs]
        v_scr[h, n_win:] = cv_ref[0, :, hs]
        qs = jnp.concatenate(
            [q_ref[:, (h * GROUP + g) * HEAD_DIM:(h * GROUP + g + 1) * HEAD_DIM] for g in range(GROUP)],
            axis=0)
        s_scr[h] = lax.dot_general(qs, k_scr[h], (((1,), (1,)), ((), ())),
                                   preferred_element_type=F32)
    for h in range(N_KV_HEADS):
        for r0 in range(0, GROUP * bq, tr):
            sk = sink_ref[h * GROUP + r0 // bq] * LOG2E
            s = s_scr[h, r0:r0 + tr, :]
            if window:
                q0 = r0 % bq
                s = s + bias_ref[0, q0:q0 + tr, :]
            m = jnp.maximum(jnp.max(s, axis=-1, keepdims=True), sk)
            p = jnp.exp2(s - m)
            denom = jnp.sum(p, axis=-1, keepdims=True) + jnp.exp2(sk - m)
            p_scr[h, r0:r0 + tr, :] = (p * (1.0 / denom)).astype(BF16)
    for h in range(N_KV_HEADS):
        o = jnp.dot(p_scr[h], v_scr[h], preferred_element_type=F32)
        for g in range(GROUP):
            c0 = (h * GROUP + g) * HEAD_DIM
            o_ref[:, c0:c0 + HEAD_DIM] = o[g * bq:(g + 1) * bq].astype(BF16)


def _window_bias(n_ctx):
    i = np.arange(WINDOW)[:, None]
    j = np.arange(WINDOW)[None, :]
    prev = np.where(j >= i, 0.0, NEG_INF)
    cur = np.zeros((WINDOW, WINDOW))
    nxt = np.where(j <= i, 0.0, NEG_INF)
    dead = np.full((WINDOW, WINDOW), NEG_INF)
    ctx = np.zeros((WINDOW, n_ctx))
    first = np.concatenate([dead, cur, nxt, ctx], axis=1)
    mid = np.concatenate([prev, cur, nxt, ctx], axis=1)
    last = np.concatenate([prev, cur, dead, ctx], axis=1)
    return np.stack([first, mid, last]).astype(np.float32)


def _attention(qkv, sink, ctx_k, ctx_v, n_batch, seq, window):
    kcol = Q_DIM // KV_DIM
    vcol = kcol + 1
    n_ctx = ctx_k.shape[1]
    if window:
        bq = WINDOW
        nq = seq // bq
        in_specs = [
            pl.BlockSpec(memory_space=pltpu.SMEM),
            pl.BlockSpec((bq, Q_DIM), lambda b, n: (b * nq + n, 0)),
            pl.BlockSpec((bq, KV_DIM), lambda b, n: (b * nq + jnp.maximum(n - 1, 0), kcol)),
            pl.BlockSpec((bq, KV_DIM), lambda b, n: (b * nq + n, kcol)),
            pl.BlockSpec((bq, KV_DIM), lambda b, n: (b * nq + jnp.minimum(n + 1, nq - 1), kcol)),
            pl.BlockSpec((bq, KV_DIM), lambda b, n: (b * nq + jnp.maximum(n - 1, 0), vcol)),
            pl.BlockSpec((bq, KV_DIM), lambda b, n: (b * nq + n, vcol)),
            pl.BlockSpec((bq, KV_DIM), lambda b, n: (b * nq + jnp.minimum(n + 1, nq - 1), vcol)),
            pl.BlockSpec((1, bq, 3 * bq + n_ctx),
                         lambda b, n: (jnp.where(n == 0, 0, jnp.where(n == nq - 1, 2, 1)), 0, 0)),
            pl.BlockSpec((1, n_ctx, KV_DIM), lambda b, n: (b, 0, 0)),
            pl.BlockSpec((1, n_ctx, KV_DIM), lambda b, n: (b, 0, 0)),
        ]
        args = [sink, qkv, qkv, qkv, qkv, qkv, qkv, qkv, jnp.asarray(_window_bias(n_ctx)), ctx_k, ctx_v]
    else:
        bq = seq
        nq = 1
        in_specs = [
            pl.BlockSpec(memory_space=pltpu.SMEM),
            pl.BlockSpec((bq, Q_DIM), lambda b, n: (b, 0)),
            pl.BlockSpec((1, n_ctx, KV_DIM), lambda b, n: (b, 0, 0)),
            pl.BlockSpec((1, n_ctx, KV_DIM), lambda b, n: (b, 0, 0)),
        ]
        args = [sink, qkv, ctx_k, ctx_v]
    n_keys = (3 * bq if window else 0) + n_ctx
    return pl.pallas_call(
        functools.partial(_attn_kernel, window=window, bq=bq),
        grid=(n_batch, nq),
        in_specs=in_specs,
        out_specs=pl.BlockSpec((bq, Q_DIM), lambda b, n: (b * nq + n, 0)),
        out_shape=jax.ShapeDtypeStruct((n_batch * seq, Q_DIM), BF16),
        scratch_shapes=[
            pltpu.VMEM((N_KV_HEADS, n_keys, HEAD_DIM), BF16),
            pltpu.VMEM((N_KV_HEADS, n_keys, HEAD_DIM), BF16),
            pltpu.VMEM((N_KV_HEADS, GROUP * bq, n_keys), F32),
            pltpu.VMEM((N_KV_HEADS, GROUP * bq, n_keys), BF16),
        ],
        compiler_params=_cparams(("parallel", "parallel")),
        name="attention",
    )(*args)


def _proj_res_kernel(a_ref, w_ref, x_ref, gate_ref, o_ref):
    half_w = o_ref.shape[-1] // 2
    for half in range(2):
        cols = slice(half * half_w, (half + 1) * half_w)
        acc = jnp.dot(a_ref[...], w_ref[:, cols], preferred_element_type=F32)
        o_ref[0, :, cols] = x_ref[0, :, cols] + gate_ref[0, :, cols] * acc


def _proj_residual(a, w, x, gate, bm):
    G, S, D = x.shape
    K = a.shape[1]
    nb = S // bm
    bn = 1024
    return pl.pallas_call(
        _proj_res_kernel,
        grid=(G, nb, D // bn),
        in_specs=[
            pl.BlockSpec((bm, K), lambda gi, i, j: (gi * nb + i, 0)),
            pl.BlockSpec((K, bn), lambda gi, i, j: (0, j)),
            pl.BlockSpec((1, bm, bn), lambda gi, i, j: (gi, i, j)),
            pl.BlockSpec((1, 1, bn), lambda gi, i, j: (gi, 0, j)),
        ],
        out_specs=pl.BlockSpec((1, bm, bn), lambda gi, i, j: (gi, i, j)),
        out_shape=jax.ShapeDtypeStruct((G, S, D), F32),
        compiler_params=_cparams(("parallel", "parallel", "parallel")),
        name="proj_residual",
    )(a, w, x, gate)


def _fill_h_ext(h_scr, xm_ref, xp_ref, xn_ref, g_ref, sh_ref, sc_ref, bm):
    _store_modulated_norm(h_scr, 0, xp_ref, HALO, g_ref, sh_ref, sc_ref)
    _store_modulated_norm(h_scr, HALO, xm_ref, bm, g_ref, sh_ref, sc_ref)
    _store_modulated_norm(h_scr, HALO + bm, xn_ref, HALO, g_ref, sh_ref, sc_ref)


def _dwconv3_rows(u_ext, cw, row0, bm, seq_len):
    assert seq_len & (seq_len - 1) == 0
    pos = (row0 + lax.broadcasted_iota(jnp.int32, (bm, 1), 0)) & (seq_len - 1)
    prev = jnp.where(pos == 0, 0.0, u_ext[HALO - 1:HALO - 1 + bm])
    nxt = jnp.where(pos == seq_len - 1, 0.0, u_ext[HALO + 1:HALO + 1 + bm])
    return prev * cw[0:1] + u_ext[HALO:HALO + bm] * cw[1:2] + nxt * cw[2:3]


def _halo_specs(bm, D, S):
    nh = bm // HALO
    last = S // HALO - 1
    return [
        pl.BlockSpec((1, bm, D), lambda gi, i, j: (gi, i, 0)),
        pl.BlockSpec((1, HALO, D), lambda gi, i, j: (gi, jnp.maximum(i * nh - 1, 0), 0)),
        pl.BlockSpec((1, HALO, D), lambda gi, i, j: (gi, jnp.minimum((i + 1) * nh, last), 0)),
    ]


def _ffn_kernel(*refs, bm, seq_len, final_norm):
    (xm_ref, xp_ref, xn_ref, g_ref, sh_ref, sc_ref, gate_ref,
     wa_ref, wb_ref, cw_ref, wd_ref) = refs[:11]
    pos = 11
    if final_norm:
        gf_ref = refs[pos]
        pos += 1
    o_ref, h_scr = refs[pos:pos + 2]

    i = pl.program_id(1)
    j = pl.program_id(2)
    D = o_ref.shape[-1]

    @pl.when(j == 0)
    def _():
        _fill_h_ext(h_scr, xm_ref, xp_ref, xn_ref, g_ref, sh_ref, sc_ref, bm)
        o_ref[0] = jnp.zeros((bm, D), F32)

    ua = jnp.dot(h_scr[...], wa_ref[...], preferred_element_type=F32)
    ub = jnp.dot(h_scr[HALO:HALO + bm], wb_ref[...], preferred_element_type=F32)
    a = _dwconv3_rows(ua, cw_ref[...], i * bm, bm, seq_len)
    act = (a * (1.0 / (1.0 + jnp.exp(-a))) * ub).astype(BF16)
    for c0 in range(0, D, FFN_DOWN_COLS):
        cols = slice(c0, c0 + FFN_DOWN_COLS)
        o_ref[0, :, cols] += jnp.dot(act, wd_ref[:, cols], preferred_element_type=F32)

    @pl.when(j == pl.num_programs(2) - 1)
    def _():
        for r in range(0, bm, NORM_ROWS):
            rows = slice(r, r + NORM_ROWS)
            y = xm_ref[0, rows, :] + gate_ref[0] * o_ref[0, rows, :]
            if final_norm:
                ms = jnp.mean(y * y, axis=-1, keepdims=True)
                y = y * lax.rsqrt(ms + EPS) * gf_ref[...]
            o_ref[0, rows, :] = y


def _conv_ffn(x, g, shift, scale, gate, w_up, conv_w, w_down, seq_len, bm, bf, g_final=None):
    G, S, D = x.shape
    nb = S // bm
    nc = D_FF // bf
    final_norm = g_final is not None
    in_specs = _halo_specs(bm, D, S) + [
        pl.BlockSpec((1, D), lambda gi, i, j: (0, 0)),
        pl.BlockSpec((1, 1, D), lambda gi, i, j: (gi, 0, 0)),
        pl.BlockSpec((1, 1, D), lambda gi, i, j: (gi, 0, 0)),
        pl.BlockSpec((1, 1, D), lambda gi, i, j: (gi, 0, 0)),
        pl.BlockSpec((D, bf), lambda gi, i, j: (0, j)),
        pl.BlockSpec((D, bf), lambda gi, i, j: (0, nc + j)),
        pl.BlockSpec((3, bf), lambda gi, i, j: (0, j)),
        pl.BlockSpec((bf, D), lambda gi, i, j: (j, 0)),
    ]
    args = [x, x, x, g, shift, scale, gate, w_up, w_up, conv_w, w_down]
    if final_norm:
        in_specs.append(pl.BlockSpec((1, D), lambda gi, i, j: (0, 0)))
        args.append(g_final)
    return pl.pallas_call(
        functools.partial(_ffn_kernel, bm=bm, seq_len=seq_len, final_norm=final_norm),
        grid=(G, nb, nc),
        in_specs=in_specs,
        out_specs=pl.BlockSpec((1, bm, D), lambda gi, i, j: (gi, i, 0)),
        out_shape=jax.ShapeDtypeStruct((G, S, D), F32),
        scratch_shapes=[pltpu.VMEM((bm + 2 * HALO, D), BF16)],
        compiler_params=pltpu.CompilerParams(
            dimension_semantics=("parallel", "parallel", "arbitrary"),
            vmem_limit_bytes=FFN_NORM_VMEM_LIMIT if final_norm else VMEM_LIMIT),
        name="conv_ffn",
    )(*args)


def _hy_in_kernel(xm_ref, xp_ref, xn_ref, g_ref, sh_ref, sc_ref, w_ref, cw_ref, o_ref, h_scr,
                  *, bm, seq_len):
    i = pl.program_id(1)
    j = pl.program_id(2)

    @pl.when(j == 0)
    def _():
        _fill_h_ext(h_scr, xm_ref, xp_ref, xn_ref, g_ref, sh_ref, sc_ref, bm)

    for c0 in range(0, o_ref.shape[-1], SUB_COLS):
        cols = slice(c0, c0 + SUB_COLS)
        u = jnp.dot(h_scr[...], w_ref[:, cols], preferred_element_type=F32)
        o_ref[0, :, cols] = _dwconv3_rows(u, cw_ref[:, cols], i * bm, bm, seq_len)


def _hyena_in(x, g, shift, scale, w_in, conv_w, seq_len, bm, bn):
    G, S, D = x.shape
    nb = S // bm
    N = w_in.shape[1]
    in_specs = _halo_specs(bm, D, S) + [
        pl.BlockSpec((1, D), lambda gi, i, j: (0, 0)),
        pl.BlockSpec((1, 1, D), lambda gi, i, j: (gi, 0, 0)),
        pl.BlockSpec((1, 1, D), lambda gi, i, j: (gi, 0, 0)),
        pl.BlockSpec((D, bn), lambda gi, i, j: (0, j)),
        pl.BlockSpec((3, bn), lambda gi, i, j: (0, j)),
    ]
    return pl.pallas_call(
        functools.partial(_hy_in_kernel, bm=bm, seq_len=seq_len),
        grid=(G, nb, N // bn),
        in_specs=in_specs,
        out_specs=pl.BlockSpec((1, bm, bn), lambda gi, i, j: (gi, i, j)),
        out_shape=jax.ShapeDtypeStruct((G, S, N), F32),
        scratch_shapes=[pltpu.VMEM((bm + 2 * HALO, D), BF16)],
        compiler_params=_cparams(("parallel", "parallel", "arbitrary")),
        name="hyena_in",
    )(x, x, x, g, shift, scale, w_in, conv_w)


class _FftPlan:
    def __init__(self, seq_len):
        self.L = seq_len
        self.n2 = FFT_N2 if 2 * seq_len > FFT_N2_SINGLE else 2 * seq_len
        self.n1 = 2 * seq_len // self.n2
        self.n = self.n1 * self.n2
        self.classes = self.n1 // 2 + 1
        self.slab = min(self.n2, seq_len)
        self.data_slabs = seq_len // self.slab

    def stage1_coef(self, m, k1):
        th = 2.0 * math.pi * ((m * k1) % self.n1) / self.n1
        return _snap(math.cos(th)), _snap(-math.sin(th))

    def class_groups(self):
        half = self.n1 // 2
        groups = [(k, half - k) for k in range((half + 1) // 2)]
        if half % 2 == 0:
            groups.append((half // 2,))
        return groups

    def class_weight(self, k1):
        if self.n1 == 1:
            return 1.0
        return 1.0 if k1 in (0, self.n1 // 2) else 2.0


def _snap(v):
    for t in (0.0, 1.0, -1.0):
        if abs(v - t) < 1e-12:
            return t
    return v


@functools.lru_cache(maxsize=None)
def _dft_consts(n1, n2):
    n = n1 * n2
    k = np.arange(n2)
    ang = 2.0 * np.pi * np.outer(k, k) / n2
    fr, fi = np.cos(ang), -np.sin(ang)
    fwd = np.block([[fr, -fi], [fi, fr]]).astype(np.float32)
    inv = np.block([[fr, fi], [-fi, fr]]).astype(np.float32)
    classes = n1 // 2 + 1
    tw = 2.0 * np.pi * np.outer(np.arange(classes), np.arange(n2)) / n
    twc = np.repeat(np.cos(tw)[:, :, None], TW_LANES, axis=2).astype(np.float32)
    tws = np.repeat((-np.sin(tw))[:, :, None], TW_LANES, axis=2).astype(np.float32)
    return fwd, inv, twc, tws


def _const_spec(shape):
    nd = len(shape)
    return pl.BlockSpec(shape, lambda *_: (0,) * nd, pipeline_mode=pl.Buffered(1))


def _lane_tile(t, lanes):
    reps = lanes // t.shape[1]
    return t if reps == 1 else jnp.concatenate([t] * reps, axis=1)


def _scale(c, x):
    return x if c == 1.0 else (-x if c == -1.0 else c * x)


def _lin2(c1, x1, c2, x2):
    if x1 is None or c1 == 0.0:
        x1 = None
    if x2 is None or c2 == 0.0:
        x2 = None
    if x1 is None and x2 is None:
        return None
    if x2 is None:
        return _scale(c1, x1)
    if x1 is None:
        return _scale(c2, x2)
    if abs(abs(c1) - abs(c2)) < 1e-12:
        s = (x1 + x2) if (c1 > 0) == (c2 > 0) else (x1 - x2)
        return _scale(c1, s)
    return c1 * x1 + c2 * x2


def _rdft_half(xs):
    n = len(xs)
    if n == 1:
        return [(xs[0], None)]
    if n == 2:
        return [(_lin2(1.0, xs[0], 1.0, xs[1]), None), (_lin2(1.0, xs[0], -1.0, xs[1]), None)]
    half, quarter = n // 2, n // 4
    ev, od = _rdft_half(xs[0::2]), _rdft_half(xs[1::2])
    out = [None] * (half + 1)
    for j in range(quarter + 1):
        th = 2.0 * math.pi * j / n
        wr, wi = _snap(math.cos(th)), _snap(-math.sin(th))
        (er, ei), (orr, oi) = ev[j], od[j]
        tr = _lin2(wr, orr, -wi, oi)
        ti = _lin2(wi, orr, wr, oi)
        out[j] = (_lin2(1.0, er, 1.0, tr), _lin2(1.0, ei, 1.0, ti))
        if half - j != j:
            out[half - j] = (_lin2(1.0, er, -1.0, tr), _lin2(1.0, ti, -1.0, ei))
    return out


def _fft_forward_all(plan, n_slabs, src_ref, src_lead, twc_ref, tws_ref, abuf):
    n2, R = plan.n2, FFT_FWD_ROWS
    lanes = abuf.shape[-1]

    def body(r, carry):
        off = pl.multiple_of(r * R, R)
        for l0 in range(0, lanes, TW_LANES):
            ls = slice(l0, l0 + TW_LANES)
            xs = [src_ref[src_lead + (pl.ds(m * n2 + off, R), ls)] if m < n_slabs else None
                  for m in range(plan.n1)]
            for k1, (ar, ai) in enumerate(_rdft_half(xs)):
                if k1 == 0:
                    abuf[0, pl.ds(off, R), ls] = ar.astype(BF16)
                    continue
                tc = twc_ref[k1, pl.ds(off, R), :]
                ts = tws_ref[k1, pl.ds(off, R), :]
                abuf[k1, pl.ds(off, R), ls] = _lin2(1.0, ar * tc, -1.0, None if ai is None else ai * ts).astype(BF16)
                abuf[k1, pl.ds(n2 + off, R), ls] = _lin2(1.0, ar * ts, 1.0, None if ai is None else ai * tc).astype(BF16)
        return carry

    lax.fori_loop(0, n2 // R, body, 0)


def _fftconv_kernel(z_ref, gate_ref, bias_ref, h_ref, fwd_ref, inv_ref, twc_ref, tws_ref,
                    o_ref, abuf, ybuf, b_scr, acc_scr, *, plan):
    n2, L, R = plan.n2, plan.L, FFT_ROWS
    lanes = o_ref.shape[-1]
    inv_n = 1.0 / plan.n

    def dots(k1, slot):
        if k1 == 0:
            xf = jnp.dot(fwd_ref[:, 0:n2], abuf[0, 0:n2], preferred_element_type=F32)
        else:
            xf = jnp.dot(fwd_ref[...], abuf[k1], preferred_element_type=F32)
        xr, xi = xf[:n2], xf[n2:]
        hr, hi = h_ref[k1, 0:n2], h_ref[k1, n2:]
        ys = slot % ybuf.shape[0]
        ybuf[ys, 0:n2] = (xr * hr - xi * hi).astype(BF16)
        ybuf[ys, n2:] = (xr * hi + xi * hr).astype(BF16)
        if k1 == 0:
            b_scr[slot, 0:n2] = jnp.dot(inv_ref[0:n2, :], ybuf[ys], preferred_element_type=F32)
        else:
            b_scr[slot] = jnp.dot(inv_ref[...], ybuf[ys], preferred_element_type=F32)

    if plan.n1 == 1:
        hr, hi = h_ref[0, 0:n2], h_ref[0, n2:]
        for s in range(z_ref.shape[0]):
            z = z_ref[s]
            xf = jnp.dot(fwd_ref[:, 0:L], z.astype(BF16), preferred_element_type=F32)
            xr, xi = xf[:n2], xf[n2:]
            ybuf[s % 2, 0:n2] = (xr * hr - xi * hi).astype(BF16)
            ybuf[s % 2, n2:] = (xr * hi + xi * hr).astype(BF16)
            b = jnp.dot(inv_ref[0:L, :], ybuf[s % 2], preferred_element_type=F32)
            y = b * inv_n + bias_ref[...] * z
            o_ref[s] = (gate_ref[s] * y).astype(o_ref.dtype)
        return

    n_slots = b_scr.shape[0]

    def inverse(group, slots, first, last):
        wk = plan.class_weight(group[-1])
        for off in range(0, n2, R):
            vals = []
            for k1, slot in zip(group, slots):
                br = b_scr[slot, off:off + R, :]
                if k1 == 0:
                    vals.append((br, None))
                    continue
                bi = b_scr[slot, n2 + off:n2 + off + R, :]
                tc = _lane_tile(twc_ref[k1, off:off + R, :], lanes)
                ts = _lane_tile(tws_ref[k1, off:off + R, :], lanes)
                vals.append((br * tc + bi * ts, bi * tc - br * ts))
            if len(group) == 2:
                (ar, ai), (br_, bi_) = vals
                by_parity = [(_lin2(1.0, ar, 1.0, br_), _lin2(1.0, ai, -1.0, bi_)),
                             (_lin2(1.0, ar, -1.0, br_), _lin2(1.0, ai, 1.0, bi_))]
            else:
                by_parity = [vals[0], vals[0]]
            for m in range(plan.data_slabs):
                c, s = plan.stage1_coef(m, group[0])
                pr, pi = by_parity[m % 2]
                t = _lin2(wk * c, pr, wk * s, pi)
                rows = slice(m * n2 + off, m * n2 + off + R)
                if last:
                    z = z_ref[0, rows, :]
                    tot = t if first else acc_scr[rows, :] + t
                    y = tot * inv_n + bias_ref[...] * z
                    o_ref[0, rows, :] = (gate_ref[0, rows, :] * y).astype(o_ref.dtype)
                elif first:
                    acc_scr[rows, :] = t
                else:
                    acc_scr[rows, :] += t

    _fft_forward_all(plan, plan.data_slabs, z_ref, (0,), twc_ref, tws_ref, abuf)
    groups = plan.class_groups()
    issued = 0
    pending = None
    for gi, group in enumerate(groups):
        slots = []
        for k1 in group:
            slots.append(issued % n_slots)
            dots(k1, issued % n_slots)
            issued += 1
        if pending is not None:
            inverse(*pending, first=pending_first, last=False)
        pending, pending_first = (group, slots), gi == 0
    inverse(*pending, first=len(groups) == 1, last=True)


def _fftconv(plan, z, z_col, gate, gate_col, bias, hspec, order, n_seq, lanes, out_dtype):
    L, n2 = plan.L, plan.n2
    nc = D_MODEL // lanes
    fwd, inv, twc, tws = _dft_consts(plan.n1, n2)
    zc, gc = z_col // lanes, gate_col // lanes
    acc_rows = L if plan.n1 > 1 else 8
    sb = 1 if plan.n1 > 1 else min(n_seq, 4)
    return pl.pallas_call(
        functools.partial(_fftconv_kernel, plan=plan),
        grid=(nc, n_seq // sb),
        in_specs=[
            pl.BlockSpec((sb, L, lanes), lambda c, b: (b, 0, zc + c)),
            pl.BlockSpec((sb, L, lanes), lambda c, b: (b, 0, gc + c)),
            pl.BlockSpec((None, 1, lanes), lambda c, b: (order, 0, c)),
            pl.BlockSpec((None, plan.classes, 2 * n2, lanes), lambda c, b: (order, 0, 0, c),
                         pipeline_mode=pl.Buffered(1)),
            _const_spec((2 * n2, 2 * n2)),
            _const_spec((2 * n2, 2 * n2)),
            _const_spec((plan.classes, n2, TW_LANES)),
            _const_spec((plan.classes, n2, TW_LANES)),
        ],
        out_specs=pl.BlockSpec((sb, L, lanes), lambda c, b: (b, 0, c)),
        out_shape=jax.ShapeDtypeStruct((n_seq, L, D_MODEL), out_dtype),
        scratch_shapes=[
            pltpu.VMEM((plan.classes, 2 * n2, lanes) if plan.n1 > 1 else (1, 16, lanes), BF16),
            pltpu.VMEM((2, 2 * n2, lanes), BF16),
            pltpu.VMEM((4, 2 * n2, lanes), F32),
            pltpu.VMEM((acc_rows, lanes), F32),
        ],
        compiler_params=_cparams(("parallel", "arbitrary")),
        name="hyena_fftconv",
    )(z, gate, bias.reshape(-1, 1, D_MODEL), hspec, jnp.asarray(fwd).astype(BF16),
      jnp.asarray(inv).astype(BF16),
      jnp.asarray(twc), jnp.asarray(tws))


@functools.lru_cache(maxsize=None)
def _filter_tables(L):
    r = np.arange(2 * L)
    p = np.where(r < L, r, 2 * L - r)
    p[L] = 0
    t = p / (L - 1.0)
    w = 2.0 * np.pi * p / L
    f = np.linspace(1e-4, HY_BANDS - 1, HY_BANDS)
    feat = np.concatenate([t[:, None], np.cos(np.outer(w, f)), -np.sin(np.outer(w, f))], axis=1)
    feat = np.pad(feat, ((0, 0), (0, HY_EMB_PAD - HY_EMB))).astype(np.float32)
    t_tab = np.repeat(t[:, None], TW_LANES, axis=1).astype(np.float32)
    return feat, t_tab


def _filter_hidden_kernel(feat_ref, w1_ref, b1_ref, w2_ref, b2_ref, fq_ref, o_ref):
    hp = lax.Precision.HIGHEST
    h = jnp.sin(fq_ref[...] * (jnp.dot(feat_ref[...], w1_ref[...], preferred_element_type=F32,
                                        precision=hp) + b1_ref[...]))
    h = jnp.sin(fq_ref[...] * (jnp.dot(h, w2_ref[...], preferred_element_type=F32,
                                        precision=hp) + b2_ref[...]))
    o_ref[...] = h


def _filter_hidden(feat, w1, b1, w2, b2, freq):
    rows = feat.shape[0]
    br = 1024 if rows % 1024 == 0 else rows
    vec = pl.BlockSpec((1, HY_FH), lambda i: (0, 0))
    return pl.pallas_call(
        _filter_hidden_kernel,
        grid=(rows // br,),
        in_specs=[
            pl.BlockSpec((br, HY_EMB_PAD), lambda i: (i, 0)),
            pl.BlockSpec((HY_EMB_PAD, HY_FH), lambda i: (0, 0)),
            vec,
            pl.BlockSpec((HY_FH, HY_FH), lambda i: (0, 0)),
            vec, vec,
        ],
        out_specs=pl.BlockSpec((br, HY_FH), lambda i: (i, 0)),
        out_shape=jax.ShapeDtypeStruct((rows, HY_FH), F32),
        compiler_params=_cparams(("parallel",)),
        name="hyena_filter_hidden",
    )(feat, w1, b1, w2, b2, freq)


def _filter_spec_kernel(hid_ref, w3a_ref, w3b_ref, t_ref, delta_ref, fwd_ref, twc_ref, tws_ref,
                        o_ref, f_scr, abuf, *, plan):
    L, n2 = plan.L, plan.n2
    lanes = o_ref.shape[-1]
    hp = lax.Precision.HIGHEST
    absd = jnp.abs(delta_ref[...])
    rows_per = min(L, 512)

    def gen(r, carry):
        for base, w_ref in ((0, w3a_ref), (L, w3b_ref)):
            rows = pl.ds(pl.multiple_of(base + r * rows_per, rows_per), rows_per)
            f = jnp.dot(hid_ref[rows, :], w_ref[0], preferred_element_type=F32, precision=hp)
            f_scr[rows, :] = f * jnp.exp(-_lane_tile(t_ref[rows, :], lanes) * absd)
        return carry

    lax.fori_loop(0, L // rows_per, gen, 0)
    f_scr[L:L + 8] = jnp.where(lax.broadcasted_iota(jnp.int32, (8, 1), 0) > 0, f_scr[L:L + 8], 0.0)

    if plan.n1 == 1:
        abuf[0, 0:n2] = f_scr[...].astype(BF16)
    else:
        _fft_forward_all(plan, plan.n1, f_scr, (), twc_ref, tws_ref, abuf)
    o_ref[0, 0] = jnp.dot(fwd_ref[:, 0:n2], abuf[0, 0:n2], preferred_element_type=F32)
    for k1 in range(1, plan.classes):
        o_ref[0, k1] = jnp.dot(fwd_ref[...], abuf[k1], preferred_element_type=F32)


def _filter_spectrum(plan, hidden, w_f3, lanes):
    L, n2 = plan.L, plan.n2
    nc = D_MODEL // lanes
    fwd, _, twc, tws = _dft_consts(plan.n1, n2)
    _, t_tab = _filter_tables(L)
    delta = np.linspace(HY_MIN_DECAY, HY_MAX_DECAY, D_MODEL).astype(np.float32)[None, :]
    w3 = w_f3.reshape(HY_FH, 2, 2, D_MODEL).transpose(1, 2, 0, 3)
    return pl.pallas_call(
        functools.partial(_filter_spec_kernel, plan=plan),
        grid=(2, nc),
        in_specs=[
            _const_spec((2 * L, HY_FH)),
            pl.BlockSpec((None, 1, HY_FH, lanes), lambda o, c: (0, o, 0, c)),
            pl.BlockSpec((None, 1, HY_FH, lanes), lambda o, c: (1, o, 0, c)),
            _const_spec((2 * L, TW_LANES)),
            pl.BlockSpec((1, lanes), lambda o, c: (0, c)),
            _const_spec((2 * n2, 2 * n2)),
            _const_spec((plan.classes, n2, TW_LANES)),
            _const_spec((plan.classes, n2, TW_LANES)),
        ],
        out_specs=pl.BlockSpec((1, plan.classes, 2 * n2, lanes), lambda o, c: (o, 0, 0, c)),
        out_shape=jax.ShapeDtypeStruct((2, plan.classes, 2 * n2, D_MODEL), F32),
        scratch_shapes=[pltpu.VMEM((2 * L, lanes), F32),
                        pltpu.VMEM((plan.classes, 2 * n2, lanes), BF16)],
        compiler_params=_cparams(("parallel", "parallel")),
        name="hyena_filter_spectrum",
    )(hidden, w3, w3, jnp.asarray(t_tab), jnp.asarray(delta), jnp.asarray(fwd).astype(BF16),
      jnp.asarray(twc), jnp.asarray(tws))


@functools.lru_cache(maxsize=None)
def _rope_tables(L):
    t = np.arange(L)
    inv = ROPE_THETA ** (-np.arange(ROPE_PAIRS, dtype=np.float32) / ROPE_PAIRS)
    ang_row = (t // GRID_W).astype(np.float32)[:, None] * inv
    ang_col = (t % GRID_W).astype(np.float32)[:, None] * inv
    zero = np.zeros_like(ang_row)
    cos = np.concatenate([np.cos(ang_row)] * 2 + [np.cos(ang_col)] * 2, axis=1)
    sa = np.concatenate([-np.sin(ang_row), zero, -np.sin(ang_col), zero], axis=1)
    sb = np.concatenate([zero, np.sin(ang_row), zero, np.sin(ang_col)], axis=1)
    variants = lambda t, ident: np.stack([t * Q_SCALE, t, np.full_like(t, ident)]).astype(np.float32)
    return variants(cos, 1.0), variants(sa, 0.0), variants(sb, 0.0)


def _hyena_mixer(x, g, shift, scale, gate, w_in, conv_w, hspec, bias, w_out, seq_len, lanes, bm):
    G, S, D = x.shape
    plan = _FftPlan(seq_len)
    n_seq = G * S // seq_len
    u = _hyena_in(x, g, shift, scale, w_in, conv_w, seq_len, bm, 1024)
    u = u.reshape(n_seq, seq_len, 3 * D)
    z1 = _fftconv(plan, u, 2 * D, u, 0, bias, hspec, 0, n_seq, lanes, F32)
    z2 = _fftconv(plan, z1, 0, u, D, bias, hspec, 1, n_seq, lanes, BF16)
    return _proj_residual(z2.reshape(G * S, D), w_out, x, gate, bm)


def kernel(x_prompt, x_sample, cache_k, cache_v, c, c_ctx, w_mod, b_mod, norm_mix, norm_ffn, norm_final, w_qkv, w_o, attn_sink, hy_w_in, hy_conv, hy_w_f1, hy_b_f1, hy_w_f2, hy_b_f2, hy_w_f3, hy_freq, hy_bias, hy_w_out, ffn_w_up, ffn_conv, ffn_w_down):
    B, SEQ, D = x_prompt.shape
    DB, DSEQ, _ = x_sample.shape
    depth = w_mod.shape[0]

    n_cond = DB + 1
    cvec = jnp.concatenate([c, c_ctx[None, :], jnp.zeros((16 - n_cond, D), F32)], axis=0)
    mod = _modulation(cvec, w_mod, b_mod).reshape(depth, 16, 6, 1, D)

    xs = x_sample
    xp = x_prompt.reshape(1, B * SEQ, D)
    streams = [
        dict(x=xs, rows=slice(0, DB), seq=DSEQ, n_seq=DB, lanes=256),
        dict(x=xp, rows=slice(DB, DB + 1), seq=SEQ, n_seq=B, lanes=256),
    ]
    bm = 1024
    rope_tabs = tuple(jnp.asarray(t) for t in _rope_tables(DSEQ))
    new_k, new_v = [], []

    for i in range(depth):
        g_mix = norm_mix[i][None, :]
        g_ffn = norm_ffn[i][None, :]
        if i % 2 == 0:
            a = i // 2
            wq = _weight_bf16(w_qkv, a)
            wo = _weight_bf16(w_o, a)
        else:
            hl = i // 2
            w_in = _weight_bf16(hy_w_in, hl)
            w_out = _weight_bf16(hy_w_out, hl)
            w1 = jnp.pad(hy_w_f1[hl], ((0, HY_EMB_PAD - HY_EMB), (0, 0)))
        w_up = _weight_bf16(ffn_w_up, i)
        w_down = _weight_bf16(ffn_w_down, i)
        g_final = norm_final[None, :] if i == depth - 1 else None

        for si, st in enumerate(streams):
            x = st["x"]
            m = mod[i, st["rows"]]
            sh1, sc1, g1, sh2, sc2, g2 = (m[:, t] for t in range(6))
            latent = si == 0
            if i % 2 == 0:
                res = _qkv_proj(x, g_mix, sh1, sc1, wq, rope_tabs if latent else None,
                                emit_kv=not latent, bm=bm)
                qkv = res[0]
                if latent:
                    ck = cache_k[:, a].reshape(DB, -1, KV_DIM).astype(BF16)
                    cv = cache_v[:, a].reshape(DB, -1, KV_DIM).astype(BF16)
                    att = _attention(qkv, attn_sink[a], ck, cv, DB, DSEQ, window=True)
                else:
                    kv = res[1]
                    new_k.append(kv[:, :KV_DIM].reshape(B, 1, SEQ, N_KV_HEADS, HEAD_DIM))
                    new_v.append(kv[:, KV_DIM:].reshape(B, 1, SEQ, N_KV_HEADS, HEAD_DIM))
                    ck = qkv[:, Q_DIM:Q_DIM + KV_DIM].reshape(B, SEQ, KV_DIM)
                    cv = qkv[:, Q_DIM + KV_DIM:].reshape(B, SEQ, KV_DIM)
                    att = _attention(qkv, attn_sink[a], ck, cv, B, SEQ, window=False)
                x = _proj_residual(att, wo, x, g1, bm)
            else:
                plan = _FftPlan(st["seq"])
                feat, _ = _filter_tables(st["seq"])
                hidden = _filter_hidden(jnp.asarray(feat), w1, hy_b_f1[hl][None, :], hy_w_f2[hl],
                                        hy_b_f2[hl][None, :], hy_freq[hl][None, :])
                hspec = _filter_spectrum(plan, hidden, hy_w_f3[hl], st["lanes"])
                x = _hyena_mixer(x, g_mix, sh1, sc1, g1, w_in, hy_conv[hl], hspec, hy_bias[hl],
                                 w_out, st["seq"], st["lanes"], bm)
            x = _conv_ffn(x, g_ffn, sh2, sc2, g2, w_up, ffn_conv[i], w_down, st["seq"], bm, 512,
                          g_final=g_final)
            st["x"] = x

    y_sample = streams[0]["x"]
    y_prompt = streams[1]["x"].reshape(B, SEQ, D)
    new_cache_k = new_k[0] if len(new_k) == 1 else jnp.concatenate(new_k, axis=1)
    new_cache_v = new_v[0] if len(new_v) == 1 else jnp.concatenate(new_v, axis=1)
    return (y_prompt, y_sample, new_cache_k, new_cache_v)
```

```python
import functools
import math

import jax
import jax.numpy as jnp
import numpy as np
from jax import lax
from jax.experimental import pallas as pl
from jax.experimental.pallas import tpu as pltpu

F32 = jnp.float32
BF16 = jnp.bfloat16

D_MODEL = 2048
HEAD_DIM = 128
N_HEADS = 16
N_KV_HEADS = 4
GROUP = N_HEADS // N_KV_HEADS
Q_DIM = N_HEADS * HEAD_DIM
KV_DIM = N_KV_HEADS * HEAD_DIM
QKV_DIM = Q_DIM + 2 * KV_DIM
GRID_W = 64
WINDOW = 128
ROPE_THETA = 10000.0
ROPE_PAIRS = HEAD_DIM // 4
HY_BANDS = 16
HY_EMB = 1 + 2 * HY_BANDS
HY_EMB_PAD = 40
HY_FH = 64
HY_MIN_DECAY = math.log(1e-2) / 1.5
HY_MAX_DECAY = math.log(1e-2) / 0.3
D_FF = 5632
EPS = 1e-6
NEG_INF = -1e30
LOG2E = math.log2(math.e)
Q_SCALE = HEAD_DIM ** -0.5 * LOG2E

VMEM_LIMIT = 56 * 1024 * 1024
FFN_NORM_VMEM_LIMIT = 61 * 1024 * 1024
CAST_ROWS, CAST_COLS = 2048, 1024
NORM_ROWS = 16
HALO = 16
FFT_N2 = 256
FFT_N2_SINGLE = 512
FFT_ROWS = 32
FFT_FWD_ROWS = 16
SUB_COLS = 256
HYENA_IN_BN = 1024
QKV_BN = 1024
ATTN_ROWS = 32
FFN_DOWN_COLS = 1024
TW_LANES = 128


def _cparams(sem):
    return pltpu.CompilerParams(dimension_semantics=sem, vmem_limit_bytes=VMEM_LIMIT)


def _store_modulated_norm(dst_ref, dst0, x_ref, n_rows, g_ref, sh_ref, sc_ref):
    gain = g_ref[...] * (1.0 + sc_ref[0])
    shift = sh_ref[0]
    for r in range(0, n_rows, NORM_ROWS):
        x = x_ref[0, r:r + NORM_ROWS, :]
        ms = jnp.mean(x * x, axis=-1, keepdims=True)
        y = x * lax.rsqrt(ms + EPS) * gain + shift
        dst_ref[dst0 + r:dst0 + r + NORM_ROWS, :] = y.astype(BF16)


def _cast_kernel(w_ref, o_ref):
    o_ref[...] = w_ref[...].astype(BF16)


def _weight_bf16(w, layer):
    _, rows, cols = w.shape
    br = CAST_ROWS if rows % CAST_ROWS == 0 else CAST_ROWS // 4
    bc = CAST_COLS
    return pl.pallas_call(
        _cast_kernel,
        grid=(rows // br, cols // bc),
        in_specs=[pl.BlockSpec((None, br, bc), lambda r, c: (layer, r, c))],
        out_specs=pl.BlockSpec((br, bc), lambda r, c: (r, c)),
        out_shape=jax.ShapeDtypeStruct((rows, cols), BF16),
        compiler_params=_cparams(("parallel", "parallel")),
        name="weight_cast",
    )(w)


def _mod_kernel(c_ref, w_ref, b_ref, o_ref):
    c = c_ref[...]
    s = c * (1.0 / (1.0 + jnp.exp(-c)))
    o_ref[0] = jnp.dot(s, w_ref[0], preferred_element_type=F32,
                       precision=lax.Precision.HIGHEST) + b_ref[0]


def _modulation(cvec, w_mod, b_mod):
    depth, d, n = w_mod.shape
    rows = cvec.shape[0]
    bn = 1024
    return pl.pallas_call(
        _mod_kernel,
        grid=(depth, n // bn),
        in_specs=[
            pl.BlockSpec((rows, d), lambda l, j: (0, 0)),
            pl.BlockSpec((1, d, bn), lambda l, j: (l, 0, j)),
            pl.BlockSpec((1, 1, bn), lambda l, j: (l, 0, j)),
        ],
        out_specs=pl.BlockSpec((1, rows, bn), lambda l, j: (l, 0, j)),
        out_shape=jax.ShapeDtypeStruct((depth, rows, n), F32),
        compiler_params=_cparams(("parallel", "parallel")),
        name="modulation",
    )(cvec, w_mod, b_mod.reshape(depth, 1, n))


def _qkv_kernel(*refs, rope, emit_kv):
    x_ref, g_ref, sh_ref, sc_ref, w_ref = refs[:5]
    pos = 5
    if rope:
        tabs = (refs[pos:pos + 3], refs[pos + 3:pos + 6])
        pos += 6
    o_ref = refs[pos]
    pos += 1
    if emit_kv:
        kv_ref = refs[pos]
        pos += 1
    h_scr = refs[pos]

    j = pl.program_id(2)
    n_q = Q_DIM // QKV_BN
    heads = QKV_BN // HEAD_DIM

    @pl.when(j == 0)
    def _():
        _store_modulated_norm(h_scr, 0, x_ref, h_scr.shape[0], g_ref, sh_ref, sc_ref)

    q_scale = jnp.where(j < n_q, Q_SCALE, 1.0)
    for c0 in range(0, QKV_BN, SUB_COLS):
        cols = slice(c0, c0 + SUB_COLS)
        acc = jnp.dot(h_scr[...], w_ref[:, cols], preferred_element_type=F32)
        if emit_kv:
            kv_ref[:, cols] = acc
        for h in range(SUB_COLS // HEAD_DIM):
            xh = acc[:, h * HEAD_DIM:(h + 1) * HEAD_DIM]
            if rope:
                cos_ref, sa_ref, sb_ref = tabs[c0 // (QKV_BN // 2)]
                xh = (xh * cos_ref[...] + pltpu.roll(xh, HEAD_DIM - 32, axis=1) * sa_ref[...]
                      + pltpu.roll(xh, 32, axis=1) * sb_ref[...])
            else:
                xh = xh * q_scale
            o_ref[:, c0 + h * HEAD_DIM:c0 + (h + 1) * HEAD_DIM] = xh.astype(BF16)


def _qkv_proj(x, g, shift, scale, w, rope_tabs, emit_kv, bm):
    G, S, D = x.shape
    nb = S // bm
    bn = QKV_BN
    assert 2 * KV_DIM == bn and Q_DIM % bn == 0 and bn // HEAD_DIM == 2 * N_KV_HEADS
    nj = QKV_DIM // bn
    n_q = Q_DIM // bn
    rope = rope_tabs is not None
    in_specs = [
        pl.BlockSpec((1, bm, D), lambda gi, i, j: (gi, i, 0)),
        pl.BlockSpec((1, D), lambda gi, i, j: (0, 0)),
        pl.BlockSpec((1, 1, D), lambda gi, i, j: (gi, 0, 0)),
        pl.BlockSpec((1, 1, D), lambda gi, i, j: (gi, 0, 0)),
        pl.BlockSpec((D, bn), lambda gi, i, j: (0, j)),
    ]
    args = [x, g, shift, scale, w]
    if rope:
        for other in (1, 2):
            for t in rope_tabs:
                in_specs.append(pl.BlockSpec(
                    (None, bm, HEAD_DIM),
                    lambda gi, i, j, other=other: (jnp.where(j < n_q, 0, other), i, 0)))
                args.append(t)
    out_specs = [pl.BlockSpec((bm, bn), lambda gi, i, j: (gi * nb + i, j))]
    out_shape = [jax.ShapeDtypeStruct((G * S, QKV_DIM), BF16)]
    if emit_kv:
        out_specs.append(pl.BlockSpec((bm, bn), lambda gi, i, j: (gi * nb + i, 0)))
        out_shape.append(jax.ShapeDtypeStruct((G * S, 2 * KV_DIM), F32))
    res = pl.pallas_call(
        functools.partial(_qkv_kernel, rope=rope, emit_kv=emit_kv),
        grid=(G, nb, nj),
        in_specs=in_specs,
        out_specs=out_specs,
        out_shape=out_shape,
        scratch_shapes=[pltpu.VMEM((bm, D), BF16)],
        compiler_params=_cparams(("parallel", "parallel", "arbitrary")),
        name="qkv_proj",
    )(*args)
    return res


def _attn_kernel(*refs, window, bq):
    sink_ref, q_ref = refs[:2]
    pos = 2
    if window:
        kp_ref, kc_ref, kn_ref, vp_ref, vc_ref, vn_ref, bias_ref = refs[pos:pos + 7]
        pos += 7
    ck_ref, cv_ref, o_ref, k_scr, v_scr, s_scr, p_scr, r_scr = refs[pos:pos + 8]

    n_ctx = ck_ref.shape[1]
    n_win = 3 * bq if window else 0
    tr = ATTN_ROWS

    for h in range(N_KV_HEADS):
        hs = slice(h * HEAD_DIM, (h + 1) * HEAD_DIM)
        if window:
            for t, (k_ref, v_ref) in enumerate(((kp_ref, vp_ref), (kc_ref, vc_ref), (kn_ref, vn_ref))):
                k_scr[h, t * bq:(t + 1) * bq] = k_ref[:, hs]
                v_scr[h, t * bq:(t + 1) * bq] = v_ref[:, hs]
        k_scr[h, n_win:] = ck_ref[0, :, hs]
        v_scr[h, n_win:] = cv_ref[0, :, hs]
        qs = jnp.concatenate(
            [q_ref[:, (h * GROUP + g) * HEAD_DIM:(h * GROUP + g + 1) * HEAD_DIM] for g in range(GROUP)],
            axis=0)
        s_scr[h] = lax.dot_general(qs, k_scr[h], (((1,), (1,)), ((), ())),
                                   preferred_element_type=F32)
    for h in range(N_KV_HEADS):
        for r0 in range(0, GROUP * bq, tr):
            sk = sink_ref[h * GROUP + r0 // bq] * LOG2E
            s = s_scr[h, r0:r0 + tr, :]
            if window:
                q0 = r0 % bq
                b = bias_ref[0, q0:q0 + tr, :]
                s = jnp.concatenate(
                    [s[:, :bq] + b[:, :bq], s[:, bq:2 * bq],
                     s[:, 2 * bq:3 * bq] + b[:, 2 * bq:3 * bq], s[:, 3 * bq:]], axis=1)
            m = jnp.maximum(jnp.max(s, axis=-1, keepdims=True), sk)
            p = jnp.exp2(s - m)
            denom = jnp.sum(p, axis=-1, keepdims=True) + jnp.exp2(sk - m)
            p_scr[h, r0:r0 + tr, :] = p.astype(BF16)
            r_scr[h, r0:r0 + tr, :] = jnp.broadcast_to(1.0 / denom, (tr, HEAD_DIM))
    for h in range(N_KV_HEADS):
        o = jnp.dot(p_scr[h], v_scr[h], preferred_element_type=F32) * r_scr[h]
        for g in range(GROUP):
            c0 = (h * GROUP + g) * HEAD_DIM
            o_ref[:, c0:c0 + HEAD_DIM] = o[g * bq:(g + 1) * bq].astype(BF16)


def _window_bias(n_ctx):
    i = np.arange(WINDOW)[:, None]
    j = np.arange(WINDOW)[None, :]
    prev = np.where(j >= i, 0.0, NEG_INF)
    cur = np.zeros((WINDOW, WINDOW))
    nxt = np.where(j <= i, 0.0, NEG_INF)
    dead = np.full((WINDOW, WINDOW), NEG_INF)
    ctx = np.zeros((WINDOW, n_ctx))
    first = np.concatenate([dead, cur, nxt, ctx], axis=1)
    mid = np.concatenate([prev, cur, nxt, ctx], axis=1)
    last = np.concatenate([prev, cur, dead, ctx], axis=1)
    return np.stack([first, mid, last]).astype(np.float32)


def _attention(qkv, sink, ctx_k, ctx_v, n_batch, seq, window):
    kcol = Q_DIM // KV_DIM
    vcol = kcol + 1
    n_ctx = ctx_k.shape[1]
    if window:
        bq = WINDOW
        nq = seq // bq
        in_specs = [
            pl.BlockSpec(memory_space=pltpu.SMEM),
            pl.BlockSpec((bq, Q_DIM), lambda b, n: (b * nq + n, 0)),
            pl.BlockSpec((bq, KV_DIM), lambda b, n: (b * nq + jnp.maximum(n - 1, 0), kcol)),
            pl.BlockSpec((bq, KV_DIM), lambda b, n: (b * nq + n, kcol)),
            pl.BlockSpec((bq, KV_DIM), lambda b, n: (b * nq + jnp.minimum(n + 1, nq - 1), kcol)),
            pl.BlockSpec((bq, KV_DIM), lambda b, n: (b * nq + jnp.maximum(n - 1, 0), vcol)),
            pl.BlockSpec((bq, KV_DIM), lambda b, n: (b * nq + n, vcol)),
            pl.BlockSpec((bq, KV_DIM), lambda b, n: (b * nq + jnp.minimum(n + 1, nq - 1), vcol)),
            pl.BlockSpec((1, bq, 3 * bq + n_ctx),
                         lambda b, n: (jnp.where(n == 0, 0, jnp.where(n == nq - 1, 2, 1)), 0, 0)),
            pl.BlockSpec((1, n_ctx, KV_DIM), lambda b, n: (b, 0, 0)),
            pl.BlockSpec((1, n_ctx, KV_DIM), lambda b, n: (b, 0, 0)),
        ]
        args = [sink, qkv, qkv, qkv, qkv, qkv, qkv, qkv, jnp.asarray(_window_bias(n_ctx)), ctx_k, ctx_v]
    else:
        bq = seq
        nq = 1
        in_specs = [
            pl.BlockSpec(memory_space=pltpu.SMEM),
            pl.BlockSpec((bq, Q_DIM), lambda b, n: (b, 0)),
            pl.BlockSpec((1, n_ctx, KV_DIM), lambda b, n: (b, 0, 0)),
            pl.BlockSpec((1, n_ctx, KV_DIM), lambda b, n: (b, 0, 0)),
        ]
        args = [sink, qkv, ctx_k, ctx_v]
    n_keys = (3 * bq if window else 0) + n_ctx
    return pl.pallas_call(
        functools.partial(_attn_kernel, window=window, bq=bq),
        grid=(n_batch, nq),
        in_specs=in_specs,
        out_specs=pl.BlockSpec((bq, Q_DIM), lambda b, n: (b * nq + n, 0)),
        out_shape=jax.ShapeDtypeStruct((n_batch * seq, Q_DIM), BF16),
        scratch_shapes=[
            pltpu.VMEM((N_KV_HEADS, n_keys, HEAD_DIM), BF16),
            pltpu.VMEM((N_KV_HEADS, n_keys, HEAD_DIM), BF16),
            pltpu.VMEM((N_KV_HEADS, GROUP * bq, n_keys), F32),
            pltpu.VMEM((N_KV_HEADS, GROUP * bq, n_keys), BF16),
            pltpu.VMEM((N_KV_HEADS, GROUP * bq, HEAD_DIM), F32),
        ],
        compiler_params=_cparams(("parallel", "parallel")),
        name="attention",
    )(*args)


def _proj_res_kernel(a_ref, w_ref, x_ref, gate_ref, o_ref):
    half_w = o_ref.shape[-1] // 2
    for half in range(2):
        cols = slice(half * half_w, (half + 1) * half_w)
        acc = jnp.dot(a_ref[...], w_ref[:, cols], preferred_element_type=F32)
        o_ref[0, :, cols] = x_ref[0, :, cols] + gate_ref[0, :, cols] * acc


def _proj_residual(a, w, x, gate, bm):
    G, S, D = x.shape
    K = a.shape[1]
    nb = S // bm
    bn = 1024
    return pl.pallas_call(
        _proj_res_kernel,
        grid=(G, nb, D // bn),
        in_specs=[
            pl.BlockSpec((bm, K), lambda gi, i, j: (gi * nb + i, 0)),
            pl.BlockSpec((K, bn), lambda gi, i, j: (0, j)),
            pl.BlockSpec((1, bm, bn), lambda gi, i, j: (gi, i, j)),
            pl.BlockSpec((1, 1, bn), lambda gi, i, j: (gi, 0, j)),
        ],
        out_specs=pl.BlockSpec((1, bm, bn), lambda gi, i, j: (gi, i, j)),
        out_shape=jax.ShapeDtypeStruct((G, S, D), F32),
        compiler_params=_cparams(("parallel", "parallel", "parallel")),
        name="proj_residual",
    )(a, w, x, gate)


def _fill_h_ext(h_scr, xm_ref, xp_ref, xn_ref, g_ref, sh_ref, sc_ref, bm):
    _store_modulated_norm(h_scr, 0, xp_ref, HALO, g_ref, sh_ref, sc_ref)
    _store_modulated_norm(h_scr, HALO, xm_ref, bm, g_ref, sh_ref, sc_ref)
    _store_modulated_norm(h_scr, HALO + bm, xn_ref, HALO, g_ref, sh_ref, sc_ref)


def _dwconv3_rows(u_ext, cw, row0, bm, seq_len):
    assert seq_len & (seq_len - 1) == 0
    pos = (row0 + lax.broadcasted_iota(jnp.int32, (bm, 1), 0)) & (seq_len - 1)
    prev = jnp.where(pos == 0, 0.0, u_ext[HALO - 1:HALO - 1 + bm])
    nxt = jnp.where(pos == seq_len - 1, 0.0, u_ext[HALO + 1:HALO + 1 + bm])
    return prev * cw[0:1] + u_ext[HALO:HALO + bm] * cw[1:2] + nxt * cw[2:3]


def _halo_specs(bm, D, S):
    nh = bm // HALO
    last = S // HALO - 1
    return [
        pl.BlockSpec((1, bm, D), lambda gi, i, j: (gi, i, 0)),
        pl.BlockSpec((1, HALO, D), lambda gi, i, j: (gi, jnp.maximum(i * nh - 1, 0), 0)),
        pl.BlockSpec((1, HALO, D), lambda gi, i, j: (gi, jnp.minimum((i + 1) * nh, last), 0)),
    ]


def _ffn_kernel(*refs, bm, seq_len, final_norm):
    (xm_ref, xp_ref, xn_ref, g_ref, sh_ref, sc_ref, gate_ref,
     wa_ref, wb_ref, cw_ref, wd_ref) = refs[:11]
    pos = 11
    if final_norm:
        gf_ref = refs[pos]
        pos += 1
    o_ref, h_scr = refs[pos:pos + 2]

    i = pl.program_id(1)
    j = pl.program_id(2)
    D = o_ref.shape[-1]

    @pl.when(j == 0)
    def _():
        _fill_h_ext(h_scr, xm_ref, xp_ref, xn_ref, g_ref, sh_ref, sc_ref, bm)
        o_ref[0] = jnp.zeros((bm, D), F32)

    ua = jnp.dot(h_scr[...], wa_ref[...], preferred_element_type=F32)
    ub = jnp.dot(h_scr[HALO:HALO + bm], wb_ref[...], preferred_element_type=F32)
    a = _dwconv3_rows(ua, cw_ref[...], i * bm, bm, seq_len)
    act = (a * (1.0 / (1.0 + jnp.exp(-a))) * ub).astype(BF16)
    for c0 in range(0, D, FFN_DOWN_COLS):
        cols = slice(c0, c0 + FFN_DOWN_COLS)
        o_ref[0, :, cols] += jnp.dot(act, wd_ref[:, cols], preferred_element_type=F32)

    @pl.when(j == pl.num_programs(2) - 1)
    def _():
        for r in range(0, bm, NORM_ROWS):
            rows = slice(r, r + NORM_ROWS)
            y = xm_ref[0, rows, :] + gate_ref[0] * o_ref[0, rows, :]
            if final_norm:
                ms = jnp.mean(y * y, axis=-1, keepdims=True)
                y = y * lax.rsqrt(ms + EPS) * gf_ref[...]
            o_ref[0, rows, :] = y


def _conv_ffn(x, g, shift, scale, gate, w_up, conv_w, w_down, seq_len, bm, bf, g_final=None):
    G, S, D = x.shape
    nb = S // bm
    nc = D_FF // bf
    final_norm = g_final is not None
    in_specs = _halo_specs(bm, D, S) + [
        pl.BlockSpec((1, D), lambda gi, i, j: (0, 0)),
        pl.BlockSpec((1, 1, D), lambda gi, i, j: (gi, 0, 0)),
        pl.BlockSpec((1, 1, D), lambda gi, i, j: (gi, 0, 0)),
        pl.BlockSpec((1, 1, D), lambda gi, i, j: (gi, 0, 0)),
        pl.BlockSpec((D, bf), lambda gi, i, j: (0, j)),
        pl.BlockSpec((D, bf), lambda gi, i, j: (0, nc + j)),
        pl.BlockSpec((3, bf), lambda gi, i, j: (0, j)),
        pl.BlockSpec((bf, D), lambda gi, i, j: (j, 0)),
    ]
    args = [x, x, x, g, shift, scale, gate, w_up, w_up, conv_w, w_down]
    if final_norm:
        in_specs.append(pl.BlockSpec((1, D), lambda gi, i, j: (0, 0)))
        args.append(g_final)
    return pl.pallas_call(
        functools.partial(_ffn_kernel, bm=bm, seq_len=seq_len, final_norm=final_norm),
        grid=(G, nb, nc),
        in_specs=in_specs,
        out_specs=pl.BlockSpec((1, bm, D), lambda gi, i, j: (gi, i, 0)),
        out_shape=jax.ShapeDtypeStruct((G, S, D), F32),
        scratch_shapes=[pltpu.VMEM((bm + 2 * HALO, D), BF16)],
        compiler_params=pltpu.CompilerParams(
            dimension_semantics=("parallel", "parallel", "arbitrary"),
            vmem_limit_bytes=FFN_NORM_VMEM_LIMIT if final_norm else VMEM_LIMIT),
        name="conv_ffn",
    )(*args)


def _hy_in_kernel(xm_ref, xp_ref, xn_ref, g_ref, sh_ref, sc_ref, w_ref, cw_ref, o_ref, h_scr,
                  *, bm, seq_len):
    i = pl.program_id(1)
    j = pl.program_id(2)

    @pl.when(j == 0)
    def _():
        _fill_h_ext(h_scr, xm_ref, xp_ref, xn_ref, g_ref, sh_ref, sc_ref, bm)

    for c0 in range(0, o_ref.shape[-1], SUB_COLS):
        cols = slice(c0, c0 + SUB_COLS)
        u = jnp.dot(h_scr[...], w_ref[:, cols], preferred_element_type=F32)
        o_ref[0, :, cols] = _dwconv3_rows(u, cw_ref[:, cols], i * bm, bm, seq_len)


def _hyena_in(x, g, shift, scale, w_in, conv_w, seq_len, bm, bn):
    G, S, D = x.shape
    nb = S // bm
    N = w_in.shape[1]
    in_specs = _halo_specs(bm, D, S) + [
        pl.BlockSpec((1, D), lambda gi, i, j: (0, 0)),
        pl.BlockSpec((1, 1, D), lambda gi, i, j: (gi, 0, 0)),
        pl.BlockSpec((1, 1, D), lambda gi, i, j: (gi, 0, 0)),
        pl.BlockSpec((D, bn), lambda gi, i, j: (0, j)),
        pl.BlockSpec((3, bn), lambda gi, i, j: (0, j)),
    ]
    return pl.pallas_call(
        functools.partial(_hy_in_kernel, bm=bm, seq_len=seq_len),
        grid=(G, nb, N // bn),
        in_specs=in_specs,
        out_specs=pl.BlockSpec((1, bm, bn), lambda gi, i, j: (gi, i, j)),
        out_shape=jax.ShapeDtypeStruct((G, S, N), F32),
        scratch_shapes=[pltpu.VMEM((bm + 2 * HALO, D), BF16)],
        compiler_params=_cparams(("parallel", "parallel", "arbitrary")),
        name="hyena_in",
    )(x, x, x, g, shift, scale, w_in, conv_w)


class _FftPlan:
    def __init__(self, seq_len):
        self.L = seq_len
        self.n2 = FFT_N2 if 2 * seq_len > FFT_N2_SINGLE else 2 * seq_len
        self.n1 = 2 * seq_len // self.n2
        self.n = self.n1 * self.n2
        self.classes = self.n1 // 2 + 1
        self.slab = min(self.n2, seq_len)
        self.data_slabs = seq_len // self.slab

    def stage1_coef(self, m, k1):
        th = 2.0 * math.pi * ((m * k1) % self.n1) / self.n1
        return _snap(math.cos(th)), _snap(-math.sin(th))

    def class_groups(self):
        half = self.n1 // 2
        groups = [(k, half - k) for k in range((half + 1) // 2)]
        if half % 2 == 0:
            groups.append((half // 2,))
        return groups

    def class_weight(self, k1):
        if self.n1 == 1:
            return 1.0
        return 1.0 if k1 in (0, self.n1 // 2) else 2.0


def _snap(v):
    for t in (0.0, 1.0, -1.0):
        if abs(v - t) < 1e-12:
            return t
    return v


@functools.lru_cache(maxsize=None)
def _dft_consts(n1, n2):
    n = n1 * n2
    k = np.arange(n2)
    ang = 2.0 * np.pi * np.outer(k, k) / n2
    fr, fi = np.cos(ang), -np.sin(ang)
    fwd = np.block([[fr, -fi], [fi, fr]]).astype(np.float32)
    inv = np.block([[fr, fi], [-fi, fr]]).astype(np.float32)
    classes = n1 // 2 + 1
    tw = 2.0 * np.pi * np.outer(np.arange(classes), np.arange(n2)) / n
    twc = np.repeat(np.cos(tw)[:, :, None], TW_LANES, axis=2).astype(np.float32)
    tws = np.repeat((-np.sin(tw))[:, :, None], TW_LANES, axis=2).astype(np.float32)
    return fwd, inv, twc, tws


def _const_spec(shape):
    nd = len(shape)
    return pl.BlockSpec(shape, lambda *_: (0,) * nd, pipeline_mode=pl.Buffered(1))


def _lane_tile(t, lanes):
    reps = lanes // t.shape[1]
    return t if reps == 1 else jnp.concatenate([t] * reps, axis=1)


def _scale(c, x):
    return x if c == 1.0 else (-x if c == -1.0 else c * x)


def _lin2(c1, x1, c2, x2):
    if x1 is None or c1 == 0.0:
        x1 = None
    if x2 is None or c2 == 0.0:
        x2 = None
    if x1 is None and x2 is None:
        return None
    if x2 is None:
        return _scale(c1, x1)
    if x1 is None:
        return _scale(c2, x2)
    if abs(abs(c1) - abs(c2)) < 1e-12:
        s = (x1 + x2) if (c1 > 0) == (c2 > 0) else (x1 - x2)
        return _scale(c1, s)
    return c1 * x1 + c2 * x2


def _rdft_half(xs):
    n = len(xs)
    if n == 1:
        return [(xs[0], None)]
    if n == 2:
        return [(_lin2(1.0, xs[0], 1.0, xs[1]), None), (_lin2(1.0, xs[0], -1.0, xs[1]), None)]
    half, quarter = n // 2, n // 4
    ev, od = _rdft_half(xs[0::2]), _rdft_half(xs[1::2])
    out = [None] * (half + 1)
    for j in range(quarter + 1):
        th = 2.0 * math.pi * j / n
        wr, wi = _snap(math.cos(th)), _snap(-math.sin(th))
        (er, ei), (orr, oi) = ev[j], od[j]
        tr = _lin2(wr, orr, -wi, oi)
        ti = _lin2(wi, orr, wr, oi)
        out[j] = (_lin2(1.0, er, 1.0, tr), _lin2(1.0, ei, 1.0, ti))
        if half - j != j:
            out[half - j] = (_lin2(1.0, er, -1.0, tr), _lin2(1.0, ti, -1.0, ei))
    return out


def _fft_forward_all(plan, n_slabs, src_ref, src_lead, twc_ref, tws_ref, abuf):
    n2, R = plan.n2, FFT_FWD_ROWS
    lanes = abuf.shape[-1]

    def body(r, carry):
        off = pl.multiple_of(r * R, R)
        for l0 in range(0, lanes, TW_LANES):
            ls = slice(l0, l0 + TW_LANES)
            xs = [src_ref[src_lead + (pl.ds(m * n2 + off, R), ls)] if m < n_slabs else None
                  for m in range(plan.n1)]
            for k1, (ar, ai) in enumerate(_rdft_half(xs)):
                if k1 == 0:
                    abuf[0, pl.ds(off, R), ls] = ar.astype(BF16)
                    continue
                tc = twc_ref[k1, pl.ds(off, R), :]
                ts = tws_ref[k1, pl.ds(off, R), :]
                abuf[k1, pl.ds(off, R), ls] = _lin2(1.0, ar * tc, -1.0, None if ai is None else ai * ts).astype(BF16)
                abuf[k1, pl.ds(n2 + off, R), ls] = _lin2(1.0, ar * ts, 1.0, None if ai is None else ai * tc).astype(BF16)
        return carry

    lax.fori_loop(0, n2 // R, body, 0)


def _fftconv_kernel(z_ref, gate_ref, bias_ref, h_ref, fwd_ref, inv_ref, twc_ref, tws_ref,
                    o_ref, abuf, ybuf, b_scr, acc_scr, *, plan):
    n2, L, R = plan.n2, plan.L, FFT_ROWS
    lanes = o_ref.shape[-1]
    inv_n = 1.0 / plan.n

    def dots(k1, slot):
        if k1 == 0:
            xf = jnp.dot(fwd_ref[:, 0:n2], abuf[0, 0:n2], preferred_element_type=F32)
        else:
            xf = jnp.dot(fwd_ref[...], abuf[k1], preferred_element_type=F32)
        xr, xi = xf[:n2], xf[n2:]
        hr, hi = h_ref[k1, 0:n2], h_ref[k1, n2:]
        ys = slot % ybuf.shape[0]
        ybuf[ys, 0:n2] = (xr * hr - xi * hi).astype(BF16)
        ybuf[ys, n2:] = (xr * hi + xi * hr).astype(BF16)
        if k1 == 0:
            b_scr[slot, 0:n2] = jnp.dot(inv_ref[0:n2, :], ybuf[ys], preferred_element_type=F32)
        else:
            b_scr[slot] = jnp.dot(inv_ref[...], ybuf[ys], preferred_element_type=F32)

    if plan.n1 == 1:
        hr, hi = h_ref[0, 0:n2], h_ref[0, n2:]
        for s in range(z_ref.shape[0]):
            z = z_ref[s]
            xf = jnp.dot(fwd_ref[:, 0:L], z.astype(BF16), preferred_element_type=F32)
            xr, xi = xf[:n2], xf[n2:]
            ybuf[s % 2, 0:n2] = (xr * hr - xi * hi).astype(BF16)
            ybuf[s % 2, n2:] = (xr * hi + xi * hr).astype(BF16)
            b = jnp.dot(inv_ref[0:L, :], ybuf[s % 2], preferred_element_type=F32)
            y = b * inv_n + bias_ref[...] * z
            o_ref[s] = (gate_ref[s] * y).astype(o_ref.dtype)
        return

    n_slots = b_scr.shape[0]

    def inverse(group, slots, first, last):
        wk = plan.class_weight(group[-1])
        for off in range(0, n2, R):
            vals = []
            for k1, slot in zip(group, slots):
                br = b_scr[slot, off:off + R, :]
                if k1 == 0:
                    vals.append((br, None))
                    continue
                bi = b_scr[slot, n2 + off:n2 + off + R, :]
                tc = _lane_tile(twc_ref[k1, off:off + R, :], lanes)
                ts = _lane_tile(tws_ref[k1, off:off + R, :], lanes)
                vals.append((br * tc + bi * ts, bi * tc - br * ts))
            if len(group) == 2:
                (ar, ai), (br_, bi_) = vals
                by_parity = [(_lin2(1.0, ar, 1.0, br_), _lin2(1.0, ai, -1.0, bi_)),
                             (_lin2(1.0, ar, -1.0, br_), _lin2(1.0, ai, 1.0, bi_))]
            else:
                by_parity = [vals[0], vals[0]]
            for m in range(plan.data_slabs):
                c, s = plan.stage1_coef(m, group[0])
                pr, pi = by_parity[m % 2]
                t = _lin2(wk * c, pr, wk * s, pi)
                rows = slice(m * n2 + off, m * n2 + off + R)
                if last:
                    z = z_ref[0, rows, :]
                    tot = t if first else acc_scr[rows, :] + t
                    y = tot * inv_n + bias_ref[...] * z
                    o_ref[0, rows, :] = (gate_ref[0, rows, :] * y).astype(o_ref.dtype)
                elif first:
                    acc_scr[rows, :] = t
                else:
                    acc_scr[rows, :] += t

    _fft_forward_all(plan, plan.data_slabs, z_ref, (0,), twc_ref, tws_ref, abuf)
    groups = plan.class_groups()
    issued = 0
    pending = None
    for gi, group in enumerate(groups):
        slots = []
        for k1 in group:
            slots.append(issued % n_slots)
            dots(k1, issued % n_slots)
            issued += 1
        if pending is not None:
            inverse(*pending, first=pending_first, last=False)
        pending, pending_first = (group, slots), gi == 0
    inverse(*pending, first=len(groups) == 1, last=True)


def _fftconv(plan, z, z_col, gate, gate_col, bias, hspec, order, n_seq, lanes, out_dtype):
    L, n2 = plan.L, plan.n2
    nc = D_MODEL // lanes
    fwd, inv, twc, tws = _dft_consts(plan.n1, n2)
    zc, gc = z_col // lanes, gate_col // lanes
    acc_rows = L if plan.n1 > 1 else 8
    sb = 1 if plan.n1 > 1 else min(n_seq, 4)
    return pl.pallas_call(
        functools.partial(_fftconv_kernel, plan=plan),
        grid=(nc, n_seq // sb),
        in_specs=[
            pl.BlockSpec((sb, L, lanes), lambda c, b: (b, 0, zc + c)),
            pl.BlockSpec((sb, L, lanes), lambda c, b: (b, 0, gc + c)),
            pl.BlockSpec((None, 1, lanes), lambda c, b: (order, 0, c)),
            pl.BlockSpec((None, plan.classes, 2 * n2, lanes), lambda c, b: (order, 0, 0, c),
                         pipeline_mode=pl.Buffered(1)),
            _const_spec((2 * n2, 2 * n2)),
            _const_spec((2 * n2, 2 * n2)),
            _const_spec((plan.classes, n2, TW_LANES)),
            _const_spec((plan.classes, n2, TW_LANES)),
        ],
        out_specs=pl.BlockSpec((sb, L, lanes), lambda c, b: (b, 0, c)),
        out_shape=jax.ShapeDtypeStruct((n_seq, L, D_MODEL), out_dtype),
        scratch_shapes=[
            pltpu.VMEM((plan.classes, 2 * n2, lanes) if plan.n1 > 1 else (1, 16, lanes), BF16),
            pltpu.VMEM((2, 2 * n2, lanes), BF16),
            pltpu.VMEM((4, 2 * n2, lanes), F32),
            pltpu.VMEM((acc_rows, lanes), F32),
        ],
        compiler_params=_cparams(("parallel", "arbitrary")),
        name="hyena_fftconv",
    )(z, gate, bias.reshape(-1, 1, D_MODEL), hspec, jnp.asarray(fwd).astype(BF16),
      jnp.asarray(inv).astype(BF16),
      jnp.asarray(twc), jnp.asarray(tws))


@functools.lru_cache(maxsize=None)
def _filter_tables(L):
    r = np.arange(2 * L)
    p = np.where(r < L, r, 2 * L - r)
    p[L] = 0
    t = p / (L - 1.0)
    w = 2.0 * np.pi * p / L
    f = np.linspace(1e-4, HY_BANDS - 1, HY_BANDS)
    feat = np.concatenate([t[:, None], np.cos(np.outer(w, f)), -np.sin(np.outer(w, f))], axis=1)
    feat = np.pad(feat, ((0, 0), (0, HY_EMB_PAD - HY_EMB))).astype(np.float32)
    t_tab = np.repeat(t[:, None], TW_LANES, axis=1).astype(np.float32)
    return feat, t_tab


def _filter_hidden_kernel(feat_ref, w1_ref, b1_ref, w2_ref, b2_ref, fq_ref, o_ref):
    hp = lax.Precision.HIGHEST
    h = jnp.sin(fq_ref[...] * (jnp.dot(feat_ref[...], w1_ref[...], preferred_element_type=F32,
                                        precision=hp) + b1_ref[...]))
    h = jnp.sin(fq_ref[...] * (jnp.dot(h, w2_ref[...], preferred_element_type=F32,
                                        precision=hp) + b2_ref[...]))
    o_ref[...] = h


def _filter_hidden(feat, w1, b1, w2, b2, freq):
    rows = feat.shape[0]
    br = 1024 if rows % 1024 == 0 else rows
    vec = pl.BlockSpec((1, HY_FH), lambda i: (0, 0))
    return pl.pallas_call(
        _filter_hidden_kernel,
        grid=(rows // br,),
        in_specs=[
            pl.BlockSpec((br, HY_EMB_PAD), lambda i: (i, 0)),
            pl.BlockSpec((HY_EMB_PAD, HY_FH), lambda i: (0, 0)),
            vec,
            pl.BlockSpec((HY_FH, HY_FH), lambda i: (0, 0)),
            vec, vec,
        ],
        out_specs=pl.BlockSpec((br, HY_FH), lambda i: (i, 0)),
        out_shape=jax.ShapeDtypeStruct((rows, HY_FH), F32),
        compiler_params=_cparams(("parallel",)),
        name="hyena_filter_hidden",
    )(feat, w1, b1, w2, b2, freq)


def _dot_bf16x3(a, b):
    a_hi = a.astype(BF16)
    a_lo = (a - a_hi.astype(F32)).astype(BF16)
    b_hi = b.astype(BF16)
    b_lo = (b - b_hi.astype(F32)).astype(BF16)
    return (jnp.dot(a_hi, b_hi, preferred_element_type=F32)
            + (jnp.dot(a_hi, b_lo, preferred_element_type=F32)
               + jnp.dot(a_lo, b_hi, preferred_element_type=F32)))


def _filter_spec_kernel(hid_ref, w3a_ref, w3b_ref, t_ref, delta_ref, fwd_ref, twc_ref, tws_ref,
                        o_ref, f_scr, abuf, *, plan):
    L, n2 = plan.L, plan.n2
    lanes = o_ref.shape[-1]
    absd = jnp.abs(delta_ref[...])
    rows_per = min(L, 512)

    def gen(r, carry):
        for base, w_ref in ((0, w3a_ref), (L, w3b_ref)):
            rows = pl.ds(pl.multiple_of(base + r * rows_per, rows_per), rows_per)
            f = _dot_bf16x3(hid_ref[rows, :], w_ref[0])
            f_scr[rows, :] = f * jnp.exp(-_lane_tile(t_ref[rows, :], lanes) * absd)
        return carry

    lax.fori_loop(0, L // rows_per, gen, 0)
    f_scr[L:L + 8] = jnp.where(lax.broadcasted_iota(jnp.int32, (8, 1), 0) > 0, f_scr[L:L + 8], 0.0)

    if plan.n1 == 1:
        abuf[0, 0:n2] = f_scr[...].astype(BF16)
    else:
        _fft_forward_all(plan, plan.n1, f_scr, (), twc_ref, tws_ref, abuf)
    o_ref[0, 0] = jnp.dot(fwd_ref[:, 0:n2], abuf[0, 0:n2], preferred_element_type=F32)
    for k1 in range(1, plan.classes):
        o_ref[0, k1] = jnp.dot(fwd_ref[...], abuf[k1], preferred_element_type=F32)


def _filter_spectrum(plan, hidden, w_f3, lanes):
    L, n2 = plan.L, plan.n2
    nc = D_MODEL // lanes
    fwd, _, twc, tws = _dft_consts(plan.n1, n2)
    _, t_tab = _filter_tables(L)
    delta = np.linspace(HY_MIN_DECAY, HY_MAX_DECAY, D_MODEL).astype(np.float32)[None, :]
    w3 = w_f3.reshape(HY_FH, 2, 2, D_MODEL).transpose(1, 2, 0, 3)
    return pl.pallas_call(
        functools.partial(_filter_spec_kernel, plan=plan),
        grid=(2, nc),
        in_specs=[
            _const_spec((2 * L, HY_FH)),
            pl.BlockSpec((None, 1, HY_FH, lanes), lambda o, c: (0, o, 0, c)),
            pl.BlockSpec((None, 1, HY_FH, lanes), lambda o, c: (1, o, 0, c)),
            _const_spec((2 * L, TW_LANES)),
            pl.BlockSpec((1, lanes), lambda o, c: (0, c)),
            _const_spec((2 * n2, 2 * n2)),
            _const_spec((plan.classes, n2, TW_LANES)),
            _const_spec((plan.classes, n2, TW_LANES)),
        ],
        out_specs=pl.BlockSpec((1, plan.classes, 2 * n2, lanes), lambda o, c: (o, 0, 0, c)),
        out_shape=jax.ShapeDtypeStruct((2, plan.classes, 2 * n2, D_MODEL), F32),
        scratch_shapes=[pltpu.VMEM((2 * L, lanes), F32),
                        pltpu.VMEM((plan.classes, 2 * n2, lanes), BF16)],
        compiler_params=_cparams(("parallel", "parallel")),
        name="hyena_filter_spectrum",
    )(hidden, w3, w3, jnp.asarray(t_tab), jnp.asarray(delta), jnp.asarray(fwd).astype(BF16),
      jnp.asarray(twc), jnp.asarray(tws))


@functools.lru_cache(maxsize=None)
def _rope_tables(L):
    t = np.arange(L)
    inv = ROPE_THETA ** (-np.arange(ROPE_PAIRS, dtype=np.float32) / ROPE_PAIRS)
    ang_row = (t // GRID_W).astype(np.float32)[:, None] * inv
    ang_col = (t % GRID_W).astype(np.float32)[:, None] * inv
    zero = np.zeros_like(ang_row)
    cos = np.concatenate([np.cos(ang_row)] * 2 + [np.cos(ang_col)] * 2, axis=1)
    sa = np.concatenate([-np.sin(ang_row), zero, -np.sin(ang_col), zero], axis=1)
    sb = np.concatenate([zero, np.sin(ang_row), zero, np.sin(ang_col)], axis=1)
    variants = lambda t, ident: np.stack([t * Q_SCALE, t, np.full_like(t, ident)]).astype(np.float32)
    return variants(cos, 1.0), variants(sa, 0.0), variants(sb, 0.0)


def _hyena_mixer(x, g, shift, scale, gate, w_in, conv_w, hspec, bias, w_out, seq_len, lanes, bm):
    G, S, D = x.shape
    plan = _FftPlan(seq_len)
    n_seq = G * S // seq_len
    u = _hyena_in(x, g, shift, scale, w_in, conv_w, seq_len, bm, HYENA_IN_BN)
    u = u.reshape(n_seq, seq_len, 3 * D)
    z1 = _fftconv(plan, u, 2 * D, u, 0, bias, hspec, 0, n_seq, lanes, F32)
    z2 = _fftconv(plan, z1, 0, u, D, bias, hspec, 1, n_seq, lanes, BF16)
    return _proj_residual(z2.reshape(G * S, D), w_out, x, gate, bm)


def kernel(x_prompt, x_sample, cache_k, cache_v, c, c_ctx, w_mod, b_mod, norm_mix, norm_ffn, norm_final, w_qkv, w_o, attn_sink, hy_w_in, hy_conv, hy_w_f1, hy_b_f1, hy_w_f2, hy_b_f2, hy_w_f3, hy_freq, hy_bias, hy_w_out, ffn_w_up, ffn_conv, ffn_w_down):
    B, SEQ, D = x_prompt.shape
    DB, DSEQ, _ = x_sample.shape
    depth = w_mod.shape[0]

    n_cond = DB + 1
    cvec = jnp.concatenate([c, c_ctx[None, :], jnp.zeros((16 - n_cond, D), F32)], axis=0)
    mod = _modulation(cvec, w_mod, b_mod).reshape(depth, 16, 6, 1, D)

    xs = x_sample
    xp = x_prompt.reshape(1, B * SEQ, D)
    streams = [
        dict(x=xs, rows=slice(0, DB), seq=DSEQ, n_seq=DB, lanes=256),
        dict(x=xp, rows=slice(DB, DB + 1), seq=SEQ, n_seq=B, lanes=256),
    ]
    bm = 1024
    rope_tabs = tuple(jnp.asarray(t) for t in _rope_tables(DSEQ))
    new_k, new_v = [], []

    for i in range(depth):
        g_mix = norm_mix[i][None, :]
        g_ffn = norm_ffn[i][None, :]
        if i % 2 == 0:
            a = i // 2
            wq = _weight_bf16(w_qkv, a)
            wo = _weight_bf16(w_o, a)
        else:
            hl = i // 2
            w_in = _weight_bf16(hy_w_in, hl)
            w_out = _weight_bf16(hy_w_out, hl)
            w1 = jnp.pad(hy_w_f1[hl], ((0, HY_EMB_PAD - HY_EMB), (0, 0)))
        w_up = _weight_bf16(ffn_w_up, i)
        w_down = _weight_bf16(ffn_w_down, i)
        g_final = norm_final[None, :] if i == depth - 1 else None

        for si, st in enumerate(streams):
            x = st["x"]
            m = mod[i, st["rows"]]
            sh1, sc1, g1, sh2, sc2, g2 = (m[:, t] for t in range(6))
            latent = si == 0
            if i % 2 == 0:
                res = _qkv_proj(x, g_mix, sh1, sc1, wq, rope_tabs if latent else None,
                                emit_kv=not latent, bm=bm)
                qkv = res[0]
                if latent:
                    ck = cache_k[:, a].reshape(DB, -1, KV_DIM).astype(BF16)
                    cv = cache_v[:, a].reshape(DB, -1, KV_DIM).astype(BF16)
                    att = _attention(qkv, attn_sink[a], ck, cv, DB, DSEQ, window=True)
                else:
                    kv = res[1]
                    new_k.append(kv[:, :KV_DIM].reshape(B, 1, SEQ, N_KV_HEADS, HEAD_DIM))
                    new_v.append(kv[:, KV_DIM:].reshape(B, 1, SEQ, N_KV_HEADS, HEAD_DIM))
                    ck = qkv[:, Q_DIM:Q_DIM + KV_DIM].reshape(B, SEQ, KV_DIM)
                    cv = qkv[:, Q_DIM + KV_DIM:].reshape(B, SEQ, KV_DIM)
                    att = _attention(qkv, attn_sink[a], ck, cv, B, SEQ, window=False)
                x = _proj_residual(att, wo, x, g1, bm)
            else:
                plan = _FftPlan(st["seq"])
                feat, _ = _filter_tables(st["seq"])
                hidden = _filter_hidden(jnp.asarray(feat), w1, hy_b_f1[hl][None, :], hy_w_f2[hl],
                                        hy_b_f2[hl][None, :], hy_freq[hl][None, :])
                hspec = _filter_spectrum(plan, hidden, hy_w_f3[hl], st["lanes"])
                x = _hyena_mixer(x, g_mix, sh1, sc1, g1, w_in, hy_conv[hl], hspec, hy_bias[hl],
                                 w_out, st["seq"], st["lanes"], bm)
            x = _conv_ffn(x, g_ffn, sh2, sc2, g2, w_up, ffn_conv[i], w_down, st["seq"], bm, 512,
                          g_final=g_final)
            st["x"] = x

    y_sample = streams[0]["x"]
    y_prompt = streams[1]["x"].reshape(B, SEQ, D)
    new_cache_k = new_k[0] if len(new_k) == 1 else jnp.concatenate(new_k, axis=1)
    new_cache_v = new_v[0] if len(new_v) == 1 else jnp.concatenate(new_v, axis=1)
    return (y_prompt, y_sample, new_cache_k, new_cache_v)
```

```python
import functools
import math

import jax
import jax.numpy as jnp
import numpy as np
from jax import lax
from jax.experimental import pallas as pl
from jax.experimental.pallas import tpu as pltpu

F32 = jnp.float32
BF16 = jnp.bfloat16

D_MODEL = 2048
HEAD_DIM = 128
N_HEADS = 16
N_KV_HEADS = 4
GROUP = N_HEADS // N_KV_HEADS
Q_DIM = N_HEADS * HEAD_DIM
KV_DIM = N_KV_HEADS * HEAD_DIM
QKV_DIM = Q_DIM + 2 * KV_DIM
GRID_W = 64
WINDOW = 128
ROPE_THETA = 10000.0
ROPE_PAIRS = HEAD_DIM // 4
HY_BANDS = 16
HY_EMB = 1 + 2 * HY_BANDS
HY_EMB_PAD = 40
HY_FH = 64
HY_MIN_DECAY = math.log(1e-2) / 1.5
HY_MAX_DECAY = math.log(1e-2) / 0.3
D_FF = 5632
EPS = 1e-6
NEG_INF = -1e30
LOG2E = math.log2(math.e)
Q_SCALE = HEAD_DIM ** -0.5 * LOG2E

VMEM_LIMIT = 56 * 1024 * 1024
FFN_NORM_VMEM_LIMIT = 61 * 1024 * 1024
CAST_ROWS, CAST_COLS = 2048, 1024
NORM_ROWS = 16
HALO = 16
FFT_N2 = 256
FFT_N2_SINGLE = 512
FFT_ROWS = 32
FFT_FWD_ROWS = 16
SUB_COLS = 256
HYENA_IN_BN = 1024
QKV_BN = 1024
ATTN_ROWS = 32
FFN_DOWN_COLS = 1024
TW_LANES = 128


def _cparams(sem):
    return pltpu.CompilerParams(dimension_semantics=sem, vmem_limit_bytes=VMEM_LIMIT)


def _store_modulated_norm(dst_ref, dst0, x_ref, n_rows, g_ref, sh_ref, sc_ref):
    gain = g_ref[...] * (1.0 + sc_ref[0])
    shift = sh_ref[0]
    for r in range(0, n_rows, NORM_ROWS):
        x = x_ref[0, r:r + NORM_ROWS, :]
        ms = jnp.mean(x * x, axis=-1, keepdims=True)
        y = x * lax.rsqrt(ms + EPS) * gain + shift
        dst_ref[dst0 + r:dst0 + r + NORM_ROWS, :] = y.astype(BF16)


def _cast_kernel(w_ref, o_ref):
    o_ref[...] = w_ref[...].astype(BF16)


def _weight_bf16(w, layer):
    _, rows, cols = w.shape
    br = CAST_ROWS if rows % CAST_ROWS == 0 else CAST_ROWS // 4
    bc = CAST_COLS
    return pl.pallas_call(
        _cast_kernel,
        grid=(rows // br, cols // bc),
        in_specs=[pl.BlockSpec((None, br, bc), lambda r, c: (layer, r, c))],
        out_specs=pl.BlockSpec((br, bc), lambda r, c: (r, c)),
        out_shape=jax.ShapeDtypeStruct((rows, cols), BF16),
        compiler_params=_cparams(("parallel", "parallel")),
        name="weight_cast",
    )(w)


def _mod_kernel(c_ref, w_ref, b_ref, o_ref):
    c = c_ref[...]
    s = c * (1.0 / (1.0 + jnp.exp(-c)))
    o_ref[0] = jnp.dot(s, w_ref[0], preferred_element_type=F32,
                       precision=lax.Precision.HIGHEST) + b_ref[0]


def _modulation(cvec, w_mod, b_mod):
    depth, d, n = w_mod.shape
    rows = cvec.shape[0]
    bn = 1024
    return pl.pallas_call(
        _mod_kernel,
        grid=(depth, n // bn),
        in_specs=[
            pl.BlockSpec((rows, d), lambda l, j: (0, 0)),
            pl.BlockSpec((1, d, bn), lambda l, j: (l, 0, j)),
            pl.BlockSpec((1, 1, bn), lambda l, j: (l, 0, j)),
        ],
        out_specs=pl.BlockSpec((1, rows, bn), lambda l, j: (l, 0, j)),
        out_shape=jax.ShapeDtypeStruct((depth, rows, n), F32),
        compiler_params=_cparams(("parallel", "parallel")),
        name="modulation",
    )(cvec, w_mod, b_mod.reshape(depth, 1, n))


def _qkv_kernel(*refs, rope, emit_kv):
    x_ref, g_ref, sh_ref, sc_ref, w_ref = refs[:5]
    pos = 5
    if rope:
        tabs = (refs[pos:pos + 3], refs[pos + 3:pos + 6])
        pos += 6
    o_ref = refs[pos]
    pos += 1
    if emit_kv:
        kv_ref = refs[pos]
        pos += 1
    h_scr = refs[pos]

    j = pl.program_id(2)
    n_q = Q_DIM // QKV_BN
    heads = QKV_BN // HEAD_DIM

    @pl.when(j == 0)
    def _():
        _store_modulated_norm(h_scr, 0, x_ref, h_scr.shape[0], g_ref, sh_ref, sc_ref)

    q_scale = jnp.where(j < n_q, Q_SCALE, 1.0)
    for c0 in range(0, QKV_BN, SUB_COLS):
        cols = slice(c0, c0 + SUB_COLS)
        acc = jnp.dot(h_scr[...], w_ref[:, cols], preferred_element_type=F32)
        if emit_kv:
            kv_ref[:, cols] = acc
        for h in range(SUB_COLS // HEAD_DIM):
            xh = acc[:, h * HEAD_DIM:(h + 1) * HEAD_DIM]
            if rope:
                cos_ref, sa_ref, sb_ref = tabs[c0 // (QKV_BN // 2)]
                xh = (xh * cos_ref[...] + pltpu.roll(xh, HEAD_DIM - 32, axis=1) * sa_ref[...]
                      + pltpu.roll(xh, 32, axis=1) * sb_ref[...])
            else:
                xh = xh * q_scale
            o_ref[:, c0 + h * HEAD_DIM:c0 + (h + 1) * HEAD_DIM] = xh.astype(BF16)


def _qkv_proj(x, g, shift, scale, w, rope_tabs, emit_kv, bm):
    G, S, D = x.shape
    nb = S // bm
    bn = QKV_BN
    assert 2 * KV_DIM == bn and Q_DIM % bn == 0 and bn // HEAD_DIM == 2 * N_KV_HEADS
    nj = QKV_DIM // bn
    n_q = Q_DIM // bn
    rope = rope_tabs is not None
    in_specs = [
        pl.BlockSpec((1, bm, D), lambda gi, i, j: (gi, i, 0)),
        pl.BlockSpec((1, D), lambda gi, i, j: (0, 0)),
        pl.BlockSpec((1, 1, D), lambda gi, i, j: (gi, 0, 0)),
        pl.BlockSpec((1, 1, D), lambda gi, i, j: (gi, 0, 0)),
        pl.BlockSpec((D, bn), lambda gi, i, j: (0, j)),
    ]
    args = [x, g, shift, scale, w]
    if rope:
        for other in (1, 2):
            for t in rope_tabs:
                in_specs.append(pl.BlockSpec(
                    (None, bm, HEAD_DIM),
                    lambda gi, i, j, other=other: (jnp.where(j < n_q, 0, other), i, 0)))
                args.append(t)
    out_specs = [pl.BlockSpec((bm, bn), lambda gi, i, j: (gi * nb + i, j))]
    out_shape = [jax.ShapeDtypeStruct((G * S, QKV_DIM), BF16)]
    if emit_kv:
        out_specs.append(pl.BlockSpec((bm, bn), lambda gi, i, j: (gi * nb + i, 0)))
        out_shape.append(jax.ShapeDtypeStruct((G * S, 2 * KV_DIM), F32))
    res = pl.pallas_call(
        functools.partial(_qkv_kernel, rope=rope, emit_kv=emit_kv),
        grid=(G, nb, nj),
        in_specs=in_specs,
        out_specs=out_specs,
        out_shape=out_shape,
        scratch_shapes=[pltpu.VMEM((bm, D), BF16)],
        compiler_params=_cparams(("parallel", "parallel", "arbitrary")),
        name="qkv_proj",
    )(*args)
    return res


def _attn_kernel(*refs, window, bq):
    sink_ref, q_ref = refs[:2]
    pos = 2
    if window:
        kp_ref, kc_ref, kn_ref, vp_ref, vc_ref, vn_ref, bias_ref = refs[pos:pos + 7]
        pos += 7
    ck_ref, cv_ref, o_ref, k_scr, v_scr, s_scr, p_scr, r_scr = refs[pos:pos + 8]

    n_ctx = ck_ref.shape[1]
    n_win = 3 * bq if window else 0
    tr = ATTN_ROWS

    def scores(h):
        hs = slice(h * HEAD_DIM, (h + 1) * HEAD_DIM)
        if window:
            for t, (k_ref, v_ref) in enumerate(((kp_ref, vp_ref), (kc_ref, vc_ref), (kn_ref, vn_ref))):
                k_scr[h, t * bq:(t + 1) * bq] = k_ref[:, hs]
                v_scr[h, t * bq:(t + 1) * bq] = v_ref[:, hs]
        k_scr[h, n_win:] = ck_ref[0, :, hs]
        v_scr[h, n_win:] = cv_ref[0, :, hs]
        qs = jnp.concatenate(
            [q_ref[:, (h * GROUP + g) * HEAD_DIM:(h * GROUP + g + 1) * HEAD_DIM] for g in range(GROUP)],
            axis=0)
        s_scr[h] = lax.dot_general(qs, k_scr[h], (((1,), (1,)), ((), ())),
                                   preferred_element_type=F32)

    def softmax(h):
        for r0 in range(0, GROUP * bq, tr):
            sk = sink_ref[h * GROUP + r0 // bq] * LOG2E
            s = s_scr[h, r0:r0 + tr, :]
            if window:
                q0 = r0 % bq
                b = bias_ref[0, q0:q0 + tr, :]
                s = jnp.concatenate(
                    [s[:, :bq] + b[:, :bq], s[:, bq:2 * bq],
                     s[:, 2 * bq:3 * bq] + b[:, 2 * bq:3 * bq], s[:, 3 * bq:]], axis=1)
            m = jnp.maximum(jnp.max(s, axis=-1, keepdims=True), sk)
            p = jnp.exp2(s - m)
            denom = jnp.sum(p, axis=-1, keepdims=True) + jnp.exp2(sk - m)
            p_scr[h, r0:r0 + tr, :] = p.astype(BF16)
            r_scr[h, r0:r0 + tr, :] = jnp.broadcast_to(1.0 / denom, (tr, HEAD_DIM))

    def values(h):
        o = jnp.dot(p_scr[h], v_scr[h], preferred_element_type=F32) * r_scr[h]
        for g in range(GROUP):
            c0 = (h * GROUP + g) * HEAD_DIM
            o_ref[:, c0:c0 + HEAD_DIM] = o[g * bq:(g + 1) * bq].astype(BF16)

    if window:
        for phase in (scores, softmax, values):
            for h in range(N_KV_HEADS):
                phase(h)
    else:
        for h in range(N_KV_HEADS):
            scores(h)
            softmax(h)
            values(h)


def _window_bias(n_ctx):
    i = np.arange(WINDOW)[:, None]
    j = np.arange(WINDOW)[None, :]
    prev = np.where(j >= i, 0.0, NEG_INF)
    cur = np.zeros((WINDOW, WINDOW))
    nxt = np.where(j <= i, 0.0, NEG_INF)
    dead = np.full((WINDOW, WINDOW), NEG_INF)
    ctx = np.zeros((WINDOW, n_ctx))
    first = np.concatenate([dead, cur, nxt, ctx], axis=1)
    mid = np.concatenate([prev, cur, nxt, ctx], axis=1)
    last = np.concatenate([prev, cur, dead, ctx], axis=1)
    return np.stack([first, mid, last]).astype(np.float32)


def _attention(qkv, sink, ctx_k, ctx_v, n_batch, seq, window):
    kcol = Q_DIM // KV_DIM
    vcol = kcol + 1
    n_ctx = ctx_k.shape[1]
    if window:
        bq = WINDOW
        nq = seq // bq
        in_specs = [
            pl.BlockSpec(memory_space=pltpu.SMEM),
            pl.BlockSpec((bq, Q_DIM), lambda b, n: (b * nq + n, 0)),
            pl.BlockSpec((bq, KV_DIM), lambda b, n: (b * nq + jnp.maximum(n - 1, 0), kcol)),
            pl.BlockSpec((bq, KV_DIM), lambda b, n: (b * nq + n, kcol)),
            pl.BlockSpec((bq, KV_DIM), lambda b, n: (b * nq + jnp.minimum(n + 1, nq - 1), kcol)),
            pl.BlockSpec((bq, KV_DIM), lambda b, n: (b * nq + jnp.maximum(n - 1, 0), vcol)),
            pl.BlockSpec((bq, KV_DIM), lambda b, n: (b * nq + n, vcol)),
            pl.BlockSpec((bq, KV_DIM), lambda b, n: (b * nq + jnp.minimum(n + 1, nq - 1), vcol)),
            pl.BlockSpec((1, bq, 3 * bq + n_ctx),
                         lambda b, n: (jnp.where(n == 0, 0, jnp.where(n == nq - 1, 2, 1)), 0, 0)),
            pl.BlockSpec((1, n_ctx, KV_DIM), lambda b, n: (b, 0, 0)),
            pl.BlockSpec((1, n_ctx, KV_DIM), lambda b, n: (b, 0, 0)),
        ]
        args = [sink, qkv, qkv, qkv, qkv, qkv, qkv, qkv, jnp.asarray(_window_bias(n_ctx)), ctx_k, ctx_v]
    else:
        bq = seq
        nq = 1
        in_specs = [
            pl.BlockSpec(memory_space=pltpu.SMEM),
            pl.BlockSpec((bq, Q_DIM), lambda b, n: (b, 0)),
            pl.BlockSpec((1, n_ctx, KV_DIM), lambda b, n: (b, 0, 0)),
            pl.BlockSpec((1, n_ctx, KV_DIM), lambda b, n: (b, 0, 0)),
        ]
        args = [sink, qkv, ctx_k, ctx_v]
    n_keys = (3 * bq if window else 0) + n_ctx
    return pl.pallas_call(
        functools.partial(_attn_kernel, window=window, bq=bq),
        grid=(n_batch, nq),
        in_specs=in_specs,
        out_specs=pl.BlockSpec((bq, Q_DIM), lambda b, n: (b * nq + n, 0)),
        out_shape=jax.ShapeDtypeStruct((n_batch * seq, Q_DIM), BF16),
        scratch_shapes=[
            pltpu.VMEM((N_KV_HEADS, n_keys, HEAD_DIM), BF16),
            pltpu.VMEM((N_KV_HEADS, n_keys, HEAD_DIM), BF16),
            pltpu.VMEM((N_KV_HEADS, GROUP * bq, n_keys), F32),
            pltpu.VMEM((N_KV_HEADS, GROUP * bq, n_keys), BF16),
            pltpu.VMEM((N_KV_HEADS, GROUP * bq, HEAD_DIM), F32),
        ],
        compiler_params=_cparams(("parallel", "parallel")),
        name="attention",
    )(*args)


def _proj_res_kernel(a_ref, w_ref, x_ref, gate_ref, o_ref):
    half_w = o_ref.shape[-1] // 2
    for half in range(2):
        cols = slice(half * half_w, (half + 1) * half_w)
        acc = jnp.dot(a_ref[...], w_ref[:, cols], preferred_element_type=F32)
        o_ref[0, :, cols] = x_ref[0, :, cols] + gate_ref[0, :, cols] * acc


def _proj_residual(a, w, x, gate, bm):
    G, S, D = x.shape
    K = a.shape[1]
    nb = S // bm
    bn = 1024
    return pl.pallas_call(
        _proj_res_kernel,
        grid=(G, nb, D // bn),
        in_specs=[
            pl.BlockSpec((bm, K), lambda gi, i, j: (gi * nb + i, 0)),
            pl.BlockSpec((K, bn), lambda gi, i, j: (0, j)),
            pl.BlockSpec((1, bm, bn), lambda gi, i, j: (gi, i, j)),
            pl.BlockSpec((1, 1, bn), lambda gi, i, j: (gi, 0, j)),
        ],
        out_specs=pl.BlockSpec((1, bm, bn), lambda gi, i, j: (gi, i, j)),
        out_shape=jax.ShapeDtypeStruct((G, S, D), F32),
        compiler_params=_cparams(("parallel", "parallel", "parallel")),
        name="proj_residual",
    )(a, w, x, gate)


def _fill_h_ext(h_scr, xm_ref, xp_ref, xn_ref, g_ref, sh_ref, sc_ref, bm):
    _store_modulated_norm(h_scr, 0, xp_ref, HALO, g_ref, sh_ref, sc_ref)
    _store_modulated_norm(h_scr, HALO, xm_ref, bm, g_ref, sh_ref, sc_ref)
    _store_modulated_norm(h_scr, HALO + bm, xn_ref, HALO, g_ref, sh_ref, sc_ref)


def _dwconv3_rows(u_ext, cw, row0, bm, seq_len):
    assert seq_len & (seq_len - 1) == 0
    pos = (row0 + lax.broadcasted_iota(jnp.int32, (bm, 1), 0)) & (seq_len - 1)
    prev = jnp.where(pos == 0, 0.0, u_ext[HALO - 1:HALO - 1 + bm])
    nxt = jnp.where(pos == seq_len - 1, 0.0, u_ext[HALO + 1:HALO + 1 + bm])
    return prev * cw[0:1] + u_ext[HALO:HALO + bm] * cw[1:2] + nxt * cw[2:3]


def _halo_specs(bm, D, S):
    nh = bm // HALO
    last = S // HALO - 1
    return [
        pl.BlockSpec((1, bm, D), lambda gi, i, j: (gi, i, 0)),
        pl.BlockSpec((1, HALO, D), lambda gi, i, j: (gi, jnp.maximum(i * nh - 1, 0), 0)),
        pl.BlockSpec((1, HALO, D), lambda gi, i, j: (gi, jnp.minimum((i + 1) * nh, last), 0)),
    ]


def _ffn_kernel(*refs, bm, seq_len, final_norm):
    (xm_ref, xp_ref, xn_ref, g_ref, sh_ref, sc_ref, gate_ref,
     wa_ref, wb_ref, cw_ref, wd_ref) = refs[:11]
    pos = 11
    if final_norm:
        gf_ref = refs[pos]
        pos += 1
    o_ref, h_scr = refs[pos:pos + 2]

    i = pl.program_id(1)
    j = pl.program_id(2)
    D = o_ref.shape[-1]

    @pl.when(j == 0)
    def _():
        _fill_h_ext(h_scr, xm_ref, xp_ref, xn_ref, g_ref, sh_ref, sc_ref, bm)
        o_ref[0] = jnp.zeros((bm, D), F32)

    ua = jnp.dot(h_scr[...], wa_ref[...], preferred_element_type=F32)
    ub = jnp.dot(h_scr[HALO:HALO + bm], wb_ref[...], preferred_element_type=F32)
    a = _dwconv3_rows(ua, cw_ref[...], i * bm, bm, seq_len)
    act = (a * (1.0 / (1.0 + jnp.exp(-a))) * ub).astype(BF16)
    for c0 in range(0, D, FFN_DOWN_COLS):
        cols = slice(c0, c0 + FFN_DOWN_COLS)
        o_ref[0, :, cols] += jnp.dot(act, wd_ref[:, cols], preferred_element_type=F32)

    @pl.when(j == pl.num_programs(2) - 1)
    def _():
        for r in range(0, bm, NORM_ROWS):
            rows = slice(r, r + NORM_ROWS)
            y = xm_ref[0, rows, :] + gate_ref[0] * o_ref[0, rows, :]
            if final_norm:
                ms = jnp.mean(y * y, axis=-1, keepdims=True)
                y = y * lax.rsqrt(ms + EPS) * gf_ref[...]
            o_ref[0, rows, :] = y


def _conv_ffn(x, g, shift, scale, gate, w_up, conv_w, w_down, seq_len, bm, bf, g_final=None):
    G, S, D = x.shape
    nb = S // bm
    nc = D_FF // bf
    final_norm = g_final is not None
    in_specs = _halo_specs(bm, D, S) + [
        pl.BlockSpec((1, D), lambda gi, i, j: (0, 0)),
        pl.BlockSpec((1, 1, D), lambda gi, i, j: (gi, 0, 0)),
        pl.BlockSpec((1, 1, D), lambda gi, i, j: (gi, 0, 0)),
        pl.BlockSpec((1, 1, D), lambda gi, i, j: (gi, 0, 0)),
        pl.BlockSpec((D, bf), lambda gi, i, j: (0, j)),
        pl.BlockSpec((D, bf), lambda gi, i, j: (0, nc + j)),
        pl.BlockSpec((3, bf), lambda gi, i, j: (0, j)),
        pl.BlockSpec((bf, D), lambda gi, i, j: (j, 0)),
    ]
    args = [x, x, x, g, shift, scale, gate, w_up, w_up, conv_w, w_down]
    if final_norm:
        in_specs.append(pl.BlockSpec((1, D), lambda gi, i, j: (0, 0)))
        args.append(g_final)
    return pl.pallas_call(
        functools.partial(_ffn_kernel, bm=bm, seq_len=seq_len, final_norm=final_norm),
        grid=(G, nb, nc),
        in_specs=in_specs,
        out_specs=pl.BlockSpec((1, bm, D), lambda gi, i, j: (gi, i, 0)),
        out_shape=jax.ShapeDtypeStruct((G, S, D), F32),
        scratch_shapes=[pltpu.VMEM((bm + 2 * HALO, D), BF16)],
        compiler_params=pltpu.CompilerParams(
            dimension_semantics=("parallel", "parallel", "arbitrary"),
            vmem_limit_bytes=FFN_NORM_VMEM_LIMIT if final_norm else VMEM_LIMIT),
        name="conv_ffn",
    )(*args)


def _hy_in_kernel(xm_ref, xp_ref, xn_ref, g_ref, sh_ref, sc_ref, w_ref, cw_ref, o_ref, h_scr,
                  *, bm, seq_len):
    i = pl.program_id(1)
    j = pl.program_id(2)

    @pl.when(j == 0)
    def _():
        _fill_h_ext(h_scr, xm_ref, xp_ref, xn_ref, g_ref, sh_ref, sc_ref, bm)

    for c0 in range(0, o_ref.shape[-1], SUB_COLS):
        cols = slice(c0, c0 + SUB_COLS)
        u = jnp.dot(h_scr[...], w_ref[:, cols], preferred_element_type=F32)
        o_ref[0, :, cols] = _dwconv3_rows(u, cw_ref[:, cols], i * bm, bm, seq_len)


def _hyena_in(x, g, shift, scale, w_in, conv_w, seq_len, bm, bn):
    G, S, D = x.shape
    nb = S // bm
    N = w_in.shape[1]
    in_specs = _halo_specs(bm, D, S) + [
        pl.BlockSpec((1, D), lambda gi, i, j: (0, 0)),
        pl.BlockSpec((1, 1, D), lambda gi, i, j: (gi, 0, 0)),
        pl.BlockSpec((1, 1, D), lambda gi, i, j: (gi, 0, 0)),
        pl.BlockSpec((D, bn), lambda gi, i, j: (0, j)),
        pl.BlockSpec((3, bn), lambda gi, i, j: (0, j)),
    ]
    return pl.pallas_call(
        functools.partial(_hy_in_kernel, bm=bm, seq_len=seq_len),
        grid=(G, nb, N // bn),
        in_specs=in_specs,
        out_specs=pl.BlockSpec((1, bm, bn), lambda gi, i, j: (gi, i, j)),
        out_shape=jax.ShapeDtypeStruct((G, S, N), F32),
        scratch_shapes=[pltpu.VMEM((bm + 2 * HALO, D), BF16)],
        compiler_params=_cparams(("parallel", "parallel", "arbitrary")),
        name="hyena_in",
    )(x, x, x, g, shift, scale, w_in, conv_w)


class _FftPlan:
    def __init__(self, seq_len):
        self.L = seq_len
        self.n2 = FFT_N2 if 2 * seq_len > FFT_N2_SINGLE else 2 * seq_len
        self.n1 = 2 * seq_len // self.n2
        self.n = self.n1 * self.n2
        self.classes = self.n1 // 2 + 1
        self.slab = min(self.n2, seq_len)
        self.data_slabs = seq_len // self.slab

    def stage1_coef(self, m, k1):
        th = 2.0 * math.pi * ((m * k1) % self.n1) / self.n1
        return _snap(math.cos(th)), _snap(-math.sin(th))

    def class_groups(self):
        half = self.n1 // 2
        groups = [(k, half - k) for k in range((half + 1) // 2)]
        if half % 2 == 0:
            groups.append((half // 2,))
        return groups

    def class_weight(self, k1):
        if self.n1 == 1:
            return 1.0
        return 1.0 if k1 in (0, self.n1 // 2) else 2.0


def _snap(v):
    for t in (0.0, 1.0, -1.0):
        if abs(v - t) < 1e-12:
            return t
    return v


@functools.lru_cache(maxsize=None)
def _dft_consts(n1, n2):
    n = n1 * n2
    k = np.arange(n2)
    ang = 2.0 * np.pi * np.outer(k, k) / n2
    fr, fi = np.cos(ang), -np.sin(ang)
    fwd = np.block([[fr, -fi], [fi, fr]]).astype(np.float32)
    inv = np.block([[fr, fi], [-fi, fr]]).astype(np.float32)
    classes = n1 // 2 + 1
    tw = 2.0 * np.pi * np.outer(np.arange(classes), np.arange(n2)) / n
    twc = np.repeat(np.cos(tw)[:, :, None], TW_LANES, axis=2).astype(np.float32)
    tws = np.repeat((-np.sin(tw))[:, :, None], TW_LANES, axis=2).astype(np.float32)
    return fwd, inv, twc, tws


def _const_spec(shape):
    nd = len(shape)
    return pl.BlockSpec(shape, lambda *_: (0,) * nd, pipeline_mode=pl.Buffered(1))


def _lane_tile(t, lanes):
    reps = lanes // t.shape[1]
    return t if reps == 1 else jnp.concatenate([t] * reps, axis=1)


def _scale(c, x):
    return x if c == 1.0 else (-x if c == -1.0 else c * x)


def _lin2(c1, x1, c2, x2):
    if x1 is None or c1 == 0.0:
        x1 = None
    if x2 is None or c2 == 0.0:
        x2 = None
    if x1 is None and x2 is None:
        return None
    if x2 is None:
        return _scale(c1, x1)
    if x1 is None:
        return _scale(c2, x2)
    if abs(abs(c1) - abs(c2)) < 1e-12:
        s = (x1 + x2) if (c1 > 0) == (c2 > 0) else (x1 - x2)
        return _scale(c1, s)
    return c1 * x1 + c2 * x2


def _rdft_half(xs):
    n = len(xs)
    if n == 1:
        return [(xs[0], None)]
    if n == 2:
        return [(_lin2(1.0, xs[0], 1.0, xs[1]), None), (_lin2(1.0, xs[0], -1.0, xs[1]), None)]
    half, quarter = n // 2, n // 4
    ev, od = _rdft_half(xs[0::2]), _rdft_half(xs[1::2])
    out = [None] * (half + 1)
    for j in range(quarter + 1):
        th = 2.0 * math.pi * j / n
        wr, wi = _snap(math.cos(th)), _snap(-math.sin(th))
        (er, ei), (orr, oi) = ev[j], od[j]
        tr = _lin2(wr, orr, -wi, oi)
        ti = _lin2(wi, orr, wr, oi)
        out[j] = (_lin2(1.0, er, 1.0, tr), _lin2(1.0, ei, 1.0, ti))
        if half - j != j:
            out[half - j] = (_lin2(1.0, er, -1.0, tr), _lin2(1.0, ti, -1.0, ei))
    return out


def _fft_forward_all(plan, n_slabs, src_ref, src_lead, twc_ref, tws_ref, abuf):
    n2, R = plan.n2, FFT_FWD_ROWS
    lanes = abuf.shape[-1]

    def body(r, carry):
        off = pl.multiple_of(r * R, R)
        for l0 in range(0, lanes, TW_LANES):
            ls = slice(l0, l0 + TW_LANES)
            xs = [src_ref[src_lead + (pl.ds(m * n2 + off, R), ls)] if m < n_slabs else None
                  for m in range(plan.n1)]
            for k1, (ar, ai) in enumerate(_rdft_half(xs)):
                if k1 == 0:
                    abuf[0, pl.ds(off, R), ls] = ar.astype(BF16)
                    continue
                tc = twc_ref[k1, pl.ds(off, R), :]
                ts = tws_ref[k1, pl.ds(off, R), :]
                abuf[k1, pl.ds(off, R), ls] = _lin2(1.0, ar * tc, -1.0, None if ai is None else ai * ts).astype(BF16)
                abuf[k1, pl.ds(n2 + off, R), ls] = _lin2(1.0, ar * ts, 1.0, None if ai is None else ai * tc).astype(BF16)
        return carry

    lax.fori_loop(0, n2 // R, body, 0)


def _fftconv_kernel(z_ref, gate_ref, bias_ref, h_ref, fwd_ref, inv_ref, twc_ref, tws_ref,
                    o_ref, abuf, ybuf, b_scr, acc_scr, *, plan):
    n2, L, R = plan.n2, plan.L, FFT_ROWS
    lanes = o_ref.shape[-1]
    inv_n = 1.0 / plan.n

    def dots(k1, slot):
        if k1 == 0:
            xf = jnp.dot(fwd_ref[:, 0:n2], abuf[0, 0:n2], preferred_element_type=F32)
        else:
            xf = jnp.dot(fwd_ref[...], abuf[k1], preferred_element_type=F32)
        xr, xi = xf[:n2], xf[n2:]
        hr, hi = h_ref[k1, 0:n2], h_ref[k1, n2:]
        ys = slot % ybuf.shape[0]
        ybuf[ys, 0:n2] = (xr * hr - xi * hi).astype(BF16)
        ybuf[ys, n2:] = (xr * hi + xi * hr).astype(BF16)
        if k1 == 0:
            b_scr[slot, 0:n2] = jnp.dot(inv_ref[0:n2, :], ybuf[ys], preferred_element_type=F32)
        else:
            b_scr[slot] = jnp.dot(inv_ref[...], ybuf[ys], preferred_element_type=F32)

    if plan.n1 == 1:
        hr, hi = h_ref[0, 0:n2], h_ref[0, n2:]
        for s in range(z_ref.shape[0]):
            z = z_ref[s]
            xf = jnp.dot(fwd_ref[:, 0:L], z.astype(BF16), preferred_element_type=F32)
            xr, xi = xf[:n2], xf[n2:]
            ybuf[s % 2, 0:n2] = (xr * hr - xi * hi).astype(BF16)
            ybuf[s % 2, n2:] = (xr * hi + xi * hr).astype(BF16)
            b = jnp.dot(inv_ref[0:L, :], ybuf[s % 2], preferred_element_type=F32)
            y = b * inv_n + bias_ref[...] * z
            o_ref[s] = (gate_ref[s] * y).astype(o_ref.dtype)
        return

    n_slots = b_scr.shape[0]

    def inverse(group, slots, first, last):
        wk = plan.class_weight(group[-1])
        for off in range(0, n2, R):
            vals = []
            for k1, slot in zip(group, slots):
                br = b_scr[slot, off:off + R, :]
                if k1 == 0:
                    vals.append((br, None))
                    continue
                bi = b_scr[slot, n2 + off:n2 + off + R, :]
                tc = _lane_tile(twc_ref[k1, off:off + R, :], lanes)
                ts = _lane_tile(tws_ref[k1, off:off + R, :], lanes)
                vals.append((br * tc + bi * ts, bi * tc - br * ts))
            if len(group) == 2:
                (ar, ai), (br_, bi_) = vals
                by_parity = [(_lin2(1.0, ar, 1.0, br_), _lin2(1.0, ai, -1.0, bi_)),
                             (_lin2(1.0, ar, -1.0, br_), _lin2(1.0, ai, 1.0, bi_))]
            else:
                by_parity = [vals[0], vals[0]]
            for m in range(plan.data_slabs):
                c, s = plan.stage1_coef(m, group[0])
                pr, pi = by_parity[m % 2]
                t = _lin2(wk * c, pr, wk * s, pi)
                rows = slice(m * n2 + off, m * n2 + off + R)
                if last:
                    z = z_ref[0, rows, :]
                    tot = t if first else acc_scr[rows, :] + t
                    y = tot * inv_n + bias_ref[...] * z
                    o_ref[0, rows, :] = (gate_ref[0, rows, :] * y).astype(o_ref.dtype)
                elif first:
                    acc_scr[rows, :] = t
                else:
                    acc_scr[rows, :] += t

    _fft_forward_all(plan, plan.data_slabs, z_ref, (0,), twc_ref, tws_ref, abuf)
    groups = plan.class_groups()
    issued = 0
    pending = None
    for gi, group in enumerate(groups):
        slots = []
        for k1 in group:
            slots.append(issued % n_slots)
            dots(k1, issued % n_slots)
            issued += 1
        if pending is not None:
            inverse(*pending, first=pending_first, last=False)
        pending, pending_first = (group, slots), gi == 0
    inverse(*pending, first=len(groups) == 1, last=True)


def _fftconv(plan, z, z_col, gate, gate_col, bias, hspec, order, n_seq, lanes, out_dtype):
    L, n2 = plan.L, plan.n2
    nc = D_MODEL // lanes
    fwd, inv, twc, tws = _dft_consts(plan.n1, n2)
    zc, gc = z_col // lanes, gate_col // lanes
    acc_rows = L if plan.n1 > 1 else 8
    sb = 1 if plan.n1 > 1 else min(n_seq, 4)
    return pl.pallas_call(
        functools.partial(_fftconv_kernel, plan=plan),
        grid=(nc, n_seq // sb),
        in_specs=[
            pl.BlockSpec((sb, L, lanes), lambda c, b: (b, 0, zc + c)),
            pl.BlockSpec((sb, L, lanes), lambda c, b: (b, 0, gc + c)),
            pl.BlockSpec((None, 1, lanes), lambda c, b: (order, 0, c)),
            pl.BlockSpec((None, plan.classes, 2 * n2, lanes), lambda c, b: (order, 0, 0, c),
                         pipeline_mode=pl.Buffered(1)),
            _const_spec((2 * n2, 2 * n2)),
            _const_spec((2 * n2, 2 * n2)),
            _const_spec((plan.classes, n2, TW_LANES)),
            _const_spec((plan.classes, n2, TW_LANES)),
        ],
        out_specs=pl.BlockSpec((sb, L, lanes), lambda c, b: (b, 0, c)),
        out_shape=jax.ShapeDtypeStruct((n_seq, L, D_MODEL), out_dtype),
        scratch_shapes=[
            pltpu.VMEM((plan.classes, 2 * n2, lanes) if plan.n1 > 1 else (1, 16, lanes), BF16),
            pltpu.VMEM((2, 2 * n2, lanes), BF16),
            pltpu.VMEM((4, 2 * n2, lanes), F32),
            pltpu.VMEM((acc_rows, lanes), F32),
        ],
        compiler_params=_cparams(("parallel", "arbitrary")),
        name="hyena_fftconv",
    )(z, gate, bias.reshape(-1, 1, D_MODEL), hspec, jnp.asarray(fwd).astype(BF16),
      jnp.asarray(inv).astype(BF16),
      jnp.asarray(twc), jnp.asarray(tws))


@functools.lru_cache(maxsize=None)
def _filter_tables(L):
    r = np.arange(2 * L)
    p = np.where(r < L, r, 2 * L - r)
    p[L] = 0
    t = p / (L - 1.0)
    w = 2.0 * np.pi * p / L
    f = np.linspace(1e-4, HY_BANDS - 1, HY_BANDS)
    feat = np.concatenate([t[:, None], np.cos(np.outer(w, f)), -np.sin(np.outer(w, f))], axis=1)
    feat = np.pad(feat, ((0, 0), (0, HY_EMB_PAD - HY_EMB))).astype(np.float32)
    t_tab = np.repeat(t[:, None], TW_LANES, axis=1).astype(np.float32)
    return feat, t_tab


def _filter_hidden_kernel(feat_ref, w1_ref, b1_ref, w2_ref, b2_ref, fq_ref, o_ref):
    hp = lax.Precision.HIGHEST
    h = jnp.sin(fq_ref[...] * (jnp.dot(feat_ref[...], w1_ref[...], preferred_element_type=F32,
                                        precision=hp) + b1_ref[...]))
    h = jnp.sin(fq_ref[...] * (jnp.dot(h, w2_ref[...], preferred_element_type=F32,
                                        precision=hp) + b2_ref[...]))
    o_ref[...] = h


def _filter_hidden(feat, w1, b1, w2, b2, freq):
    rows = feat.shape[0]
    br = 1024 if rows % 1024 == 0 else rows
    vec = pl.BlockSpec((1, HY_FH), lambda i: (0, 0))
    return pl.pallas_call(
        _filter_hidden_kernel,
        grid=(rows // br,),
        in_specs=[
            pl.BlockSpec((br, HY_EMB_PAD), lambda i: (i, 0)),
            pl.BlockSpec((HY_EMB_PAD, HY_FH), lambda i: (0, 0)),
            vec,
            pl.BlockSpec((HY_FH, HY_FH), lambda i: (0, 0)),
            vec, vec,
        ],
        out_specs=pl.BlockSpec((br, HY_FH), lambda i: (i, 0)),
        out_shape=jax.ShapeDtypeStruct((rows, HY_FH), F32),
        compiler_params=_cparams(("parallel",)),
        name="hyena_filter_hidden",
    )(feat, w1, b1, w2, b2, freq)


def _dot_bf16x3(a, b):
    a_hi = a.astype(BF16)
    a_lo = (a - a_hi.astype(F32)).astype(BF16)
    b_hi = b.astype(BF16)
    b_lo = (b - b_hi.astype(F32)).astype(BF16)
    return (jnp.dot(a_hi, b_hi, preferred_element_type=F32)
            + (jnp.dot(a_hi, b_lo, preferred_element_type=F32)
               + jnp.dot(a_lo, b_hi, preferred_element_type=F32)))


def _filter_spec_kernel(hid_ref, w3a_ref, w3b_ref, t_ref, delta_ref, fwd_ref, twc_ref, tws_ref,
                        o_ref, f_scr, abuf, *, plan):
    L, n2 = plan.L, plan.n2
    lanes = o_ref.shape[-1]
    absd = jnp.abs(delta_ref[...])
    rows_per = min(L, 512)

    def gen(r, carry):
        for base, w_ref in ((0, w3a_ref), (L, w3b_ref)):
            rows = pl.ds(pl.multiple_of(base + r * rows_per, rows_per), rows_per)
            f = _dot_bf16x3(hid_ref[rows, :], w_ref[0])
            f_scr[rows, :] = f * jnp.exp(-_lane_tile(t_ref[rows, :], lanes) * absd)
        return carry

    lax.fori_loop(0, L // rows_per, gen, 0)
    f_scr[L:L + 8] = jnp.where(lax.broadcasted_iota(jnp.int32, (8, 1), 0) > 0, f_scr[L:L + 8], 0.0)

    if plan.n1 == 1:
        abuf[0, 0:n2] = f_scr[...].astype(BF16)
    else:
        _fft_forward_all(plan, plan.n1, f_scr, (), twc_ref, tws_ref, abuf)
    o_ref[0, 0] = jnp.dot(fwd_ref[:, 0:n2], abuf[0, 0:n2], preferred_element_type=F32)
    for k1 in range(1, plan.classes):
        o_ref[0, k1] = jnp.dot(fwd_ref[...], abuf[k1], preferred_element_type=F32)


def _filter_spectrum(plan, hidden, w_f3, lanes):
    L, n2 = plan.L, plan.n2
    nc = D_MODEL // lanes
    fwd, _, twc, tws = _dft_consts(plan.n1, n2)
    _, t_tab = _filter_tables(L)
    delta = np.linspace(HY_MIN_DECAY, HY_MAX_DECAY, D_MODEL).astype(np.float32)[None, :]
    w3 = w_f3.reshape(HY_FH, 2, 2, D_MODEL).transpose(1, 2, 0, 3)
    return pl.pallas_call(
        functools.partial(_filter_spec_kernel, plan=plan),
        grid=(2, nc),
        in_specs=[
            _const_spec((2 * L, HY_FH)),
            pl.BlockSpec((None, 1, HY_FH, lanes), lambda o, c: (0, o, 0, c)),
            pl.BlockSpec((None, 1, HY_FH, lanes), lambda o, c: (1, o, 0, c)),
            _const_spec((2 * L, TW_LANES)),
            pl.BlockSpec((1, lanes), lambda o, c: (0, c)),
            _const_spec((2 * n2, 2 * n2)),
            _const_spec((plan.classes, n2, TW_LANES)),
            _const_spec((plan.classes, n2, TW_LANES)),
        ],
        out_specs=pl.BlockSpec((1, plan.classes, 2 * n2, lanes), lambda o, c: (o, 0, 0, c)),
        out_shape=jax.ShapeDtypeStruct((2, plan.classes, 2 * n2, D_MODEL), F32),
        scratch_shapes=[pltpu.VMEM((2 * L, lanes), F32),
                        pltpu.VMEM((plan.classes, 2 * n2, lanes), BF16)],
        compiler_params=_cparams(("parallel", "parallel")),
        name="hyena_filter_spectrum",
    )(hidden, w3, w3, jnp.asarray(t_tab), jnp.asarray(delta), jnp.asarray(fwd).astype(BF16),
      jnp.asarray(twc), jnp.asarray(tws))


@functools.lru_cache(maxsize=None)
def _rope_tables(L):
    t = np.arange(L)
    inv = ROPE_THETA ** (-np.arange(ROPE_PAIRS, dtype=np.float32) / ROPE_PAIRS)
    ang_row = (t // GRID_W).astype(np.float32)[:, None] * inv
    ang_col = (t % GRID_W).astype(np.float32)[:, None] * inv
    zero = np.zeros_like(ang_row)
    cos = np.concatenate([np.cos(ang_row)] * 2 + [np.cos(ang_col)] * 2, axis=1)
    sa = np.concatenate([-np.sin(ang_row), zero, -np.sin(ang_col), zero], axis=1)
    sb = np.concatenate([zero, np.sin(ang_row), zero, np.sin(ang_col)], axis=1)
    variants = lambda t, ident: np.stack([t * Q_SCALE, t, np.full_like(t, ident)]).astype(np.float32)
    return variants(cos, 1.0), variants(sa, 0.0), variants(sb, 0.0)


def _hyena_mixer(x, g, shift, scale, gate, w_in, conv_w, hspec, bias, w_out, seq_len, lanes, bm):
    G, S, D = x.shape
    plan = _FftPlan(seq_len)
    n_seq = G * S // seq_len
    u = _hyena_in(x, g, shift, scale, w_in, conv_w, seq_len, bm, HYENA_IN_BN)
    u = u.reshape(n_seq, seq_len, 3 * D)
    z1 = _fftconv(plan, u, 2 * D, u, 0, bias, hspec, 0, n_seq, lanes, F32)
    z2 = _fftconv(plan, z1, 0, u, D, bias, hspec, 1, n_seq, lanes, BF16)
    return _proj_residual(z2.reshape(G * S, D), w_out, x, gate, bm)


def kernel(x_prompt, x_sample, cache_k, cache_v, c, c_ctx, w_mod, b_mod, norm_mix, norm_ffn, norm_final, w_qkv, w_o, attn_sink, hy_w_in, hy_conv, hy_w_f1, hy_b_f1, hy_w_f2, hy_b_f2, hy_w_f3, hy_freq, hy_bias, hy_w_out, ffn_w_up, ffn_conv, ffn_w_down):
    B, SEQ, D = x_prompt.shape
    DB, DSEQ, _ = x_sample.shape
    depth = w_mod.shape[0]

    n_cond = DB + 1
    cvec = jnp.concatenate([c, c_ctx[None, :], jnp.zeros((16 - n_cond, D), F32)], axis=0)
    mod = _modulation(cvec, w_mod, b_mod).reshape(depth, 16, 6, 1, D)

    xs = x_sample
    xp = x_prompt.reshape(1, B * SEQ, D)
    streams = [
        dict(x=xs, rows=slice(0, DB), seq=DSEQ, n_seq=DB, lanes=256),
        dict(x=xp, rows=slice(DB, DB + 1), seq=SEQ, n_seq=B, lanes=256),
    ]
    bm = 1024
    rope_tabs = tuple(jnp.asarray(t) for t in _rope_tables(DSEQ))
    new_k, new_v = [], []

    for i in range(depth):
        g_mix = norm_mix[i][None, :]
        g_ffn = norm_ffn[i][None, :]
        if i % 2 == 0:
            a = i // 2
            wq = _weight_bf16(w_qkv, a)
            wo = _weight_bf16(w_o, a)
        else:
            hl = i // 2
            w_in = _weight_bf16(hy_w_in, hl)
            w_out = _weight_bf16(hy_w_out, hl)
            w1 = jnp.pad(hy_w_f1[hl], ((0, HY_EMB_PAD - HY_EMB), (0, 0)))
        w_up = _weight_bf16(ffn_w_up, i)
        w_down = _weight_bf16(ffn_w_down, i)
        g_final = norm_final[None, :] if i == depth - 1 else None

        for si, st in enumerate(streams):
            x = st["x"]
            m = mod[i, st["rows"]]
            sh1, sc1, g1, sh2, sc2, g2 = (m[:, t] for t in range(6))
            latent = si == 0
            if i % 2 == 0:
                res = _qkv_proj(x, g_mix, sh1, sc1, wq, rope_tabs if latent else None,
                                emit_kv=not latent, bm=bm)
                qkv = res[0]
                if latent:
                    ck = cache_k[:, a].reshape(DB, -1, KV_DIM).astype(BF16)
                    cv = cache_v[:, a].reshape(DB, -1, KV_DIM).astype(BF16)
                    att = _attention(qkv, attn_sink[a], ck, cv, DB, DSEQ, window=True)
                else:
                    kv = res[1]
                    new_k.append(kv[:, :KV_DIM].reshape(B, 1, SEQ, N_KV_HEADS, HEAD_DIM))
                    new_v.append(kv[:, KV_DIM:].reshape(B, 1, SEQ, N_KV_HEADS, HEAD_DIM))
                    ck = qkv[:, Q_DIM:Q_DIM + KV_DIM].reshape(B, SEQ, KV_DIM)
                    cv = qkv[:, Q_DIM + KV_DIM:].reshape(B, SEQ, KV_DIM)
                    att = _attention(qkv, attn_sink[a], ck, cv, B, SEQ, window=False)
                x = _proj_residual(att, wo, x, g1, bm)
            else:
                plan = _FftPlan(st["seq"])
                feat, _ = _filter_tables(st["seq"])
                hidden = _filter_hidden(jnp.asarray(feat), w1, hy_b_f1[hl][None, :], hy_w_f2[hl],
                                        hy_b_f2[hl][None, :], hy_freq[hl][None, :])
                hspec = _filter_spectrum(plan, hidden, hy_w_f3[hl], st["lanes"])
                x = _hyena_mixer(x, g_mix, sh1, sc1, g1, w_in, hy_conv[hl], hspec, hy_bias[hl],
                                 w_out, st["seq"], st["lanes"], bm)
            x = _conv_ffn(x, g_ffn, sh2, sc2, g2, w_up, ffn_conv[i], w_down, st["seq"], bm, 512,
                          g_final=g_final)
            st["x"] = x

    y_sample = streams[0]["x"]
    y_prompt = streams[1]["x"].reshape(B, SEQ, D)
    new_cache_k = new_k[0] if len(new_k) == 1 else jnp.concatenate(new_k, axis=1)
    new_cache_v = new_v[0] if len(new_v) == 1 else jnp.concatenate(new_v, axis=1)
    return (y_prompt, y_sample, new_cache_k, new_cache_v)
```

```python
import functools
import math

import jax
import jax.numpy as jnp
import numpy as np
from jax import lax
from jax.experimental import pallas as pl
from jax.experimental.pallas import tpu as pltpu

F32 = jnp.float32
BF16 = jnp.bfloat16

D_MODEL = 2048
HEAD_DIM = 128
N_HEADS = 16
N_KV_HEADS = 4
GROUP = N_HEADS // N_KV_HEADS
Q_DIM = N_HEADS * HEAD_DIM
KV_DIM = N_KV_HEADS * HEAD_DIM
QKV_DIM = Q_DIM + 2 * KV_DIM
GRID_W = 64
WINDOW = 128
ROPE_THETA = 10000.0
ROPE_PAIRS = HEAD_DIM // 4
HY_BANDS = 16
HY_EMB = 1 + 2 * HY_BANDS
HY_EMB_PAD = 40
HY_FH = 64
HY_MIN_DECAY = math.log(1e-2) / 1.5
HY_MAX_DECAY = math.log(1e-2) / 0.3
D_FF = 5632
EPS = 1e-6
NEG_INF = -1e30
LOG2E = math.log2(math.e)
Q_SCALE = HEAD_DIM ** -0.5 * LOG2E

VMEM_LIMIT = 56 * 1024 * 1024
FFN_NORM_VMEM_LIMIT = 61 * 1024 * 1024
CAST_ROWS, CAST_COLS = 2048, 1024
NORM_ROWS = 16
HALO = 16
FFT_N2 = 256
FFT_N2_SINGLE = 512
FFT_ROWS = 32
FFT_FWD_ROWS = 16
SUB_COLS = 256
HYENA_IN_BN = 1024
QKV_BN = 1024
ATTN_ROWS = 32
FFN_DOWN_COLS = 1024
TW_LANES = 128


def _cparams(sem):
    return pltpu.CompilerParams(dimension_semantics=sem, vmem_limit_bytes=VMEM_LIMIT)


def _store_modulated_norm(dst_ref, dst0, x_ref, n_rows, g_ref, sh_ref, sc_ref):
    gain = g_ref[...] * (1.0 + sc_ref[0])
    shift = sh_ref[0]
    for r in range(0, n_rows, NORM_ROWS):
        x = x_ref[0, r:r + NORM_ROWS, :]
        ms = jnp.mean(x * x, axis=-1, keepdims=True)
        y = x * lax.rsqrt(ms + EPS) * gain + shift
        dst_ref[dst0 + r:dst0 + r + NORM_ROWS, :] = y.astype(BF16)


def _cast_kernel(w_ref, o_ref):
    o_ref[...] = w_ref[...].astype(BF16)


def _weight_bf16(w, layer):
    _, rows, cols = w.shape
    br = CAST_ROWS if rows % CAST_ROWS == 0 else CAST_ROWS // 4
    bc = CAST_COLS
    return pl.pallas_call(
        _cast_kernel,
        grid=(rows // br, cols // bc),
        in_specs=[pl.BlockSpec((None, br, bc), lambda r, c: (layer, r, c))],
        out_specs=pl.BlockSpec((br, bc), lambda r, c: (r, c)),
        out_shape=jax.ShapeDtypeStruct((rows, cols), BF16),
        compiler_params=_cparams(("parallel", "parallel")),
        name="weight_cast",
    )(w)


def _mod_kernel(c_ref, w_ref, b_ref, o_ref):
    c = c_ref[...]
    s = c * (1.0 / (1.0 + jnp.exp(-c)))
    o_ref[0] = jnp.dot(s, w_ref[0], preferred_element_type=F32,
                       precision=lax.Precision.HIGHEST) + b_ref[0]


def _modulation(cvec, w_mod, b_mod):
    depth, d, n = w_mod.shape
    rows = cvec.shape[0]
    bn = 1024
    return pl.pallas_call(
        _mod_kernel,
        grid=(depth, n // bn),
        in_specs=[
            pl.BlockSpec((rows, d), lambda l, j: (0, 0)),
            pl.BlockSpec((1, d, bn), lambda l, j: (l, 0, j)),
            pl.BlockSpec((1, 1, bn), lambda l, j: (l, 0, j)),
        ],
        out_specs=pl.BlockSpec((1, rows, bn), lambda l, j: (l, 0, j)),
        out_shape=jax.ShapeDtypeStruct((depth, rows, n), F32),
        compiler_params=_cparams(("parallel", "parallel")),
        name="modulation",
    )(cvec, w_mod, b_mod.reshape(depth, 1, n))


def _qkv_kernel(*refs, rope, emit_kv):
    x_ref, g_ref, sh_ref, sc_ref, w_ref = refs[:5]
    pos = 5
    if rope:
        tabs = (refs[pos:pos + 3], refs[pos + 3:pos + 6])
        pos += 6
    o_ref = refs[pos]
    pos += 1
    if emit_kv:
        kv_ref = refs[pos]
        pos += 1
    h_scr = refs[pos]

    j = pl.program_id(2)
    n_q = Q_DIM // QKV_BN
    heads = QKV_BN // HEAD_DIM

    @pl.when(j == 0)
    def _():
        _store_modulated_norm(h_scr, 0, x_ref, h_scr.shape[0], g_ref, sh_ref, sc_ref)

    q_scale = jnp.where(j < n_q, Q_SCALE, 1.0)
    for c0 in range(0, QKV_BN, SUB_COLS):
        cols = slice(c0, c0 + SUB_COLS)
        acc = jnp.dot(h_scr[...], w_ref[:, cols], preferred_element_type=F32)
        if emit_kv:
            kv_ref[:, cols] = acc
        for h in range(SUB_COLS // HEAD_DIM):
            xh = acc[:, h * HEAD_DIM:(h + 1) * HEAD_DIM]
            if rope:
                cos_ref, sa_ref, sb_ref = tabs[c0 // (QKV_BN // 2)]
                xh = (xh * cos_ref[...] + pltpu.roll(xh, HEAD_DIM - 32, axis=1) * sa_ref[...]
                      + pltpu.roll(xh, 32, axis=1) * sb_ref[...])
            else:
                xh = xh * q_scale
            o_ref[:, c0 + h * HEAD_DIM:c0 + (h + 1) * HEAD_DIM] = xh.astype(BF16)


def _qkv_proj(x, g, shift, scale, w, rope_tabs, emit_kv, bm):
    G, S, D = x.shape
    nb = S // bm
    bn = QKV_BN
    assert 2 * KV_DIM == bn and Q_DIM % bn == 0 and bn // HEAD_DIM == 2 * N_KV_HEADS
    nj = QKV_DIM // bn
    n_q = Q_DIM // bn
    rope = rope_tabs is not None
    in_specs = [
        pl.BlockSpec((1, bm, D), lambda gi, i, j: (gi, i, 0)),
        pl.BlockSpec((1, D), lambda gi, i, j: (0, 0)),
        pl.BlockSpec((1, 1, D), lambda gi, i, j: (gi, 0, 0)),
        pl.BlockSpec((1, 1, D), lambda gi, i, j: (gi, 0, 0)),
        pl.BlockSpec((D, bn), lambda gi, i, j: (0, j)),
    ]
    args = [x, g, shift, scale, w]
    if rope:
        for other in (1, 2):
            for t in rope_tabs:
                in_specs.append(pl.BlockSpec(
                    (None, bm, HEAD_DIM),
                    lambda gi, i, j, other=other: (jnp.where(j < n_q, 0, other), i, 0)))
                args.append(t)
    out_specs = [pl.BlockSpec((bm, bn), lambda gi, i, j: (gi * nb + i, j))]
    out_shape = [jax.ShapeDtypeStruct((G * S, QKV_DIM), BF16)]
    if emit_kv:
        out_specs.append(pl.BlockSpec((bm, bn), lambda gi, i, j: (gi * nb + i, 0)))
        out_shape.append(jax.ShapeDtypeStruct((G * S, 2 * KV_DIM), F32))
    res = pl.pallas_call(
        functools.partial(_qkv_kernel, rope=rope, emit_kv=emit_kv),
        grid=(G, nb, nj),
        in_specs=in_specs,
        out_specs=out_specs,
        out_shape=out_shape,
        scratch_shapes=[pltpu.VMEM((bm, D), BF16)],
        compiler_params=_cparams(("parallel", "parallel", "arbitrary")),
        name="qkv_proj",
    )(*args)
    return res


def _attn_kernel(*refs, window, bq):
    sink_ref, q_ref = refs[:2]
    pos = 2
    if window:
        kp_ref, kc_ref, kn_ref, vp_ref, vc_ref, vn_ref, bias_ref = refs[pos:pos + 7]
        pos += 7
    ck_ref, cv_ref, o_ref, k_scr, v_scr, s_scr, p_scr, r_scr = refs[pos:pos + 8]

    n_ctx = ck_ref.shape[1]
    n_win = 3 * bq if window else 0
    tr = ATTN_ROWS

    def scores(h):
        hs = slice(h * HEAD_DIM, (h + 1) * HEAD_DIM)
        if window:
            for t, (k_ref, v_ref) in enumerate(((kp_ref, vp_ref), (kc_ref, vc_ref), (kn_ref, vn_ref))):
                k_scr[h, t * bq:(t + 1) * bq] = k_ref[:, hs]
                v_scr[h, t * bq:(t + 1) * bq] = v_ref[:, hs]
        k_scr[h, n_win:] = ck_ref[0, :, hs]
        v_scr[h, n_win:] = cv_ref[0, :, hs]
        qs = jnp.concatenate(
            [q_ref[:, (h * GROUP + g) * HEAD_DIM:(h * GROUP + g + 1) * HEAD_DIM] for g in range(GROUP)],
            axis=0)
        s_scr[h] = lax.dot_general(qs, k_scr[h], (((1,), (1,)), ((), ())),
                                   preferred_element_type=F32)

    def softmax(h):
        for r0 in range(0, GROUP * bq, tr):
            sk = sink_ref[h * GROUP + r0 // bq] * LOG2E
            s = s_scr[h, r0:r0 + tr, :]
            if window:
                q0 = r0 % bq
                b = bias_ref[0, q0:q0 + tr, :]
                s = jnp.concatenate(
                    [s[:, :bq] + b[:, :bq], s[:, bq:2 * bq],
                     s[:, 2 * bq:3 * bq] + b[:, 2 * bq:3 * bq], s[:, 3 * bq:]], axis=1)
            m = jnp.maximum(jnp.max(s, axis=-1, keepdims=True), sk)
            p = jnp.exp2(s - m)
            denom = jnp.sum(p, axis=-1, keepdims=True) + jnp.exp2(sk - m)
            p_scr[h, r0:r0 + tr, :] = p.astype(BF16)
            r_scr[h, r0:r0 + tr, :] = jnp.broadcast_to(1.0 / denom, (tr, HEAD_DIM))

    def values(h):
        o = jnp.dot(p_scr[h], v_scr[h], preferred_element_type=F32) * r_scr[h]
        for g in range(GROUP):
            c0 = (h * GROUP + g) * HEAD_DIM
            o_ref[:, c0:c0 + HEAD_DIM] = o[g * bq:(g + 1) * bq].astype(BF16)

    if window:
        for step in range(N_KV_HEADS + 2):
            for phase, lag in ((scores, 0), (softmax, 1), (values, 2)):
                if 0 <= step - lag < N_KV_HEADS:
                    phase(step - lag)
    else:
        for h in range(N_KV_HEADS):
            scores(h)
            softmax(h)
            values(h)


def _window_bias(n_ctx):
    i = np.arange(WINDOW)[:, None]
    j = np.arange(WINDOW)[None, :]
    prev = np.where(j >= i, 0.0, NEG_INF)
    cur = np.zeros((WINDOW, WINDOW))
    nxt = np.where(j <= i, 0.0, NEG_INF)
    dead = np.full((WINDOW, WINDOW), NEG_INF)
    ctx = np.zeros((WINDOW, n_ctx))
    first = np.concatenate([dead, cur, nxt, ctx], axis=1)
    mid = np.concatenate([prev, cur, nxt, ctx], axis=1)
    last = np.concatenate([prev, cur, dead, ctx], axis=1)
    return np.stack([first, mid, last]).astype(np.float32)


def _attention(qkv, sink, ctx_k, ctx_v, n_batch, seq, window):
    kcol = Q_DIM // KV_DIM
    vcol = kcol + 1
    n_ctx = ctx_k.shape[1]
    if window:
        bq = WINDOW
        nq = seq // bq
        in_specs = [
            pl.BlockSpec(memory_space=pltpu.SMEM),
            pl.BlockSpec((bq, Q_DIM), lambda b, n: (b * nq + n, 0)),
            pl.BlockSpec((bq, KV_DIM), lambda b, n: (b * nq + jnp.maximum(n - 1, 0), kcol)),
            pl.BlockSpec((bq, KV_DIM), lambda b, n: (b * nq + n, kcol)),
            pl.BlockSpec((bq, KV_DIM), lambda b, n: (b * nq + jnp.minimum(n + 1, nq - 1), kcol)),
            pl.BlockSpec((bq, KV_DIM), lambda b, n: (b * nq + jnp.maximum(n - 1, 0), vcol)),
            pl.BlockSpec((bq, KV_DIM), lambda b, n: (b * nq + n, vcol)),
            pl.BlockSpec((bq, KV_DIM), lambda b, n: (b * nq + jnp.minimum(n + 1, nq - 1), vcol)),
            pl.BlockSpec((1, bq, 3 * bq + n_ctx),
                         lambda b, n: (jnp.where(n == 0, 0, jnp.where(n == nq - 1, 2, 1)), 0, 0)),
            pl.BlockSpec((1, n_ctx, KV_DIM), lambda b, n: (b, 0, 0)),
            pl.BlockSpec((1, n_ctx, KV_DIM), lambda b, n: (b, 0, 0)),
        ]
        args = [sink, qkv, qkv, qkv, qkv, qkv, qkv, qkv, jnp.asarray(_window_bias(n_ctx)), ctx_k, ctx_v]
    else:
        bq = seq
        nq = 1
        in_specs = [
            pl.BlockSpec(memory_space=pltpu.SMEM),
            pl.BlockSpec((bq, Q_DIM), lambda b, n: (b, 0)),
            pl.BlockSpec((1, n_ctx, KV_DIM), lambda b, n: (b, 0, 0)),
            pl.BlockSpec((1, n_ctx, KV_DIM), lambda b, n: (b, 0, 0)),
        ]
        args = [sink, qkv, ctx_k, ctx_v]
    n_keys = (3 * bq if window else 0) + n_ctx
    return pl.pallas_call(
        functools.partial(_attn_kernel, window=window, bq=bq),
        grid=(n_batch, nq),
        in_specs=in_specs,
        out_specs=pl.BlockSpec((bq, Q_DIM), lambda b, n: (b * nq + n, 0)),
        out_shape=jax.ShapeDtypeStruct((n_batch * seq, Q_DIM), BF16),
        scratch_shapes=[
            pltpu.VMEM((N_KV_HEADS, n_keys, HEAD_DIM), BF16),
            pltpu.VMEM((N_KV_HEADS, n_keys, HEAD_DIM), BF16),
            pltpu.VMEM((N_KV_HEADS, GROUP * bq, n_keys), F32),
            pltpu.VMEM((N_KV_HEADS, GROUP * bq, n_keys), BF16),
            pltpu.VMEM((N_KV_HEADS, GROUP * bq, HEAD_DIM), F32),
        ],
        compiler_params=_cparams(("parallel", "parallel")),
        name="attention",
    )(*args)


def _proj_res_kernel(a_ref, w_ref, x_ref, gate_ref, o_ref):
    half_w = o_ref.shape[-1] // 2
    for half in range(2):
        cols = slice(half * half_w, (half + 1) * half_w)
        acc = jnp.dot(a_ref[...], w_ref[:, cols], preferred_element_type=F32)
        o_ref[0, :, cols] = x_ref[0, :, cols] + gate_ref[0, :, cols] * acc


def _proj_residual(a, w, x, gate, bm):
    G, S, D = x.shape
    K = a.shape[1]
    nb = S // bm
    bn = 1024
    return pl.pallas_call(
        _proj_res_kernel,
        grid=(G, nb, D // bn),
        in_specs=[
            pl.BlockSpec((bm, K), lambda gi, i, j: (gi * nb + i, 0)),
            pl.BlockSpec((K, bn), lambda gi, i, j: (0, j)),
            pl.BlockSpec((1, bm, bn), lambda gi, i, j: (gi, i, j)),
            pl.BlockSpec((1, 1, bn), lambda gi, i, j: (gi, 0, j)),
        ],
        out_specs=pl.BlockSpec((1, bm, bn), lambda gi, i, j: (gi, i, j)),
        out_shape=jax.ShapeDtypeStruct((G, S, D), F32),
        compiler_params=_cparams(("parallel", "parallel", "parallel")),
        name="proj_residual",
    )(a, w, x, gate)


def _fill_h_ext(h_scr, xm_ref, xp_ref, xn_ref, g_ref, sh_ref, sc_ref, bm):
    _store_modulated_norm(h_scr, 0, xp_ref, HALO, g_ref, sh_ref, sc_ref)
    _store_modulated_norm(h_scr, HALO, xm_ref, bm, g_ref, sh_ref, sc_ref)
    _store_modulated_norm(h_scr, HALO + bm, xn_ref, HALO, g_ref, sh_ref, sc_ref)


def _dwconv3_rows(u_ext, cw, row0, bm, seq_len):
    assert seq_len & (seq_len - 1) == 0
    pos = (row0 + lax.broadcasted_iota(jnp.int32, (bm, 1), 0)) & (seq_len - 1)
    prev = jnp.where(pos == 0, 0.0, u_ext[HALO - 1:HALO - 1 + bm])
    nxt = jnp.where(pos == seq_len - 1, 0.0, u_ext[HALO + 1:HALO + 1 + bm])
    return prev * cw[0:1] + u_ext[HALO:HALO + bm] * cw[1:2] + nxt * cw[2:3]


def _halo_specs(bm, D, S):
    nh = bm // HALO
    last = S // HALO - 1
    return [
        pl.BlockSpec((1, bm, D), lambda gi, i, j: (gi, i, 0)),
        pl.BlockSpec((1, HALO, D), lambda gi, i, j: (gi, jnp.maximum(i * nh - 1, 0), 0)),
        pl.BlockSpec((1, HALO, D), lambda gi, i, j: (gi, jnp.minimum((i + 1) * nh, last), 0)),
    ]


def _ffn_kernel(*refs, bm, seq_len, final_norm):
    (xm_ref, xp_ref, xn_ref, g_ref, sh_ref, sc_ref, gate_ref,
     wa_ref, wb_ref, cw_ref, wd_ref) = refs[:11]
    pos = 11
    if final_norm:
        gf_ref = refs[pos]
        pos += 1
    o_ref, h_scr = refs[pos:pos + 2]

    i = pl.program_id(1)
    j = pl.program_id(2)
    D = o_ref.shape[-1]

    @pl.when(j == 0)
    def _():
        _fill_h_ext(h_scr, xm_ref, xp_ref, xn_ref, g_ref, sh_ref, sc_ref, bm)
        o_ref[0] = jnp.zeros((bm, D), F32)

    ua = jnp.dot(h_scr[...], wa_ref[...], preferred_element_type=F32)
    ub = jnp.dot(h_scr[HALO:HALO + bm], wb_ref[...], preferred_element_type=F32)
    a = _dwconv3_rows(ua, cw_ref[...], i * bm, bm, seq_len)
    act = (a * (1.0 / (1.0 + jnp.exp(-a))) * ub).astype(BF16)
    for c0 in range(0, D, FFN_DOWN_COLS):
        cols = slice(c0, c0 + FFN_DOWN_COLS)
        o_ref[0, :, cols] += jnp.dot(act, wd_ref[:, cols], preferred_element_type=F32)

    @pl.when(j == pl.num_programs(2) - 1)
    def _():
        for r in range(0, bm, NORM_ROWS):
            rows = slice(r, r + NORM_ROWS)
            y = xm_ref[0, rows, :] + gate_ref[0] * o_ref[0, rows, :]
            if final_norm:
                ms = jnp.mean(y * y, axis=-1, keepdims=True)
                y = y * lax.rsqrt(ms + EPS) * gf_ref[...]
            o_ref[0, rows, :] = y


def _conv_ffn(x, g, shift, scale, gate, w_up, conv_w, w_down, seq_len, bm, bf, g_final=None):
    G, S, D = x.shape
    nb = S // bm
    nc = D_FF // bf
    final_norm = g_final is not None
    in_specs = _halo_specs(bm, D, S) + [
        pl.BlockSpec((1, D), lambda gi, i, j: (0, 0)),
        pl.BlockSpec((1, 1, D), lambda gi, i, j: (gi, 0, 0)),
        pl.BlockSpec((1, 1, D), lambda gi, i, j: (gi, 0, 0)),
        pl.BlockSpec((1, 1, D), lambda gi, i, j: (gi, 0, 0)),
        pl.BlockSpec((D, bf), lambda gi, i, j: (0, j)),
        pl.BlockSpec((D, bf), lambda gi, i, j: (0, nc + j)),
        pl.BlockSpec((3, bf), lambda gi, i, j: (0, j)),
        pl.BlockSpec((bf, D), lambda gi, i, j: (j, 0)),
    ]
    args = [x, x, x, g, shift, scale, gate, w_up, w_up, conv_w, w_down]
    if final_norm:
        in_specs.append(pl.BlockSpec((1, D), lambda gi, i, j: (0, 0)))
        args.append(g_final)
    return pl.pallas_call(
        functools.partial(_ffn_kernel, bm=bm, seq_len=seq_len, final_norm=final_norm),
        grid=(G, nb, nc),
        in_specs=in_specs,
        out_specs=pl.BlockSpec((1, bm, D), lambda gi, i, j: (gi, i, 0)),
        out_shape=jax.ShapeDtypeStruct((G, S, D), F32),
        scratch_shapes=[pltpu.VMEM((bm + 2 * HALO, D), BF16)],
        compiler_params=pltpu.CompilerParams(
            dimension_semantics=("parallel", "parallel", "arbitrary"),
            vmem_limit_bytes=FFN_NORM_VMEM_LIMIT if final_norm else VMEM_LIMIT),
        name="conv_ffn",
    )(*args)


def _hy_in_kernel(xm_ref, xp_ref, xn_ref, g_ref, sh_ref, sc_ref, w_ref, cw_ref, o_ref, h_scr,
                  *, bm, seq_len):
    i = pl.program_id(1)
    j = pl.program_id(2)

    @pl.when(j == 0)
    def _():
        _fill_h_ext(h_scr, xm_ref, xp_ref, xn_ref, g_ref, sh_ref, sc_ref, bm)

    for c0 in range(0, o_ref.shape[-1], SUB_COLS):
        cols = slice(c0, c0 + SUB_COLS)
        u = jnp.dot(h_scr[...], w_ref[:, cols], preferred_element_type=F32)
        o_ref[0, :, cols] = _dwconv3_rows(u, cw_ref[:, cols], i * bm, bm, seq_len)


def _hyena_in(x, g, shift, scale, w_in, conv_w, seq_len, bm, bn):
    G, S, D = x.shape
    nb = S // bm
    N = w_in.shape[1]
    in_specs = _halo_specs(bm, D, S) + [
        pl.BlockSpec((1, D), lambda gi, i, j: (0, 0)),
        pl.BlockSpec((1, 1, D), lambda gi, i, j: (gi, 0, 0)),
        pl.BlockSpec((1, 1, D), lambda gi, i, j: (gi, 0, 0)),
        pl.BlockSpec((D, bn), lambda gi, i, j: (0, j)),
        pl.BlockSpec((3, bn), lambda gi, i, j: (0, j)),
    ]
    return pl.pallas_call(
        functools.partial(_hy_in_kernel, bm=bm, seq_len=seq_len),
        grid=(G, nb, N // bn),
        in_specs=in_specs,
        out_specs=pl.BlockSpec((1, bm, bn), lambda gi, i, j: (gi, i, j)),
        out_shape=jax.ShapeDtypeStruct((G, S, N), F32),
        scratch_shapes=[pltpu.VMEM((bm + 2 * HALO, D), BF16)],
        compiler_params=_cparams(("parallel", "parallel", "arbitrary")),
        name="hyena_in",
    )(x, x, x, g, shift, scale, w_in, conv_w)


class _FftPlan:
    def __init__(self, seq_len):
        self.L = seq_len
        self.n2 = FFT_N2 if 2 * seq_len > FFT_N2_SINGLE else 2 * seq_len
        self.n1 = 2 * seq_len // self.n2
        self.n = self.n1 * self.n2
        self.classes = self.n1 // 2 + 1
        self.slab = min(self.n2, seq_len)
        self.data_slabs = seq_len // self.slab

    def stage1_coef(self, m, k1):
        th = 2.0 * math.pi * ((m * k1) % self.n1) / self.n1
        return _snap(math.cos(th)), _snap(-math.sin(th))

    def class_groups(self):
        half = self.n1 // 2
        groups = [(k, half - k) for k in range((half + 1) // 2)]
        if half % 2 == 0:
            groups.append((half // 2,))
        return groups

    def class_weight(self, k1):
        if self.n1 == 1:
            return 1.0
        return 1.0 if k1 in (0, self.n1 // 2) else 2.0


def _snap(v):
    for t in (0.0, 1.0, -1.0):
        if abs(v - t) < 1e-12:
            return t
    return v


@functools.lru_cache(maxsize=None)
def _dft_consts(n1, n2):
    n = n1 * n2
    k = np.arange(n2)
    ang = 2.0 * np.pi * np.outer(k, k) / n2
    fr, fi = np.cos(ang), -np.sin(ang)
    fwd = np.block([[fr, -fi], [fi, fr]]).astype(np.float32)
    inv = np.block([[fr, fi], [-fi, fr]]).astype(np.float32)
    classes = n1 // 2 + 1
    tw = 2.0 * np.pi * np.outer(np.arange(classes), np.arange(n2)) / n
    twc = np.repeat(np.cos(tw)[:, :, None], TW_LANES, axis=2).astype(np.float32)
    tws = np.repeat((-np.sin(tw))[:, :, None], TW_LANES, axis=2).astype(np.float32)
    return fwd, inv, twc, tws


def _const_spec(shape):
    nd = len(shape)
    return pl.BlockSpec(shape, lambda *_: (0,) * nd, pipeline_mode=pl.Buffered(1))


def _lane_tile(t, lanes):
    reps = lanes // t.shape[1]
    return t if reps == 1 else jnp.concatenate([t] * reps, axis=1)


def _scale(c, x):
    return x if c == 1.0 else (-x if c == -1.0 else c * x)


def _lin2(c1, x1, c2, x2):
    if x1 is None or c1 == 0.0:
        x1 = None
    if x2 is None or c2 == 0.0:
        x2 = None
    if x1 is None and x2 is None:
        return None
    if x2 is None:
        return _scale(c1, x1)
    if x1 is None:
        return _scale(c2, x2)
    if abs(abs(c1) - abs(c2)) < 1e-12:
        s = (x1 + x2) if (c1 > 0) == (c2 > 0) else (x1 - x2)
        return _scale(c1, s)
    return c1 * x1 + c2 * x2


def _rdft_half(xs):
    n = len(xs)
    if n == 1:
        return [(xs[0], None)]
    if n == 2:
        return [(_lin2(1.0, xs[0], 1.0, xs[1]), None), (_lin2(1.0, xs[0], -1.0, xs[1]), None)]
    half, quarter = n // 2, n // 4
    ev, od = _rdft_half(xs[0::2]), _rdft_half(xs[1::2])
    out = [None] * (half + 1)
    for j in range(quarter + 1):
        th = 2.0 * math.pi * j / n
        wr, wi = _snap(math.cos(th)), _snap(-math.sin(th))
        (er, ei), (orr, oi) = ev[j], od[j]
        tr = _lin2(wr, orr, -wi, oi)
        ti = _lin2(wi, orr, wr, oi)
        out[j] = (_lin2(1.0, er, 1.0, tr), _lin2(1.0, ei, 1.0, ti))
        if half - j != j:
            out[half - j] = (_lin2(1.0, er, -1.0, tr), _lin2(1.0, ti, -1.0, ei))
    return out


def _fft_forward_all(plan, n_slabs, src_ref, src_lead, twc_ref, tws_ref, abuf):
    n2, R = plan.n2, FFT_FWD_ROWS
    lanes = abuf.shape[-1]

    def body(r, carry):
        off = pl.multiple_of(r * R, R)
        for l0 in range(0, lanes, TW_LANES):
            ls = slice(l0, l0 + TW_LANES)
            xs = [src_ref[src_lead + (pl.ds(m * n2 + off, R), ls)] if m < n_slabs else None
                  for m in range(plan.n1)]
            for k1, (ar, ai) in enumerate(_rdft_half(xs)):
                if k1 == 0:
                    abuf[0, pl.ds(off, R), ls] = ar.astype(BF16)
                    continue
                tc = twc_ref[k1, pl.ds(off, R), :]
                ts = tws_ref[k1, pl.ds(off, R), :]
                abuf[k1, pl.ds(off, R), ls] = _lin2(1.0, ar * tc, -1.0, None if ai is None else ai * ts).astype(BF16)
                abuf[k1, pl.ds(n2 + off, R), ls] = _lin2(1.0, ar * ts, 1.0, None if ai is None else ai * tc).astype(BF16)
        return carry

    lax.fori_loop(0, n2 // R, body, 0)


def _fftconv_kernel(z_ref, gate_ref, bias_ref, h_ref, fwd_ref, inv_ref, twc_ref, tws_ref,
                    o_ref, abuf, ybuf, b_scr, acc_scr, *, plan):
    n2, L, R = plan.n2, plan.L, FFT_ROWS
    lanes = o_ref.shape[-1]
    inv_n = 1.0 / plan.n

    def dots(k1, slot):
        if k1 == 0:
            xf = jnp.dot(fwd_ref[:, 0:n2], abuf[0, 0:n2], preferred_element_type=F32)
        else:
            xf = jnp.dot(fwd_ref[...], abuf[k1], preferred_element_type=F32)
        xr, xi = xf[:n2], xf[n2:]
        hr, hi = h_ref[k1, 0:n2], h_ref[k1, n2:]
        ys = slot % ybuf.shape[0]
        ybuf[ys, 0:n2] = (xr * hr - xi * hi).astype(BF16)
        ybuf[ys, n2:] = (xr * hi + xi * hr).astype(BF16)
        if k1 == 0:
            b_scr[slot, 0:n2] = jnp.dot(inv_ref[0:n2, :], ybuf[ys], preferred_element_type=F32)
        else:
            b_scr[slot] = jnp.dot(inv_ref[...], ybuf[ys], preferred_element_type=F32)

    if plan.n1 == 1:
        hr, hi = h_ref[0, 0:n2], h_ref[0, n2:]
        for s in range(z_ref.shape[0]):
            z = z_ref[s]
            xf = jnp.dot(fwd_ref[:, 0:L], z.astype(BF16), preferred_element_type=F32)
            xr, xi = xf[:n2], xf[n2:]
            ybuf[s % 2, 0:n2] = (xr * hr - xi * hi).astype(BF16)
            ybuf[s % 2, n2:] = (xr * hi + xi * hr).astype(BF16)
            b = jnp.dot(inv_ref[0:L, :], ybuf[s % 2], preferred_element_type=F32)
            y = b * inv_n + bias_ref[...] * z
            o_ref[s] = (gate_ref[s] * y).astype(o_ref.dtype)
        return

    n_slots = b_scr.shape[0]

    def inverse(group, slots, first, last):
        wk = plan.class_weight(group[-1])
        for off in range(0, n2, R):
            vals = []
            for k1, slot in zip(group, slots):
                br = b_scr[slot, off:off + R, :]
                if k1 == 0:
                    vals.append((br, None))
                    continue
                bi = b_scr[slot, n2 + off:n2 + off + R, :]
                tc = _lane_tile(twc_ref[k1, off:off + R, :], lanes)
                ts = _lane_tile(tws_ref[k1, off:off + R, :], lanes)
                vals.append((br * tc + bi * ts, bi * tc - br * ts))
            if len(group) == 2:
                (ar, ai), (br_, bi_) = vals
                by_parity = [(_lin2(1.0, ar, 1.0, br_), _lin2(1.0, ai, -1.0, bi_)),
                             (_lin2(1.0, ar, -1.0, br_), _lin2(1.0, ai, 1.0, bi_))]
            else:
                by_parity = [vals[0], vals[0]]
            for m in range(plan.data_slabs):
                c, s = plan.stage1_coef(m, group[0])
                pr, pi = by_parity[m % 2]
                t = _lin2(wk * c, pr, wk * s, pi)
                rows = slice(m * n2 + off, m * n2 + off + R)
                if last:
                    z = z_ref[0, rows, :]
                    tot = t if first else acc_scr[rows, :] + t
                    y = tot * inv_n + bias_ref[...] * z
                    o_ref[0, rows, :] = (gate_ref[0, rows, :] * y).astype(o_ref.dtype)
                elif first:
                    acc_scr[rows, :] = t
                else:
                    acc_scr[rows, :] += t

    _fft_forward_all(plan, plan.data_slabs, z_ref, (0,), twc_ref, tws_ref, abuf)
    groups = plan.class_groups()
    issued = 0
    pending = None
    for gi, group in enumerate(groups):
        slots = []
        for k1 in group:
            slots.append(issued % n_slots)
            dots(k1, issued % n_slots)
            issued += 1
        if pending is not None:
            inverse(*pending, first=pending_first, last=False)
        pending, pending_first = (group, slots), gi == 0
    inverse(*pending, first=len(groups) == 1, last=True)


def _fftconv(plan, z, z_col, gate, gate_col, bias, hspec, order, n_seq, lanes, out_dtype):
    L, n2 = plan.L, plan.n2
    nc = D_MODEL // lanes
    fwd, inv, twc, tws = _dft_consts(plan.n1, n2)
    zc, gc = z_col // lanes, gate_col // lanes
    acc_rows = L if plan.n1 > 1 else 8
    sb = 1 if plan.n1 > 1 else min(n_seq, 4)
    return pl.pallas_call(
        functools.partial(_fftconv_kernel, plan=plan),
        grid=(nc, n_seq // sb),
        in_specs=[
            pl.BlockSpec((sb, L, lanes), lambda c, b: (b, 0, zc + c)),
            pl.BlockSpec((sb, L, lanes), lambda c, b: (b, 0, gc + c)),
            pl.BlockSpec((None, 1, lanes), lambda c, b: (order, 0, c)),
            pl.BlockSpec((None, plan.classes, 2 * n2, lanes), lambda c, b: (order, 0, 0, c),
                         pipeline_mode=pl.Buffered(1)),
            _const_spec((2 * n2, 2 * n2)),
            _const_spec((2 * n2, 2 * n2)),
            _const_spec((plan.classes, n2, TW_LANES)),
            _const_spec((plan.classes, n2, TW_LANES)),
        ],
        out_specs=pl.BlockSpec((sb, L, lanes), lambda c, b: (b, 0, c)),
        out_shape=jax.ShapeDtypeStruct((n_seq, L, D_MODEL), out_dtype),
        scratch_shapes=[
            pltpu.VMEM((plan.classes, 2 * n2, lanes) if plan.n1 > 1 else (1, 16, lanes), BF16),
            pltpu.VMEM((2, 2 * n2, lanes), BF16),
            pltpu.VMEM((4, 2 * n2, lanes), F32),
            pltpu.VMEM((acc_rows, lanes), F32),
        ],
        compiler_params=_cparams(("parallel", "arbitrary")),
        name="hyena_fftconv",
    )(z, gate, bias.reshape(-1, 1, D_MODEL), hspec, jnp.asarray(fwd).astype(BF16),
      jnp.asarray(inv).astype(BF16),
      jnp.asarray(twc), jnp.asarray(tws))


@functools.lru_cache(maxsize=None)
def _filter_tables(L):
    r = np.arange(2 * L)
    p = np.where(r < L, r, 2 * L - r)
    p[L] = 0
    t = p / (L - 1.0)
    w = 2.0 * np.pi * p / L
    f = np.linspace(1e-4, HY_BANDS - 1, HY_BANDS)
    feat = np.concatenate([t[:, None], np.cos(np.outer(w, f)), -np.sin(np.outer(w, f))], axis=1)
    feat = np.pad(feat, ((0, 0), (0, HY_EMB_PAD - HY_EMB))).astype(np.float32)
    t_tab = np.repeat(t[:, None], TW_LANES, axis=1).astype(np.float32)
    return feat, t_tab


def _filter_hidden_kernel(feat_ref, w1_ref, b1_ref, w2_ref, b2_ref, fq_ref, o_ref):
    hp = lax.Precision.HIGHEST
    h = jnp.sin(fq_ref[...] * (jnp.dot(feat_ref[...], w1_ref[...], preferred_element_type=F32,
                                        precision=hp) + b1_ref[...]))
    h = jnp.sin(fq_ref[...] * (jnp.dot(h, w2_ref[...], preferred_element_type=F32,
                                        precision=hp) + b2_ref[...]))
    o_ref[...] = h


def _filter_hidden(feat, w1, b1, w2, b2, freq):
    rows = feat.shape[0]
    br = 1024 if rows % 1024 == 0 else rows
    vec = pl.BlockSpec((1, HY_FH), lambda i: (0, 0))
    return pl.pallas_call(
        _filter_hidden_kernel,
        grid=(rows // br,),
        in_specs=[
            pl.BlockSpec((br, HY_EMB_PAD), lambda i: (i, 0)),
            pl.BlockSpec((HY_EMB_PAD, HY_FH), lambda i: (0, 0)),
            vec,
            pl.BlockSpec((HY_FH, HY_FH), lambda i: (0, 0)),
            vec, vec,
        ],
        out_specs=pl.BlockSpec((br, HY_FH), lambda i: (i, 0)),
        out_shape=jax.ShapeDtypeStruct((rows, HY_FH), F32),
        compiler_params=_cparams(("parallel",)),
        name="hyena_filter_hidden",
    )(feat, w1, b1, w2, b2, freq)


def _dot_bf16x3(a, b):
    a_hi = a.astype(BF16)
    a_lo = (a - a_hi.astype(F32)).astype(BF16)
    b_hi = b.astype(BF16)
    b_lo = (b - b_hi.astype(F32)).astype(BF16)
    return (jnp.dot(a_hi, b_hi, preferred_element_type=F32)
            + (jnp.dot(a_hi, b_lo, preferred_element_type=F32)
               + jnp.dot(a_lo, b_hi, preferred_element_type=F32)))


def _filter_spec_kernel(hid_ref, w3a_ref, w3b_ref, t_ref, delta_ref, fwd_ref, twc_ref, tws_ref,
                        o_ref, f_scr, abuf, *, plan):
    L, n2 = plan.L, plan.n2
    lanes = o_ref.shape[-1]
    absd = jnp.abs(delta_ref[...])
    rows_per = min(L, 512)

    def gen(r, carry):
        for base, w_ref in ((0, w3a_ref), (L, w3b_ref)):
            rows = pl.ds(pl.multiple_of(base + r * rows_per, rows_per), rows_per)
            f = _dot_bf16x3(hid_ref[rows, :], w_ref[0])
            f_scr[rows, :] = f * jnp.exp(-_lane_tile(t_ref[rows, :], lanes) * absd)
        return carry

    lax.fori_loop(0, L // rows_per, gen, 0)
    f_scr[L:L + 8] = jnp.where(lax.broadcasted_iota(jnp.int32, (8, 1), 0) > 0, f_scr[L:L + 8], 0.0)

    if plan.n1 == 1:
        abuf[0, 0:n2] = f_scr[...].astype(BF16)
    else:
        _fft_forward_all(plan, plan.n1, f_scr, (), twc_ref, tws_ref, abuf)
    o_ref[0, 0] = jnp.dot(fwd_ref[:, 0:n2], abuf[0, 0:n2], preferred_element_type=F32)
    for k1 in range(1, plan.classes):
        o_ref[0, k1] = jnp.dot(fwd_ref[...], abuf[k1], preferred_element_type=F32)


def _filter_spectrum(plan, hidden, w_f3, lanes):
    L, n2 = plan.L, plan.n2
    nc = D_MODEL // lanes
    fwd, _, twc, tws = _dft_consts(plan.n1, n2)
    _, t_tab = _filter_tables(L)
    delta = np.linspace(HY_MIN_DECAY, HY_MAX_DECAY, D_MODEL).astype(np.float32)[None, :]
    w3 = w_f3.reshape(HY_FH, 2, 2, D_MODEL).transpose(1, 2, 0, 3)
    return pl.pallas_call(
        functools.partial(_filter_spec_kernel, plan=plan),
        grid=(2, nc),
        in_specs=[
            _const_spec((2 * L, HY_FH)),
            pl.BlockSpec((None, 1, HY_FH, lanes), lambda o, c: (0, o, 0, c)),
            pl.BlockSpec((None, 1, HY_FH, lanes), lambda o, c: (1, o, 0, c)),
            _const_spec((2 * L, TW_LANES)),
            pl.BlockSpec((1, lanes), lambda o, c: (0, c)),
            _const_spec((2 * n2, 2 * n2)),
            _const_spec((plan.classes, n2, TW_LANES)),
            _const_spec((plan.classes, n2, TW_LANES)),
        ],
        out_specs=pl.BlockSpec((1, plan.classes, 2 * n2, lanes), lambda o, c: (o, 0, 0, c)),
        out_shape=jax.ShapeDtypeStruct((2, plan.classes, 2 * n2, D_MODEL), F32),
        scratch_shapes=[pltpu.VMEM((2 * L, lanes), F32),
                        pltpu.VMEM((plan.classes, 2 * n2, lanes), BF16)],
        compiler_params=_cparams(("parallel", "parallel")),
        name="hyena_filter_spectrum",
    )(hidden, w3, w3, jnp.asarray(t_tab), jnp.asarray(delta), jnp.asarray(fwd).astype(BF16),
      jnp.asarray(twc), jnp.asarray(tws))


@functools.lru_cache(maxsize=None)
def _rope_tables(L):
    t = np.arange(L)
    inv = ROPE_THETA ** (-np.arange(ROPE_PAIRS, dtype=np.float32) / ROPE_PAIRS)
    ang_row = (t // GRID_W).astype(np.float32)[:, None] * inv
    ang_col = (t % GRID_W).astype(np.float32)[:, None] * inv
    zero = np.zeros_like(ang_row)
    cos = np.concatenate([np.cos(ang_row)] * 2 + [np.cos(ang_col)] * 2, axis=1)
    sa = np.concatenate([-np.sin(ang_row), zero, -np.sin(ang_col), zero], axis=1)
    sb = np.concatenate([zero, np.sin(ang_row), zero, np.sin(ang_col)], axis=1)
    variants = lambda t, ident: np.stack([t * Q_SCALE, t, np.full_like(t, ident)]).astype(np.float32)
    return variants(cos, 1.0), variants(sa, 0.0), variants(sb, 0.0)


def _hyena_mixer(x, g, shift, scale, gate, w_in, conv_w, hspec, bias, w_out, seq_len, lanes, bm):
    G, S, D = x.shape
    plan = _FftPlan(seq_len)
    n_seq = G * S // seq_len
    u = _hyena_in(x, g, shift, scale, w_in, conv_w, seq_len, bm, HYENA_IN_BN)
    u = u.reshape(n_seq, seq_len, 3 * D)
    z1 = _fftconv(plan, u, 2 * D, u, 0, bias, hspec, 0, n_seq, lanes, F32)
    z2 = _fftconv(plan, z1, 0, u, D, bias, hspec, 1, n_seq, lanes, BF16)
    return _proj_residual(z2.reshape(G * S, D), w_out, x, gate, bm)


def kernel(x_prompt, x_sample, cache_k, cache_v, c, c_ctx, w_mod, b_mod, norm_mix, norm_ffn, norm_final, w_qkv, w_o, attn_sink, hy_w_in, hy_conv, hy_w_f1, hy_b_f1, hy_w_f2, hy_b_f2, hy_w_f3, hy_freq, hy_bias, hy_w_out, ffn_w_up, ffn_conv, ffn_w_down):
    B, SEQ, D = x_prompt.shape
    DB, DSEQ, _ = x_sample.shape
    depth = w_mod.shape[0]

    n_cond = DB + 1
    cvec = jnp.concatenate([c, c_ctx[None, :], jnp.zeros((16 - n_cond, D), F32)], axis=0)
    mod = _modulation(cvec, w_mod, b_mod).reshape(depth, 16, 6, 1, D)

    xs = x_sample
    xp = x_prompt.reshape(1, B * SEQ, D)
    streams = [
        dict(x=xs, rows=slice(0, DB), seq=DSEQ, n_seq=DB, lanes=256),
        dict(x=xp, rows=slice(DB, DB + 1), seq=SEQ, n_seq=B, lanes=256),
    ]
    bm = 1024
    rope_tabs = tuple(jnp.asarray(t) for t in _rope_tables(DSEQ))
    new_k, new_v = [], []

    for i in range(depth):
        g_mix = norm_mix[i][None, :]
        g_ffn = norm_ffn[i][None, :]
        if i % 2 == 0:
            a = i // 2
            wq = _weight_bf16(w_qkv, a)
            wo = _weight_bf16(w_o, a)
        else:
            hl = i // 2
            w_in = _weight_bf16(hy_w_in, hl)
            w_out = _weight_bf16(hy_w_out, hl)
            w1 = jnp.pad(hy_w_f1[hl], ((0, HY_EMB_PAD - HY_EMB), (0, 0)))
        w_up = _weight_bf16(ffn_w_up, i)
        w_down = _weight_bf16(ffn_w_down, i)
        g_final = norm_final[None, :] if i == depth - 1 else None

        for si, st in enumerate(streams):
            x = st["x"]
            m = mod[i, st["rows"]]
            sh1, sc1, g1, sh2, sc2, g2 = (m[:, t] for t in range(6))
            latent = si == 0
            if i % 2 == 0:
                res = _qkv_proj(x, g_mix, sh1, sc1, wq, rope_tabs if latent else None,
                                emit_kv=not latent, bm=bm)
                qkv = res[0]
                if latent:
                    ck = cache_k[:, a].reshape(DB, -1, KV_DIM).astype(BF16)
                    cv = cache_v[:, a].reshape(DB, -1, KV_DIM).astype(BF16)
                    att = _attention(qkv, attn_sink[a], ck, cv, DB, DSEQ, window=True)
                else:
                    kv = res[1]
                    new_k.append(kv[:, :KV_DIM].reshape(B, 1, SEQ, N_KV_HEADS, HEAD_DIM))
                    new_v.append(kv[:, KV_DIM:].reshape(B, 1, SEQ, N_KV_HEADS, HEAD_DIM))
                    ck = qkv[:, Q_DIM:Q_DIM + KV_DIM].reshape(B, SEQ, KV_DIM)
                    cv = qkv[:, Q_DIM + KV_DIM:].reshape(B, SEQ, KV_DIM)
                    att = _attention(qkv, attn_sink[a], ck, cv, B, SEQ, window=False)
                x = _proj_residual(att, wo, x, g1, bm)
            else:
                plan = _FftPlan(st["seq"])
                feat, _ = _filter_tables(st["seq"])
                hidden = _filter_hidden(jnp.asarray(feat), w1, hy_b_f1[hl][None, :], hy_w_f2[hl],
                                        hy_b_f2[hl][None, :], hy_freq[hl][None, :])
                hspec = _filter_spectrum(plan, hidden, hy_w_f3[hl], st["lanes"])
                x = _hyena_mixer(x, g_mix, sh1, sc1, g1, w_in, hy_conv[hl], hspec, hy_bias[hl],
                                 w_out, st["seq"], st["lanes"], bm)
            x = _conv_ffn(x, g_ffn, sh2, sc2, g2, w_up, ffn_conv[i], w_down, st["seq"], bm, 512,
                          g_final=g_final)
            st["x"] = x

    y_sample = streams[0]["x"]
    y_prompt = streams[1]["x"].reshape(B, SEQ, D)
    new_cache_k = new_k[0] if len(new_k) == 1 else jnp.concatenate(new_k, axis=1)
    new_cache_v = new_v[0] if len(new_v) == 1 else jnp.concatenate(new_v, axis=1)
    return (y_prompt, y_sample, new_cache_k, new_cache_v)
```
